```python
import math
import jax, jax.numpy as jnp
from jax import lax
import numpy as np

D_MODEL = 1024
BATCH = 2
SEQ = 8192
DEPTH = 1

RWKV_HEAD = 64
D_RWKV = D_MODEL // 2
N_RWKV_HEADS = D_RWKV // RWKV_HEAD
DECAY_LORA = 64
AAA_LORA = 64
GATE_LORA = 160
LN_X_EPS = 64e-5
D_CONV = D_MODEL // 2
CONV_WIDTH = 3
N_MEM = 256
N_XHEADS = 4
XHEAD_DIM = D_MODEL // N_XHEADS
D_FF = 4 * D_MODEL
RMS_EPS = 1e-6

SPLITS = (D_RWKV, D_RWKV, D_RWKV, DECAY_LORA, AAA_LORA, GATE_LORA,
          D_CONV, D_CONV, D_CONV,
          D_MODEL, D_MODEL)
D_IN = sum(SPLITS)
RWKV_COLS = 3 * D_RWKV + DECAY_LORA + AAA_LORA + GATE_LORA

kernel_name = "hybrid_rwkv7_shortconv_xattn_block"


def rms_norm(x, g):
    xf = x.astype(jnp.float32)
    y = xf * lax.rsqrt(jnp.mean(xf * xf, axis=-1, keepdims=True) + RMS_EPS)
    return (y * g.astype(jnp.float32)).astype(x.dtype)


def token_shift(p):
    return jnp.pad(p[:, :-1], ((0, 0), (1, 0), (0, 0)))


def wkv7_scan(r, w, k, v, a, b):
    bsz, _, h, n = r.shape

    def step(state, inp):
        r_t, w_t, k_t, v_t, a_t, b_t = inp
        sa = jnp.einsum("bhvk,bhk->bhv", state, a_t)
        state = (state * w_t[:, :, None, :]
                 + sa[..., None] * b_t[:, :, None, :]
                 + v_t[..., None] * k_t[:, :, None, :])
        return state, jnp.einsum("bhvk,bhk->bhv", state, r_t)

    xs = tuple(jnp.moveaxis(t, 1, 0) for t in (r, w, k, v, a, b))
    s0 = jnp.zeros((bsz, h, n, n), jnp.float32)
    _, ys = lax.scan(step, s0, xs)
    return jnp.moveaxis(ys, 0, 1)


def rwkv7_branch(r, k, v, wd, ad, gd, w0, w_lora_w, a0, w_lora_a, w_lora_g,
                 k_k, k_a, r_k, ln_x_w, ln_x_b):
    bsz, s, _ = r.shape
    f32 = jnp.float32
    log_w = -jax.nn.softplus(-(w0 + jnp.tanh(wd) @ w_lora_w)) - 0.5
    decay = jnp.exp(-jnp.exp(log_w.astype(f32)))
    a = jax.nn.sigmoid(a0 + ad @ w_lora_a)
    g = jax.nn.sigmoid(gd) @ w_lora_g

    def heads(t):
        return t.reshape(bsz, s, N_RWKV_HEADS, RWKV_HEAD).astype(f32)

    kk = heads(k * k_k)
    kk = kk / jnp.maximum(jnp.sqrt(jnp.sum(kk * kk, axis=-1, keepdims=True)), 1e-12)
    k = k * (1.0 + (a - 1.0) * k_a)
    rh, kh, vh, ah, wh = heads(r), heads(k), heads(v), heads(a), heads(decay)
    y = wkv7_scan(rh, wh, kh, vh, -kk, kk * ah)
    mu = jnp.mean(y, axis=-1, keepdims=True)
    var = jnp.mean(jnp.square(y - mu), axis=-1, keepdims=True)
    y = (y - mu) * lax.rsqrt(var + LN_X_EPS)
    y = y.reshape(bsz, s, D_RWKV) * ln_x_w.astype(f32) + ln_x_b.astype(f32)
    bonus = jnp.sum(rh * kh * r_k.astype(f32), axis=-1, keepdims=True) * vh
    y = y + bonus.reshape(bsz, s, D_RWKV)
    return y.astype(r.dtype) * g


def short_conv_branch(bg, cg, xc, conv_w):
    u = cg * xc
    y = lax.conv_general_dilated(
        u, conv_w.astype(u.dtype), window_strides=(1,),
        padding=[(CONV_WIDTH - 1, 0)],
        dimension_numbers=("NWC", "WIO", "NWC"),
        feature_group_count=D_CONV)
    return bg * y


def memory_cross_attention(h, mem_n, w_q, w_kv, w_xo):
    bsz, s, _ = h.shape
    n_mem = mem_n.shape[1]
    q = (h @ w_q).reshape(bsz, s, N_XHEADS, XHEAD_DIM)
    k, v = jnp.split(mem_n @ w_kv, 2, axis=-1)
    k = k.reshape(bsz, n_mem, N_XHEADS, XHEAD_DIM)
    v = v.reshape(bsz, n_mem, N_XHEADS, XHEAD_DIM)
    scores = jnp.einsum("bshd,bmhd->bhsm", q, k).astype(jnp.float32) / math.sqrt(XHEAD_DIM)
    probs = jax.nn.softmax(scores, axis=-1).astype(v.dtype)
    o = jnp.einsum("bhsm,bmhd->bshd", probs, v).reshape(bsz, s, D_MODEL)
    return o @ w_xo


def setup_inputs(seed: int = 0) -> dict:
    key = jax.random.key(seed)
    ks = jax.random.split(key, 32)
    it = iter(range(32))
    f32 = jnp.float32
    L = DEPTH

    def nrm(shape, scale):
        return scale * jax.random.normal(ks[next(it)], shape, f32)

    def gain(shape):
        return 1.0 + 0.05 * jax.random.normal(ks[next(it)], shape, f32)

    return {
        "x": nrm((BATCH, SEQ, D_MODEL), 1.0),
        "mem": nrm((BATCH, N_MEM, D_MODEL), 1.0),
        "norm_mix": gain((L, D_MODEL)),
        "w_in": nrm((L, D_MODEL, D_IN), D_MODEL ** -0.5),
        "b_gate": nrm((L, 2 * D_MODEL), 0.01),
        "mu_shift": jax.random.uniform(ks[next(it)], (L, RWKV_COLS), f32),
        "w0": jax.random.uniform(ks[next(it)], (L, D_RWKV), f32, -6.0, -1.0),
        "w_lora_w": nrm((L, DECAY_LORA, D_RWKV), 0.1 * DECAY_LORA ** -0.5),
        "a0": nrm((L, D_RWKV), 0.1),
        "w_lora_a": nrm((L, AAA_LORA, D_RWKV), AAA_LORA ** -0.5),
        "w_lora_g": nrm((L, GATE_LORA, D_RWKV), GATE_LORA ** -0.5),
        "k_k": 0.85 + nrm((L, D_RWKV), 0.05),
        "k_a": gain((L, D_RWKV)),
        "r_k": nrm((L, N_RWKV_HEADS, RWKV_HEAD), 0.1),
        "ln_x_w": gain((L, D_RWKV)),
        "ln_x_b": nrm((L, D_RWKV), 0.01),
        "conv_w": nrm((L, CONV_WIDTH, 1, D_CONV), CONV_WIDTH ** -0.5),
        "w_proj_a": nrm((L, D_RWKV, D_MODEL), D_RWKV ** -0.5),
        "w_proj_b": nrm((L, D_CONV, D_MODEL), D_CONV ** -0.5),
        "w_out_mix": nrm((L, D_MODEL, D_MODEL), D_MODEL ** -0.5),
        "norm_xattn": gain((L, D_MODEL)),
        "norm_mem": gain((L, D_MODEL)),
        "w_q": nrm((L, D_MODEL, D_MODEL), D_MODEL ** -0.5),
        "w_kv": nrm((L, D_MODEL, 2 * D_MODEL), D_MODEL ** -0.5),
        "w_xo": nrm((L, D_MODEL, D_MODEL), D_MODEL ** -0.5),
        "norm_mlp": gain((L, D_MODEL)),
        "w_up": nrm((L, D_MODEL, D_FF), D_MODEL ** -0.5),
        "w_down": nrm((L, D_FF, D_MODEL), D_FF ** -0.5),
        "norm_final": gain((D_MODEL,)),
    }


def reference(x, mem, norm_mix, w_in, b_gate, mu_shift, w0, w_lora_w, a0, w_lora_a,
              w_lora_g, k_k, k_a, r_k, ln_x_w, ln_x_b, conv_w, w_proj_a, w_proj_b,
              w_out_mix, norm_xattn, norm_mem, w_q, w_kv, w_xo, norm_mlp, w_up,
              w_down, norm_final):
    offsets = np.cumsum(SPLITS)[:-1].tolist()
    for i in range(DEPTH):
        h = rms_norm(x, norm_mix[i])
        p = h @ w_in[i]
        p_rw = p[..., :RWKV_COLS]
        p_rw = p_rw + (token_shift(p_rw) - p_rw) * mu_shift[i]
        p = jnp.concatenate([p_rw, p[..., RWKV_COLS:]], axis=-1)
        r, k, v, wd, ad, gd, cb, cc, cx, ga, gb = jnp.split(p, offsets, axis=-1)
        out_a = rwkv7_branch(r, k, v, wd, ad, gd, w0[i], w_lora_w[i], a0[i], w_lora_a[i],
                             w_lora_g[i], k_k[i], k_a[i], r_k[i], ln_x_w[i], ln_x_b[i])
        out_b = short_conv_branch(cb, cc, cx, conv_w[i])
        bg_a, bg_b = jnp.split(b_gate[i], 2)
        merged = (jax.nn.sigmoid(ga + bg_a) * (out_a @ w_proj_a[i])
                  + jax.nn.sigmoid(gb + bg_b) * (out_b @ w_proj_b[i]))
        x = x + merged @ w_out_mix[i]
        mem_n = rms_norm(mem, norm_mem[i])
        x = x + memory_cross_attention(rms_norm(x, norm_xattn[i]), mem_n,
                                       w_q[i], w_kv[i], w_xo[i])
        hm = rms_norm(x, norm_mlp[i])
        x = x + jnp.square(jax.nn.relu(hm @ w_up[i])) @ w_down[i]
    return rms_norm(x, norm_final)
```

```python
import functools

import jax
import jax.numpy as jnp
from jax import lax
from jax.experimental import pallas as pl
from jax.experimental.pallas import tpu as pltpu

F32 = jnp.float32
BF16 = jnp.bfloat16

D_MODEL = 1024
D_RWKV = 512
RWKV_HEAD = 64
DECAY_LORA = 64
AAA_LORA = 64
GATE_LORA = 160
LN_X_EPS = 64e-5
D_CONV = 512
N_XHEADS = 4
XHEAD_DIM = D_MODEL // N_XHEADS
D_FF = 4 * D_MODEL
RMS_EPS = 1e-6

LANES = 128
CHUNK = 64
PAIR = 2 * RWKV_HEAD
N_PAIRS = D_RWKV // PAIR
WD_OFF, AD_OFF, GD_OFF = 3 * D_RWKV, 3 * D_RWKV + LANES, 3 * D_RWKV + 2 * LANES
RW_COLS = GD_OFF + 2 * LANES
VMEM_LIMIT = 56 * 1024 * 1024

ROW_W0, ROW_A0, ROW_KK, ROW_KA, ROW_RK, ROW_LNB, ROW_LNW, ROW_CONV = 0, 1, 2, 3, 4, 5, 6, 7
VEC_ROWS = 16


def _bdot(a, b):
    return jnp.dot(a.astype(BF16), b.astype(BF16), preferred_element_type=F32)


def _bdot_nt(a, b):
    return lax.dot_general(a.astype(BF16), b.astype(BF16), (((1,), (1,)), ((), ())),
                           preferred_element_type=F32)


def _split2(a):
    hi = a.astype(BF16)
    lo = (a - hi.astype(F32)).astype(BF16)
    return hi, lo


def _split3(a):
    hi = a.astype(BF16)
    rem = a - hi.astype(F32)
    mid = rem.astype(BF16)
    lo = (rem - mid.astype(F32)).astype(BF16)
    return hi, mid, lo


def _dot_sel_rhs(a, sel):
    hi, mid, lo = _split3(a)
    d = functools.partial(jnp.dot, preferred_element_type=F32)
    return d(hi, sel) + d(mid, sel) + d(lo, sel)


def _dot_sel_lhs(sel, a):
    hi, mid, lo = _split3(a)
    d = functools.partial(jnp.dot, preferred_element_type=F32)
    return d(sel, hi) + d(sel, mid) + d(sel, lo)


def _dot_x3(a, b_hi, b_lo):
    a_hi, a_lo = _split2(a)
    d = functools.partial(jnp.dot, preferred_element_type=F32)
    return d(a_hi, b_hi) + d(a_hi, b_lo) + d(a_lo, b_hi)


def _rms(x, g):
    return x * lax.rsqrt(jnp.mean(x * x, axis=-1, keepdims=True) + RMS_EPS) * g


def _sigmoid(x):
    return 1.0 / (1.0 + jnp.exp(-x))


def _softplus(x):
    return jnp.maximum(x, 0.0) + jnp.log(1.0 + jnp.exp(-jnp.abs(x)))


def _shift_rows(x, carry_rows, n):
    rows = lax.broadcasted_iota(jnp.int32, (x.shape[0], 1), 0)
    out = pltpu.roll(x, n, 0)
    nc = carry_rows.shape[0]
    for j in range(n):
        out = jnp.where(rows == j, carry_rows[nc - n + j:nc - n + j + 1, :], out)
    return out


def _memkv_kernel(mem_ref, g_ref, w_ref, k_ref, v_ref):
    m = _rms(mem_ref[0], g_ref[...])
    kv = _bdot(m, w_ref[...])
    k_ref[0] = kv[:, :D_MODEL].astype(BF16)
    v_ref[0] = kv[:, D_MODEL:].astype(BF16)


def _memkv(mem, g, w_kv):
    b, n, d = mem.shape
    return pl.pallas_call(
        _memkv_kernel,
        grid=(b,),
        in_specs=[pl.BlockSpec((1, n, d), lambda i: (i, 0, 0)),
                  pl.BlockSpec((1, d), lambda i: (0, 0)),
                  pl.BlockSpec((d, 2 * d), lambda i: (0, 0))],
        out_specs=[pl.BlockSpec((1, n, d), lambda i: (i, 0, 0))] * 2,
        out_shape=[jax.ShapeDtypeStruct((b, n, d), BF16)] * 2,
        compiler_params=pltpu.CompilerParams(dimension_semantics=("arbitrary",),
                                             vmem_limit_bytes=VMEM_LIMIT),
        name="memkv",
    )(mem, g, w_kv)


def _inproj_kernel(x_ref, nm_ref, wrw_ref, wconv_ref, mu_ref, vec_ref,
                   wwh_ref, wwl_ref, wah_ref, wal_ref, wgh_ref, wgl_ref, bd_ref, tri_ref,
                   rt_ref, at_ref, bt_ref, kt_ref, bh_ref, kh_ref, v_ref, wc_ref,
                   g_ref, bonus_ref, ob_ref,
                   pcarry, ucarry):
    @pl.when(pl.program_id(1) == 0)
    def _():
        pcarry[...] = jnp.zeros_like(pcarry)
        ucarry[...] = jnp.zeros_like(ucarry)

    t = x_ref.shape[1]
    h = _rms(x_ref[0], nm_ref[...]).astype(BF16)

    pc = jnp.dot(h, wconv_ref[...], preferred_element_type=F32)
    u = pc[:, D_CONV:2 * D_CONV] * pc[:, 2 * D_CONV:]
    uc = ucarry[...]
    conv = (vec_ref[ROW_CONV:ROW_CONV + 1, :] * _shift_rows(u, uc, 2)
            + vec_ref[ROW_CONV + 1:ROW_CONV + 2, :] * _shift_rows(u, uc, 1)
            + vec_ref[ROW_CONV + 2:ROW_CONV + 3, :] * u)
    ob_ref[0] = (pc[:, :D_CONV] * conv).astype(BF16)
    ucarry[...] = u[t - 8:, :]

    p = jnp.dot(h, wrw_ref[...], preferred_element_type=F32)
    ps = _shift_rows(p, pcarry[...], 1)
    pcarry[...] = p[t - 8:, :]
    xm = p + (ps - p) * mu_ref[...]
    r = xm[:, :D_RWKV]
    k = xm[:, D_RWKV:2 * D_RWKV]
    v = xm[:, 2 * D_RWKV:3 * D_RWKV]
    wd = xm[:, WD_OFF:AD_OFF]
    ad = xm[:, AD_OFF:GD_OFF]
    gd = xm[:, GD_OFF:]

    def vec(row):
        return vec_ref[row:row + 1, :]

    log_w = -_softplus(-(vec(ROW_W0) + _dot_x3(jnp.tanh(wd), wwh_ref[...], wwl_ref[...]))) - 0.5
    ld = -jnp.exp(log_w)
    a = _sigmoid(vec(ROW_A0) + _dot_x3(ad, wah_ref[...], wal_ref[...]))
    g_ref[0] = _dot_x3(_sigmoid(gd), wgh_ref[...], wgl_ref[...])
    bd = bd_ref[...]
    kk = k * vec(ROW_KK)
    kk = kk / jnp.maximum(jnp.sqrt(_dot_sel_rhs(kk * kk, bd)), 1e-12)
    k2 = k * (1.0 + (a - 1.0) * vec(ROW_KA))
    bonus_ref[0] = _dot_sel_rhs(r * k2 * vec(ROW_RK), bd) * v + vec(ROW_LNB)
    v_ref[0] = v.astype(BF16)
    na = -kk
    nb = kk * a

    tri = tri_ref[...]
    for c in range(t // CHUNK):
        sl = slice(c * CHUNK, (c + 1) * CHUNK)
        ldc = ld[sl]
        cum = _dot_sel_lhs(tri, ldc)
        tot = cum[CHUNK - 1:CHUNK, :]
        e_inc = jnp.exp(cum)
        e_inv = jnp.exp(-cum)
        e_prev = jnp.exp(cum - ldc)
        e_end = jnp.exp(tot - cum)
        rt_ref[0, sl, :] = (r[sl] * e_inc).astype(BF16)
        at_ref[0, sl, :] = (na[sl] * e_prev).astype(BF16)
        bt_ref[0, sl, :] = (nb[sl] * e_inv).astype(BF16)
        kt_ref[0, sl, :] = (k2[sl] * e_inv).astype(BF16)
        bh_ref[0, sl, :] = (nb[sl] * e_end).astype(BF16)
        kh_ref[0, sl, :] = (k2[sl] * e_end).astype(BF16)
        wc_ref[0, c] = jnp.exp(tot)


def _inproj(x, nm, wrw, wconv, mu, vec, lora, bd, tri, tile):
    b, s, d = x.shape
    nt = s // tile
    const = lambda *shape: pl.BlockSpec(shape, lambda i, j: (0,) * len(shape))
    tok = lambda w: pl.BlockSpec((1, tile, w), lambda i, j: (i, j, 0))
    outs = ([jax.ShapeDtypeStruct((b, s, D_RWKV), BF16)] * 7
            + [jax.ShapeDtypeStruct((b, s // CHUNK, 1, D_RWKV), F32)]
            + [jax.ShapeDtypeStruct((b, s, D_RWKV), F32)] * 2
            + [jax.ShapeDtypeStruct((b, s, D_CONV), BF16)])
    out_specs = ([tok(D_RWKV)] * 7
                 + [pl.BlockSpec((1, tile // CHUNK, 1, D_RWKV), lambda i, j: (i, j, 0, 0))]
                 + [tok(D_RWKV)] * 2 + [tok(D_CONV)])
    return pl.pallas_call(
        _inproj_kernel,
        grid=(b, nt),
        in_specs=[tok(d), const(1, d), const(d, RW_COLS), const(d, 3 * D_CONV),
                  const(1, RW_COLS), const(VEC_ROWS, D_RWKV)]
                 + [const(*w.shape) for w in lora]
                 + [const(D_RWKV, D_RWKV), const(CHUNK, CHUNK)],
        out_specs=out_specs,
        out_shape=outs,
        scratch_shapes=[pltpu.VMEM((8, RW_COLS), F32), pltpu.VMEM((8, D_CONV), F32)],
        compiler_params=pltpu.CompilerParams(dimension_semantics=("arbitrary", "arbitrary"),
                                             vmem_limit_bytes=VMEM_LIMIT),
        name="inproj",
    )(x, nm, wrw, wconv, mu, vec, *lora, bd, tri)


def _pair_expand(x):
    even = lax.broadcasted_iota(jnp.int32, x.shape, 1) < RWKV_HEAD
    zero = jnp.zeros_like(x)
    return jnp.concatenate([jnp.where(even, x, zero), jnp.where(even, zero, x)], axis=0)


def _wkv_kernel(rt_ref, at_ref, bt_ref, kt_ref, bh_ref, kh_ref, v_ref, wc_ref, y_ref, state):
    @pl.when(pl.program_id(0) == 0)
    def _():
        state[...] = jnp.zeros_like(state)

    nb, t, _ = rt_ref.shape
    n2 = 2 * CHUNK
    row = lax.broadcasted_iota(jnp.int32, (n2, n2), 0)
    col = lax.broadcasted_iota(jnp.int32, (n2, n2), 1)
    strict = col < row
    incl = col <= row
    eye = (col == row).astype(F32)

    for c in range(t // CHUNK):
        rows = slice(c * CHUNK, (c + 1) * CHUNK)
        for b in range(nb):
            for p in range(N_PAIRS):
                lanes = slice(p * PAIR, (p + 1) * PAIR)
                ld = lambda ref: _pair_expand(ref[b, rows, lanes])
                r2, a2, b2, k2, bh2, kh2, v2 = (ld(ref) for ref in
                                                (rt_ref, at_ref, bt_ref, kt_ref, bh_ref, kh_ref, v_ref))
                lab = jnp.where(strict, _bdot_nt(a2, b2), 0.0)
                lak = jnp.where(strict, _bdot_nt(a2, k2), 0.0)
                mrb = jnp.where(incl, _bdot_nt(r2, b2), 0.0)
                mrk = jnp.where(incl, _bdot_nt(r2, k2), 0.0)
                npow = lab
                tinv = eye + lab
                for _ in range(5):
                    npow = _bdot(npow, npow)
                    tinv = tinv + _bdot(tinv, npow)
                s = state[b, p]
                u2 = _bdot(tinv, _bdot_nt(a2, s) + _bdot(lak, v2))
                y2 = _bdot_nt(r2, s) + _bdot(mrb, u2) + _bdot(mrk, v2)
                y_ref[b, rows, lanes] = y2[:CHUNK] + y2[CHUNK:]
                state[b, p] = (s * wc_ref[b, c, :, lanes]
                               + _bdot(u2.T, bh2) + _bdot(v2.astype(F32).T, kh2))


def _wkv(rt, at, bt, kt, bh, kh, v, wc, tile):
    b, s, d = rt.shape
    tok = pl.BlockSpec((b, tile, d), lambda j: (0, j, 0))
    return pl.pallas_call(
        _wkv_kernel,
        grid=(s // tile,),
        in_specs=[tok] * 7 + [pl.BlockSpec((b, tile // CHUNK, 1, d), lambda j: (0, j, 0, 0))],
        out_specs=tok,
        out_shape=jax.ShapeDtypeStruct((b, s, d), F32),
        scratch_shapes=[pltpu.VMEM((b, N_PAIRS, PAIR, PAIR), F32)],
        compiler_params=pltpu.CompilerParams(dimension_semantics=("arbitrary",),
                                             vmem_limit_bytes=VMEM_LIMIT),
        name="wkv",
    )(rt, at, bt, kt, bh, kh, v, wc)


def _mix_kernel(x_ref, y_ref, g_ref, bonus_ref, ob_ref, k_ref, v_ref, vec_ref, nm_ref, bg_ref,
                wgate_ref, wpa_ref, wpb_ref, wom_ref, nx_ref, wq_ref, wxo_ref, bd_ref, out_ref):
    x = x_ref[0]
    h = _rms(x, nm_ref[...]).astype(BF16)
    sg = _sigmoid(jnp.dot(h, wgate_ref[...], preferred_element_type=F32) + bg_ref[...])

    bd = bd_ref[...]
    y = y_ref[0]
    yc = y - _dot_sel_rhs(y, bd) * (1.0 / RWKV_HEAD)
    var = _dot_sel_rhs(yc * yc, bd) * (1.0 / RWKV_HEAD)
    yn = yc * lax.rsqrt(var + LN_X_EPS)
    oa = (yn * vec_ref[ROW_LNW:ROW_LNW + 1, :] + bonus_ref[0]) * g_ref[0]

    merged = (sg[:, :D_MODEL] * _bdot(oa, wpa_ref[...])
              + sg[:, D_MODEL:] * jnp.dot(ob_ref[0], wpb_ref[...], preferred_element_type=F32))
    x1 = x + _bdot(merged, wom_ref[...])

    q = _bdot(_rms(x1, nx_ref[...]), wq_ref[...]) * (XHEAD_DIM ** -0.5)
    heads = []
    for hh in range(N_XHEADS):
        cols = slice(hh * XHEAD_DIM, (hh + 1) * XHEAD_DIM)
        sc = _bdot_nt(q[:, cols], k_ref[0, :, cols])
        e = jnp.exp(sc - jnp.max(sc, axis=-1, keepdims=True))
        pr = e / jnp.sum(e, axis=-1, keepdims=True)
        heads.append(_bdot(pr, v_ref[0, :, cols]))
    o = jnp.concatenate(heads, axis=-1)
    out_ref[0] = x1 + _bdot(o, wxo_ref[...])


def _mix(x, y, g, bonus, ob, kmem, vmem, vec, nm, bg, wgate, wpa, wpb, wom, nx, wq, wxo, bd, tile):
    b, s, d = x.shape
    n_mem = kmem.shape[1]
    const = lambda *shape: pl.BlockSpec(shape, lambda i, j: (0,) * len(shape))
    tok = lambda w: pl.BlockSpec((1, tile, w), lambda i, j: (i, j, 0))
    memspec = pl.BlockSpec((1, n_mem, d), lambda i, j: (i, 0, 0))
    return pl.pallas_call(
        _mix_kernel,
        grid=(b, s // tile),
        in_specs=[tok(d), tok(D_RWKV), tok(D_RWKV), tok(D_RWKV), tok(D_CONV), memspec, memspec,
                  const(VEC_ROWS, D_RWKV), const(1, d), const(1, 2 * d), const(d, 2 * d),
                  const(D_RWKV, d), const(D_CONV, d), const(d, d), const(1, d), const(d, d),
                  const(d, d), const(D_RWKV, D_RWKV)],
        out_specs=tok(d),
        out_shape=jax.ShapeDtypeStruct((b, s, d), F32),
        compiler_params=pltpu.CompilerParams(dimension_semantics=("arbitrary", "arbitrary"),
                                             vmem_limit_bytes=VMEM_LIMIT),
        name="mix",
    )(x, y, g, bonus, ob, kmem, vmem, vec, nm, bg, wgate, wpa, wpb, wom, nx, wq, wxo, bd)


def _mlp_kernel(x_ref, nm_ref, wup_ref, wdown_ref, nf_ref, out_ref):
    x = x_ref[...]
    h = _rms(x, nm_ref[...]).astype(BF16)
    acc = x
    for c in range(D_FF // D_MODEL):
        cols = slice(c * D_MODEL, (c + 1) * D_MODEL)
        up = jnp.maximum(jnp.dot(h, wup_ref[:, cols], preferred_element_type=F32), 0.0)
        acc = acc + _bdot(up * up, wdown_ref[cols, :])
    out_ref[...] = _rms(acc, nf_ref[...])


def _mlp(x, nm, wup, wdown, nf, tile):
    n, d = x.shape
    const = lambda *shape: pl.BlockSpec(shape, lambda i: (0,) * len(shape))
    tok = pl.BlockSpec((tile, d), lambda i: (i, 0))
    return pl.pallas_call(
        _mlp_kernel,
        grid=(n // tile,),
        in_specs=[tok, const(1, d), const(d, D_FF), const(D_FF, d), const(1, d)],
        out_specs=tok,
        out_shape=jax.ShapeDtypeStruct((n, d), F32),
        compiler_params=pltpu.CompilerParams(dimension_semantics=("arbitrary",),
                                             vmem_limit_bytes=VMEM_LIMIT),
        name="mlp",
    )(x, nm, wup, wdown, nf)


def _pad_cols(w, width):
    return jnp.pad(w, ((0, 0), (0, width - w.shape[1])))


def _pad_rows(w, height):
    return jnp.pad(w, ((0, height - w.shape[0]), (0, 0)))


def _layer(x, kmem, vmem, norm_mix, w_in, b_gate, mu_shift, w0, w_lora_w, a0, w_lora_a, w_lora_g,
           k_k, k_a, r_k, ln_x_w, ln_x_b, conv_w, w_proj_a, w_proj_b, w_out_mix, norm_xattn,
           w_q, w_xo, bd, tri):
    c_wd = 3 * D_RWKV
    c_ad = c_wd + DECAY_LORA
    c_gd = c_ad + AAA_LORA
    c_conv = c_gd + GATE_LORA
    c_gate = c_conv + 3 * D_CONV
    wrw = jnp.concatenate([w_in[:, :c_wd], _pad_cols(w_in[:, c_wd:c_ad], LANES),
                           _pad_cols(w_in[:, c_ad:c_gd], LANES),
                           _pad_cols(w_in[:, c_gd:c_conv], 2 * LANES)], axis=1).astype(BF16)
    mu = jnp.concatenate([mu_shift[None, :c_wd], _pad_cols(mu_shift[None, c_wd:c_ad], LANES),
                          _pad_cols(mu_shift[None, c_ad:c_gd], LANES),
                          _pad_cols(mu_shift[None, c_gd:], 2 * LANES)], axis=1)
    wconv = w_in[:, c_conv:c_gate].astype(BF16)
    wgate = w_in[:, c_gate:].astype(BF16)
    lora = []
    for w, rows in ((w_lora_w, LANES), (w_lora_a, LANES), (w_lora_g, 2 * LANES)):
        lora.extend(_split2(_pad_rows(w, rows)))
    vec = jnp.stack([w0, a0, k_k, k_a, r_k.reshape(-1), ln_x_b, ln_x_w,
                     conv_w[0, 0], conv_w[1, 0], conv_w[2, 0]])
    vec = _pad_rows(vec, VEC_ROWS)

    rt, at, bt, kt, bh, kh, v, wc, g, bonus, ob = _inproj(
        x, norm_mix[None], wrw, wconv, mu, vec, lora, bd, tri, tile=256)
    y = _wkv(rt, at, bt, kt, bh, kh, v, wc, tile=2 * CHUNK)
    return _mix(x, y, g, bonus, ob, kmem, vmem, vec, norm_mix[None], b_gate[None], wgate,
                w_proj_a.astype(BF16), w_proj_b.astype(BF16), w_out_mix.astype(BF16),
                norm_xattn[None], w_q.astype(BF16), w_xo.astype(BF16), bd, tile=256)


def kernel(x, mem, norm_mix, w_in, b_gate, mu_shift, w0, w_lora_w, a0, w_lora_a, w_lora_g, k_k, k_a, r_k, ln_x_w, ln_x_b, conv_w, w_proj_a, w_proj_b, w_out_mix, norm_xattn, norm_mem, w_q, w_kv, w_xo, norm_mlp, w_up, w_down, norm_final):
    assert w_in.shape[0] == 1, "the MLP kernel fuses the final norm: single-layer trunk only"
    bsz, s, d = x.shape
    head = jnp.arange(D_RWKV) // RWKV_HEAD
    bd = (head[:, None] == head[None, :]).astype(BF16)
    step = jnp.arange(CHUNK)
    tri = (step[None, :] <= step[:, None]).astype(BF16)
    kmem, vmem = _memkv(mem, norm_mem[0][None], w_kv[0].astype(BF16))
    x = _layer(x, kmem, vmem, norm_mix[0], w_in[0], b_gate[0], mu_shift[0], w0[0], w_lora_w[0],
               a0[0], w_lora_a[0], w_lora_g[0], k_k[0], k_a[0], r_k[0], ln_x_w[0], ln_x_b[0],
               conv_w[0], w_proj_a[0], w_proj_b[0], w_out_mix[0], norm_xattn[0], w_q[0],
               w_xo[0], bd, tri)
    return _mlp(x.reshape(bsz * s, d), norm_mlp[0][None], w_up[0].astype(BF16),
                w_down[0].astype(BF16), norm_final[None], tile=512).reshape(bsz, s, d)
```

```python
import functools

import jax
import jax.numpy as jnp
from jax import lax
from jax.experimental import pallas as pl
from jax.experimental.pallas import tpu as pltpu

F32 = jnp.float32
BF16 = jnp.bfloat16

D_MODEL = 1024
D_RWKV = 512
RWKV_HEAD = 64
DECAY_LORA = 64
AAA_LORA = 64
GATE_LORA = 160
LN_X_EPS = 64e-5
D_CONV = 512
N_XHEADS = 4
XHEAD_DIM = D_MODEL // N_XHEADS
D_FF = 4 * D_MODEL
RMS_EPS = 1e-6

LANES = 128
CHUNK = 64
PAIR = 2 * RWKV_HEAD
N_PAIRS = D_RWKV // PAIR
WD_OFF, AD_OFF, GD_OFF = 3 * D_RWKV, 3 * D_RWKV + LANES, 3 * D_RWKV + 2 * LANES
RW_COLS = GD_OFF + 2 * LANES
VMEM_LIMIT = 56 * 1024 * 1024

ROW_W0, ROW_A0, ROW_KK, ROW_KA, ROW_RK, ROW_LNB, ROW_LNW, ROW_CONV = 0, 1, 2, 3, 4, 5, 6, 7
VEC_ROWS = 16


def _bdot(a, b):
    return jnp.dot(a.astype(BF16), b.astype(BF16), preferred_element_type=F32)


def _bdot_nt(a, b):
    return lax.dot_general(a.astype(BF16), b.astype(BF16), (((1,), (1,)), ((), ())),
                           preferred_element_type=F32)


def _split2(a):
    hi = a.astype(BF16)
    lo = (a - hi.astype(F32)).astype(BF16)
    return hi, lo


def _split3(a):
    hi = a.astype(BF16)
    rem = a - hi.astype(F32)
    mid = rem.astype(BF16)
    lo = (rem - mid.astype(F32)).astype(BF16)
    return hi, mid, lo


def _dot_sel_rhs(a, sel):
    hi, mid, lo = _split3(a)
    d = functools.partial(jnp.dot, preferred_element_type=F32)
    return d(hi, sel) + d(mid, sel) + d(lo, sel)


def _dot_sel_lhs(sel, a):
    hi, mid, lo = _split3(a)
    d = functools.partial(jnp.dot, preferred_element_type=F32)
    return d(sel, hi) + d(sel, mid) + d(sel, lo)


def _dot_x3(a, b_hi, b_lo):
    a_hi, a_lo = _split2(a)
    d = functools.partial(jnp.dot, preferred_element_type=F32)
    return d(a_hi, b_hi) + d(a_hi, b_lo) + d(a_lo, b_hi)


def _rms(x, g):
    return x * lax.rsqrt(jnp.mean(x * x, axis=-1, keepdims=True) + RMS_EPS) * g


def _sigmoid(x):
    return 1.0 / (1.0 + jnp.exp(-x))


def _softplus(x):
    return jnp.maximum(x, 0.0) + jnp.log(1.0 + jnp.exp(-jnp.abs(x)))


def _shift_rows(x, carry_rows, n):
    rows = lax.broadcasted_iota(jnp.int32, (x.shape[0], 1), 0)
    out = pltpu.roll(x, n, 0)
    nc = carry_rows.shape[0]
    for j in range(n):
        out = jnp.where(rows == j, carry_rows[nc - n + j:nc - n + j + 1, :], out)
    return out


def _memkv_kernel(mem_ref, g_ref, w_ref, k_ref, v_ref):
    m = _rms(mem_ref[0], g_ref[...])
    kv = _bdot(m, w_ref[...])
    k_ref[0] = kv[:, :D_MODEL].astype(BF16)
    v_ref[0] = kv[:, D_MODEL:].astype(BF16)


def _memkv(mem, g, w_kv):
    b, n, d = mem.shape
    return pl.pallas_call(
        _memkv_kernel,
        grid=(b,),
        in_specs=[pl.BlockSpec((1, n, d), lambda i: (i, 0, 0)),
                  pl.BlockSpec((1, d), lambda i: (0, 0)),
                  pl.BlockSpec((d, 2 * d), lambda i: (0, 0))],
        out_specs=[pl.BlockSpec((1, n, d), lambda i: (i, 0, 0))] * 2,
        out_shape=[jax.ShapeDtypeStruct((b, n, d), BF16)] * 2,
        compiler_params=pltpu.CompilerParams(dimension_semantics=("arbitrary",),
                                             vmem_limit_bytes=VMEM_LIMIT),
        name="memkv",
    )(mem, g, w_kv)


def _inproj_kernel(x_ref, nm_ref, wrw_ref, wconv_ref, mu_ref, vec_ref,
                   wwh_ref, wwl_ref, wah_ref, wal_ref, wgh_ref, wgl_ref, bd_ref, tri_ref,
                   rt_ref, at_ref, bt_ref, kt_ref, bh_ref, kh_ref, v_ref, wc_ref,
                   g_ref, bonus_ref, ob_ref,
                   pcarry, ucarry):
    @pl.when(pl.program_id(1) == 0)
    def _():
        pcarry[...] = jnp.zeros_like(pcarry)
        ucarry[...] = jnp.zeros_like(ucarry)

    t = x_ref.shape[1]
    h = _rms(x_ref[0], nm_ref[...]).astype(BF16)

    pc = jnp.dot(h, wconv_ref[...], preferred_element_type=F32)
    u = pc[:, D_CONV:2 * D_CONV] * pc[:, 2 * D_CONV:]
    uc = ucarry[...]
    conv = (vec_ref[ROW_CONV:ROW_CONV + 1, :] * _shift_rows(u, uc, 2)
            + vec_ref[ROW_CONV + 1:ROW_CONV + 2, :] * _shift_rows(u, uc, 1)
            + vec_ref[ROW_CONV + 2:ROW_CONV + 3, :] * u)
    ob_ref[0] = (pc[:, :D_CONV] * conv).astype(BF16)
    ucarry[...] = u[t - 8:, :]

    p = jnp.dot(h, wrw_ref[...], preferred_element_type=F32)
    ps = _shift_rows(p, pcarry[...], 1)
    pcarry[...] = p[t - 8:, :]
    xm = p + (ps - p) * mu_ref[...]
    r = xm[:, :D_RWKV]
    k = xm[:, D_RWKV:2 * D_RWKV]
    v = xm[:, 2 * D_RWKV:3 * D_RWKV]
    wd = xm[:, WD_OFF:AD_OFF]
    ad = xm[:, AD_OFF:GD_OFF]
    gd = xm[:, GD_OFF:]

    def vec(row):
        return vec_ref[row:row + 1, :]

    log_w = -_softplus(-(vec(ROW_W0) + _dot_x3(jnp.tanh(wd), wwh_ref[...], wwl_ref[...]))) - 0.5
    ld = -jnp.exp(log_w)
    a = _sigmoid(vec(ROW_A0) + _dot_x3(ad, wah_ref[...], wal_ref[...]))
    g_ref[0] = _dot_x3(_sigmoid(gd), wgh_ref[...], wgl_ref[...])
    bd = bd_ref[...]
    kk = k * vec(ROW_KK)
    kk = kk / jnp.maximum(jnp.sqrt(_dot_sel_rhs(kk * kk, bd)), 1e-12)
    k2 = k * (1.0 + (a - 1.0) * vec(ROW_KA))
    bonus_ref[0] = _dot_sel_rhs(r * k2 * vec(ROW_RK), bd) * v + vec(ROW_LNB)
    v_ref[0] = v.astype(BF16)
    na = -kk
    nb = kk * a

    tri = tri_ref[...]
    for c in range(t // CHUNK):
        sl = slice(c * CHUNK, (c + 1) * CHUNK)
        ldc = ld[sl]
        cum = _dot_sel_lhs(tri, ldc)
        tot = cum[CHUNK - 1:CHUNK, :]
        e_inc = jnp.exp(cum)
        e_inv = jnp.exp(-cum)
        e_prev = jnp.exp(cum - ldc)
        e_end = jnp.exp(tot - cum)
        rt_ref[0, sl, :] = (r[sl] * e_inc).astype(BF16)
        at_ref[0, sl, :] = (na[sl] * e_prev).astype(BF16)
        bt_ref[0, sl, :] = (nb[sl] * e_inv).astype(BF16)
        kt_ref[0, sl, :] = (k2[sl] * e_inv).astype(BF16)
        bh_ref[0, sl, :] = (nb[sl] * e_end).astype(BF16)
        kh_ref[0, sl, :] = (k2[sl] * e_end).astype(BF16)
        wc_ref[0, c] = jnp.exp(tot)


def _inproj(x, nm, wrw, wconv, mu, vec, lora, bd, tri, tile):
    b, s, d = x.shape
    nt = s // tile
    const = lambda *shape: pl.BlockSpec(shape, lambda i, j: (0,) * len(shape))
    tok = lambda w: pl.BlockSpec((1, tile, w), lambda i, j: (i, j, 0))
    outs = ([jax.ShapeDtypeStruct((b, s, D_RWKV), BF16)] * 7
            + [jax.ShapeDtypeStruct((b, s // CHUNK, 1, D_RWKV), F32)]
            + [jax.ShapeDtypeStruct((b, s, D_RWKV), F32)] * 2
            + [jax.ShapeDtypeStruct((b, s, D_CONV), BF16)])
    out_specs = ([tok(D_RWKV)] * 7
                 + [pl.BlockSpec((1, tile // CHUNK, 1, D_RWKV), lambda i, j: (i, j, 0, 0))]
                 + [tok(D_RWKV)] * 2 + [tok(D_CONV)])
    return pl.pallas_call(
        _inproj_kernel,
        grid=(b, nt),
        in_specs=[tok(d), const(1, d), const(d, RW_COLS), const(d, 3 * D_CONV),
                  const(1, RW_COLS), const(VEC_ROWS, D_RWKV)]
                 + [const(*w.shape) for w in lora]
                 + [const(D_RWKV, D_RWKV), const(CHUNK, CHUNK)],
        out_specs=out_specs,
        out_shape=outs,
        scratch_shapes=[pltpu.VMEM((8, RW_COLS), F32), pltpu.VMEM((8, D_CONV), F32)],
        compiler_params=pltpu.CompilerParams(dimension_semantics=("arbitrary", "arbitrary"),
                                             vmem_limit_bytes=VMEM_LIMIT),
        name="inproj",
    )(x, nm, wrw, wconv, mu, vec, *lora, bd, tri)


def _pair_expand(x):
    even = lax.broadcasted_iota(jnp.int32, x.shape, 1) < RWKV_HEAD
    zero = jnp.zeros_like(x)
    return jnp.concatenate([jnp.where(even, x, zero), jnp.where(even, zero, x)], axis=0)


def _rows(*parts):
    return jnp.concatenate(parts, axis=0)


def _wkv_kernel(rt_ref, at_ref, bt_ref, kt_ref, bh_ref, kh_ref, v_ref, wc_ref, y_ref, state):
    @pl.when(pl.program_id(0) == 0)
    def _():
        state[...] = jnp.zeros_like(state)

    nb, t, _ = rt_ref.shape
    inst = [(b, p) for b in range(nb) for p in range(N_PAIRS)]
    row = lax.broadcasted_iota(jnp.int32, (CHUNK, PAIR), 0)
    col = lax.broadcasted_iota(jnp.int32, (CHUNK, PAIR), 1) % RWKV_HEAD
    strict = col < row
    incl = col <= row
    eye = (col == row).astype(F32)
    row2 = lax.broadcasted_iota(jnp.int32, (PAIR, PAIR), 0) < RWKV_HEAD
    col2 = lax.broadcasted_iota(jnp.int32, (PAIR, PAIR), 1) < RWKV_HEAD
    same_head = row2 == col2
    ex = lambda m: _pair_expand(m.astype(BF16))

    z = [state[b, p] for b, p in inst]
    for c in range(t // CHUNK):
        rows = slice(c * CHUNK, (c + 1) * CHUNK)
        load = lambda ref: [ref[b, rows, p * PAIR:(p + 1) * PAIR] for b, p in inst]
        r, a, bt, kt, bh, kh, v = (load(ref) for ref in
                                   (rt_ref, at_ref, bt_ref, kt_ref, bh_ref, kh_ref, v_ref))
        n = range(len(inst))
        g = [_bdot_nt(_rows(a[i], r[i]), _rows(_pair_expand(bt[i]), _pair_expand(kt[i]))) for i in n]
        lab = [jnp.where(strict, g[i][:CHUNK, :PAIR], 0.0) for i in n]
        lak = [jnp.where(strict, g[i][:CHUNK, PAIR:], 0.0) for i in n]
        mrb = [jnp.where(incl, g[i][CHUNK:, :PAIR], 0.0) for i in n]
        mrk = [jnp.where(incl, g[i][CHUNK:, PAIR:], 0.0) for i in n]
        npow = [_bdot(lab[i], ex(lab[i])) for i in n]
        tinv = [eye + lab[i] for i in n]
        for _ in range(4):
            prod = [_bdot(_rows(tinv[i], npow[i]), ex(npow[i])) for i in n]
            tinv = [tinv[i] + prod[i][:CHUNK] for i in n]
            npow = [prod[i][CHUNK:] for i in n]
        tinv = [tinv[i] + _bdot(tinv[i], ex(npow[i])) for i in n]
        lmv = [_bdot(_rows(lak[i], mrk[i]), _pair_expand(v[i])) for i in n]
        pq = [_bdot(tinv[i], jnp.concatenate([_pair_expand(a[i]), ex(lmv[i][:CHUNK])], axis=1))
              for i in n]
        kv = [jnp.where(same_head, _bdot(kh[i].astype(F32).T, v[i]), 0.0) for i in n]
        bht = [bh[i].astype(F32).T for i in n]
        wcol = [jnp.broadcast_to(wc_ref[b, c, :, p * PAIR:(p + 1) * PAIR], (PAIR, PAIR)).T
                for b, p in inst]
        pr = [_bdot(_rows(pq[i][:, :PAIR].astype(BF16), r[i]), z[i]) for i in n]
        u = [pr[i][:CHUNK] + pq[i][:, PAIR:] for i in n]
        z = [wcol[i] * z[i] + jnp.where(same_head, _bdot(bht[i], u[i]), 0.0) + kv[i] for i in n]
        y = [pr[i][CHUNK:] + _bdot(mrb[i], ex(u[i])) + lmv[i][CHUNK:] for i in n]
        for i, (b, p) in enumerate(inst):
            y_ref[b, rows, p * PAIR:(p + 1) * PAIR] = y[i]
    for i, (b, p) in enumerate(inst):
        state[b, p] = z[i]


def _wkv(rt, at, bt, kt, bh, kh, v, wc, tile):
    b, s, d = rt.shape
    tok = pl.BlockSpec((b, tile, d), lambda j: (0, j, 0))
    return pl.pallas_call(
        _wkv_kernel,
        grid=(s // tile,),
        in_specs=[tok] * 7 + [pl.BlockSpec((b, tile // CHUNK, 1, d), lambda j: (0, j, 0, 0))],
        out_specs=tok,
        out_shape=jax.ShapeDtypeStruct((b, s, d), F32),
        scratch_shapes=[pltpu.VMEM((b, N_PAIRS, PAIR, PAIR), F32)],
        compiler_params=pltpu.CompilerParams(dimension_semantics=("arbitrary",),
                                             vmem_limit_bytes=VMEM_LIMIT),
        name="wkv",
    )(rt, at, bt, kt, bh, kh, v, wc)


def _mix_kernel(x_ref, y_ref, g_ref, bonus_ref, ob_ref, k_ref, v_ref, vec_ref, nm_ref, bg_ref,
                wgate_ref, wpa_ref, wpb_ref, wom_ref, nx_ref, wq_ref, wxo_ref, bd_ref, out_ref):
    x = x_ref[0]
    h = _rms(x, nm_ref[...]).astype(BF16)
    sg = _sigmoid(jnp.dot(h, wgate_ref[...], preferred_element_type=F32) + bg_ref[...])

    bd = bd_ref[...]
    y = y_ref[0]
    yc = y - _dot_sel_rhs(y, bd) * (1.0 / RWKV_HEAD)
    var = _dot_sel_rhs(yc * yc, bd) * (1.0 / RWKV_HEAD)
    yn = yc * lax.rsqrt(var + LN_X_EPS)
    oa = (yn * vec_ref[ROW_LNW:ROW_LNW + 1, :] + bonus_ref[0]) * g_ref[0]

    merged = (sg[:, :D_MODEL] * _bdot(oa, wpa_ref[...])
              + sg[:, D_MODEL:] * jnp.dot(ob_ref[0], wpb_ref[...], preferred_element_type=F32))
    x1 = x + _bdot(merged, wom_ref[...])

    q = _bdot(_rms(x1, nx_ref[...]), wq_ref[...]) * (XHEAD_DIM ** -0.5)
    heads = []
    for hh in range(N_XHEADS):
        cols = slice(hh * XHEAD_DIM, (hh + 1) * XHEAD_DIM)
        sc = _bdot_nt(q[:, cols], k_ref[0, :, cols])
        e = jnp.exp(sc - jnp.max(sc, axis=-1, keepdims=True))
        pr = e / jnp.sum(e, axis=-1, keepdims=True)
        heads.append(_bdot(pr, v_ref[0, :, cols]))
    o = jnp.concatenate(heads, axis=-1)
    out_ref[0] = x1 + _bdot(o, wxo_ref[...])


def _mix(x, y, g, bonus, ob, kmem, vmem, vec, nm, bg, wgate, wpa, wpb, wom, nx, wq, wxo, bd, tile):
    b, s, d = x.shape
    n_mem = kmem.shape[1]
    const = lambda *shape: pl.BlockSpec(shape, lambda i, j: (0,) * len(shape))
    tok = lambda w: pl.BlockSpec((1, tile, w), lambda i, j: (i, j, 0))
    memspec = pl.BlockSpec((1, n_mem, d), lambda i, j: (i, 0, 0))
    return pl.pallas_call(
        _mix_kernel,
        grid=(b, s // tile),
        in_specs=[tok(d), tok(D_RWKV), tok(D_RWKV), tok(D_RWKV), tok(D_CONV), memspec, memspec,
                  const(VEC_ROWS, D_RWKV), const(1, d), const(1, 2 * d), const(d, 2 * d),
                  const(D_RWKV, d), const(D_CONV, d), const(d, d), const(1, d), const(d, d),
                  const(d, d), const(D_RWKV, D_RWKV)],
        out_specs=tok(d),
        out_shape=jax.ShapeDtypeStruct((b, s, d), F32),
        compiler_params=pltpu.CompilerParams(dimension_semantics=("arbitrary", "arbitrary"),
                                             vmem_limit_bytes=VMEM_LIMIT),
        name="mix",
    )(x, y, g, bonus, ob, kmem, vmem, vec, nm, bg, wgate, wpa, wpb, wom, nx, wq, wxo, bd)


def _mlp_kernel(x_ref, nm_ref, wup_ref, wdown_ref, nf_ref, out_ref):
    x = x_ref[...]
    h = _rms(x, nm_ref[...]).astype(BF16)
    acc = x
    for c in range(D_FF // D_MODEL):
        cols = slice(c * D_MODEL, (c + 1) * D_MODEL)
        up = jnp.maximum(jnp.dot(h, wup_ref[:, cols], preferred_element_type=F32), 0.0)
        acc = acc + _bdot(up * up, wdown_ref[cols, :])
    out_ref[...] = _rms(acc, nf_ref[...])


def _mlp(x, nm, wup, wdown, nf, tile):
    n, d = x.shape
    const = lambda *shape: pl.BlockSpec(shape, lambda i: (0,) * len(shape))
    tok = pl.BlockSpec((tile, d), lambda i: (i, 0))
    return pl.pallas_call(
        _mlp_kernel,
        grid=(n // tile,),
        in_specs=[tok, const(1, d), const(d, D_FF), const(D_FF, d), const(1, d)],
        out_specs=tok,
        out_shape=jax.ShapeDtypeStruct((n, d), F32),
        compiler_params=pltpu.CompilerParams(dimension_semantics=("arbitrary",),
                                             vmem_limit_bytes=VMEM_LIMIT),
        name="mlp",
    )(x, nm, wup, wdown, nf)


def _pad_cols(w, width):
    return jnp.pad(w, ((0, 0), (0, width - w.shape[1])))


def _pad_rows(w, height):
    return jnp.pad(w, ((0, height - w.shape[0]), (0, 0)))


def _layer(x, kmem, vmem, norm_mix, w_in, b_gate, mu_shift, w0, w_lora_w, a0, w_lora_a, w_lora_g,
           k_k, k_a, r_k, ln_x_w, ln_x_b, conv_w, w_proj_a, w_proj_b, w_out_mix, norm_xattn,
           w_q, w_xo, bd, tri):
    c_wd = 3 * D_RWKV
    c_ad = c_wd + DECAY_LORA
    c_gd = c_ad + AAA_LORA
    c_conv = c_gd + GATE_LORA
    c_gate = c_conv + 3 * D_CONV
    wrw = jnp.concatenate([w_in[:, :c_wd], _pad_cols(w_in[:, c_wd:c_ad], LANES),
                           _pad_cols(w_in[:, c_ad:c_gd], LANES),
                           _pad_cols(w_in[:, c_gd:c_conv], 2 * LANES)], axis=1).astype(BF16)
    mu = jnp.concatenate([mu_shift[None, :c_wd], _pad_cols(mu_shift[None, c_wd:c_ad], LANES),
                          _pad_cols(mu_shift[None, c_ad:c_gd], LANES),
                          _pad_cols(mu_shift[None, c_gd:], 2 * LANES)], axis=1)
    wconv = w_in[:, c_conv:c_gate].astype(BF16)
    wgate = w_in[:, c_gate:].astype(BF16)
    lora = []
    for w, rows in ((w_lora_w, LANES), (w_lora_a, LANES), (w_lora_g, 2 * LANES)):
        lora.extend(_split2(_pad_rows(w, rows)))
    vec = jnp.stack([w0, a0, k_k, k_a, r_k.reshape(-1), ln_x_b, ln_x_w,
                     conv_w[0, 0], conv_w[1, 0], conv_w[2, 0]])
    vec = _pad_rows(vec, VEC_ROWS)

    rt, at, bt, kt, bh, kh, v, wc, g, bonus, ob = _inproj(
        x, norm_mix[None], wrw, wconv, mu, vec, lora, bd, tri, tile=256)
    y = _wkv(rt, at, bt, kt, bh, kh, v, wc, tile=2 * CHUNK)
    return _mix(x, y, g, bonus, ob, kmem, vmem, vec, norm_mix[None], b_gate[None], wgate,
                w_proj_a.astype(BF16), w_proj_b.astype(BF16), w_out_mix.astype(BF16),
                norm_xattn[None], w_q.astype(BF16), w_xo.astype(BF16), bd, tile=256)


def kernel(x, mem, norm_mix, w_in, b_gate, mu_shift, w0, w_lora_w, a0, w_lora_a, w_lora_g, k_k, k_a, r_k, ln_x_w, ln_x_b, conv_w, w_proj_a, w_proj_b, w_out_mix, norm_xattn, norm_mem, w_q, w_kv, w_xo, norm_mlp, w_up, w_down, norm_final):
    assert w_in.shape[0] == 1, "the MLP kernel fuses the final norm: single-layer trunk only"
    bsz, s, d = x.shape
    head = jnp.arange(D_RWKV) // RWKV_HEAD
    bd = (head[:, None] == head[None, :]).astype(BF16)
    step = jnp.arange(CHUNK)
    tri = (step[None, :] <= step[:, None]).astype(BF16)
    kmem, vmem = _memkv(mem, norm_mem[0][None], w_kv[0].astype(BF16))
    x = _layer(x, kmem, vmem, norm_mix[0], w_in[0], b_gate[0], mu_shift[0], w0[0], w_lora_w[0],
               a0[0], w_lora_a[0], w_lora_g[0], k_k[0], k_a[0], r_k[0], ln_x_w[0], ln_x_b[0],
               conv_w[0], w_proj_a[0], w_proj_b[0], w_out_mix[0], norm_xattn[0], w_q[0],
               w_xo[0], bd, tri)
    return _mlp(x.reshape(bsz * s, d), norm_mlp[0][None], w_up[0].astype(BF16),
                w_down[0].astype(BF16), norm_final[None], tile=512).reshape(bsz, s, d)
```

```python
import functools

import jax
import jax.numpy as jnp
from jax import lax
from jax.experimental import pallas as pl
from jax.experimental.pallas import tpu as pltpu

F32 = jnp.float32
BF16 = jnp.bfloat16

D_MODEL = 1024
D_RWKV = 512
RWKV_HEAD = 64
DECAY_LORA = 64
AAA_LORA = 64
GATE_LORA = 160
LN_X_EPS = 64e-5
D_CONV = 512
N_XHEADS = 4
XHEAD_DIM = D_MODEL // N_XHEADS
D_FF = 4 * D_MODEL
RMS_EPS = 1e-6

LANES = 128
CHUNK = 64
PAIR = 2 * RWKV_HEAD
N_PAIRS = D_RWKV // PAIR
WD_OFF, AD_OFF, GD_OFF = 3 * D_RWKV, 3 * D_RWKV + LANES, 3 * D_RWKV + 2 * LANES
RW_COLS = GD_OFF + 2 * LANES
VMEM_LIMIT = 56 * 1024 * 1024

ROW_W0, ROW_A0, ROW_KK, ROW_KA, ROW_RK, ROW_LNB, ROW_LNW, ROW_CONV = 0, 1, 2, 3, 4, 5, 6, 7
VEC_ROWS = 16

TILE_INPROJ = 512
TILE_WKV = 2 * CHUNK
TILE_MIX = 512
TILE_MLP = 1024


def _bdot(a, b):
    return jnp.dot(a.astype(BF16), b.astype(BF16), preferred_element_type=F32)


def _bdot_nt(a, b):
    return lax.dot_general(a.astype(BF16), b.astype(BF16), (((1,), (1,)), ((), ())),
                           preferred_element_type=F32)


def _split2(a):
    hi = a.astype(BF16)
    lo = (a - hi.astype(F32)).astype(BF16)
    return hi, lo


def _dot_sel_rhs(a, sel):
    hi, lo = _split2(a)
    d = functools.partial(jnp.dot, preferred_element_type=F32)
    return d(hi, sel) + d(lo, sel)


def _dot_sel_lhs(sel, a):
    hi, lo = _split2(a)
    d = functools.partial(jnp.dot, preferred_element_type=F32)
    return d(sel, hi) + d(sel, lo)


def _rms(x, g):
    return x * lax.rsqrt(jnp.mean(x * x, axis=-1, keepdims=True) + RMS_EPS) * g


def _sigmoid(x):
    return 1.0 / (1.0 + jnp.exp(-x))


def _softplus(x):
    return jnp.maximum(x, 0.0) + jnp.log(1.0 + jnp.exp(-jnp.abs(x)))


def _shift_rows(x, carry_rows, n):
    nc = carry_rows.shape[0]
    out = pltpu.roll(x, n, 0)
    rows = lax.broadcasted_iota(jnp.int32, (nc, 1), 0)
    head = out[:nc]
    for j in range(n):
        head = jnp.where(rows == j, carry_rows[nc - n + j:nc - n + j + 1, :], head)
    return jnp.concatenate([head, out[nc:]], axis=0)


def _const_spec(*shape):
    return pl.BlockSpec(shape, lambda *_: (0,) * len(shape), pipeline_mode=pl.Buffered(1))


def _memkv_kernel(mem_ref, g_ref, w_ref, k_ref, v_ref):
    m = _rms(mem_ref[0], g_ref[...])
    kv = _bdot(m, w_ref[...])
    k_ref[0] = kv[:, :D_MODEL].astype(BF16)
    v_ref[0] = kv[:, D_MODEL:].astype(BF16)


def _memkv(mem, g, w_kv):
    b, n, d = mem.shape
    return pl.pallas_call(
        _memkv_kernel,
        grid=(b,),
        in_specs=[pl.BlockSpec((1, n, d), lambda i: (i, 0, 0)),
                  pl.BlockSpec((1, d), lambda i: (0, 0)),
                  pl.BlockSpec((d, 2 * d), lambda i: (0, 0))],
        out_specs=[pl.BlockSpec((1, n, d), lambda i: (i, 0, 0))] * 2,
        out_shape=[jax.ShapeDtypeStruct((b, n, d), BF16)] * 2,
        compiler_params=pltpu.CompilerParams(dimension_semantics=("arbitrary",),
                                             vmem_limit_bytes=VMEM_LIMIT),
        name="memkv",
    )(mem, g, w_kv)


def _inproj_kernel(x_ref, nm_ref, wrw_ref, wconv_ref, mu_ref, vec_ref,
                   ww_ref, wa_ref, wg_ref, bd_ref, tri_ref,
                   rt_ref, at_ref, bt_ref, kt_ref, bh_ref, kh_ref, v_ref, wc_ref,
                   g_ref, bonus_ref, ob_ref,
                   pcarry, ucarry):
    @pl.when(pl.program_id(1) == 0)
    def _():
        pcarry[...] = jnp.zeros_like(pcarry)
        ucarry[...] = jnp.zeros_like(ucarry)

    t = x_ref.shape[1]
    h = _rms(x_ref[0], nm_ref[...]).astype(BF16)

    pc = jnp.dot(h, wconv_ref[...], preferred_element_type=F32)
    u = pc[:, D_CONV:2 * D_CONV] * pc[:, 2 * D_CONV:]
    uc = ucarry[...]
    conv = (vec_ref[ROW_CONV:ROW_CONV + 1, :] * _shift_rows(u, uc, 2)
            + vec_ref[ROW_CONV + 1:ROW_CONV + 2, :] * _shift_rows(u, uc, 1)
            + vec_ref[ROW_CONV + 2:ROW_CONV + 3, :] * u)
    ob_ref[0] = (pc[:, :D_CONV] * conv).astype(BF16)
    ucarry[...] = u[t - 8:, :]

    p = jnp.dot(h, wrw_ref[...], preferred_element_type=F32)
    ps = _shift_rows(p, pcarry[...], 1)
    pcarry[...] = p[t - 8:, :]
    xm = p + (ps - p) * mu_ref[...]
    r = xm[:, :D_RWKV]
    k = xm[:, D_RWKV:2 * D_RWKV]
    v = xm[:, 2 * D_RWKV:3 * D_RWKV]
    wd = xm[:, WD_OFF:AD_OFF]
    ad = xm[:, AD_OFF:GD_OFF]
    gd = xm[:, GD_OFF:]

    def vec(row):
        return vec_ref[row:row + 1, :]

    log_w = -_softplus(-(vec(ROW_W0) + _bdot(jnp.tanh(wd), ww_ref[...]))) - 0.5
    ld = -jnp.exp(log_w)
    a = _sigmoid(vec(ROW_A0) + _bdot(ad, wa_ref[...]))
    g_ref[0] = _bdot(_sigmoid(gd), wg_ref[...])
    bd = bd_ref[...]
    kk = k * vec(ROW_KK)
    kk = kk * lax.rsqrt(jnp.maximum(_dot_sel_rhs(kk * kk, bd), 1e-24))
    k2 = k * (1.0 + (a - 1.0) * vec(ROW_KA))
    bonus_ref[0] = _bdot(r * k2 * vec(ROW_RK), bd) * v + vec(ROW_LNB)
    v_ref[0] = v.astype(BF16)
    na = -kk
    nb = kk * a

    tri = tri_ref[...]
    for c in range(t // CHUNK):
        sl = slice(c * CHUNK, (c + 1) * CHUNK)
        ldc = ld[sl]
        cum = _dot_sel_lhs(tri, ldc)
        tot = cum[CHUNK - 1:CHUNK, :]
        e_inc = jnp.exp(cum)
        e_inv = jnp.exp(-cum)
        e_prev = jnp.exp(cum - ldc)
        e_end = jnp.exp(tot - cum)
        rt_ref[0, sl, :] = (r[sl] * e_inc).astype(BF16)
        at_ref[0, sl, :] = (na[sl] * e_prev).astype(BF16)
        bt_ref[0, sl, :] = (nb[sl] * e_inv).astype(BF16)
        kt_ref[0, sl, :] = (k2[sl] * e_inv).astype(BF16)
        bh_ref[0, sl, :] = (nb[sl] * e_end).astype(BF16)
        kh_ref[0, sl, :] = (k2[sl] * e_end).astype(BF16)
        wc_ref[0, c] = jnp.exp(tot)


def _inproj(x, nm, wrw, wconv, mu, vec, lora, bd, tri, tile):
    b, s, d = x.shape
    nt = s // tile
    const = _const_spec
    tok = lambda w: pl.BlockSpec((1, tile, w), lambda i, j: (i, j, 0))
    outs = ([jax.ShapeDtypeStruct((b, s, D_RWKV), BF16)] * 7
            + [jax.ShapeDtypeStruct((b, s // CHUNK, 1, D_RWKV), F32)]
            + [jax.ShapeDtypeStruct((b, s, D_RWKV), F32)] * 2
            + [jax.ShapeDtypeStruct((b, s, D_CONV), BF16)])
    out_specs = ([tok(D_RWKV)] * 7
                 + [pl.BlockSpec((1, tile // CHUNK, 1, D_RWKV), lambda i, j: (i, j, 0, 0))]
                 + [tok(D_RWKV)] * 2 + [tok(D_CONV)])
    return pl.pallas_call(
        _inproj_kernel,
        grid=(b, nt),
        in_specs=[tok(d), const(1, d), const(d, RW_COLS), const(d, 3 * D_CONV),
                  const(1, RW_COLS), const(VEC_ROWS, D_RWKV)]
                 + [const(*w.shape) for w in lora]
                 + [const(D_RWKV, D_RWKV), const(CHUNK, CHUNK)],
        out_specs=out_specs,
        out_shape=outs,
        scratch_shapes=[pltpu.VMEM((8, RW_COLS), F32), pltpu.VMEM((8, D_CONV), F32)],
        compiler_params=pltpu.CompilerParams(dimension_semantics=("arbitrary", "arbitrary"),
                                             vmem_limit_bytes=VMEM_LIMIT),
        name="inproj",
    )(x, nm, wrw, wconv, mu, vec, *lora, bd, tri)


def _pair_expand(x):
    even = lax.broadcasted_iota(jnp.int32, x.shape, 1) < RWKV_HEAD
    zero = jnp.zeros_like(x)
    return jnp.concatenate([jnp.where(even, x, zero), jnp.where(even, zero, x)], axis=0)


def _rows(*parts):
    return jnp.concatenate(parts, axis=0)


def _wkv_kernel(rt_ref, at_ref, bt_ref, kt_ref, bh_ref, kh_ref, v_ref, wc_ref, y_ref, state):
    @pl.when(pl.program_id(0) == 0)
    def _():
        state[...] = jnp.zeros_like(state)

    nb, t, _ = rt_ref.shape
    inst = [(b, p) for b in range(nb) for p in range(N_PAIRS)]
    row = lax.broadcasted_iota(jnp.int32, (CHUNK, PAIR), 0)
    col = lax.broadcasted_iota(jnp.int32, (CHUNK, PAIR), 1) % RWKV_HEAD
    strict = col < row
    incl = col <= row
    eye = (col == row).astype(F32)
    row2 = lax.broadcasted_iota(jnp.int32, (PAIR, PAIR), 0) < RWKV_HEAD
    col2 = lax.broadcasted_iota(jnp.int32, (PAIR, PAIR), 1) < RWKV_HEAD
    same_head = row2 == col2
    ex = lambda m: _pair_expand(m.astype(BF16))

    z = [state[b, p] for b, p in inst]
    for c in range(t // CHUNK):
        rows = slice(c * CHUNK, (c + 1) * CHUNK)
        load = lambda ref: [ref[b, rows, p * PAIR:(p + 1) * PAIR] for b, p in inst]
        r, a, bt, kt, bh, kh, v = (load(ref) for ref in
                                   (rt_ref, at_ref, bt_ref, kt_ref, bh_ref, kh_ref, v_ref))
        n = range(len(inst))
        g = [_bdot_nt(_rows(a[i], r[i]), _rows(_pair_expand(bt[i]), _pair_expand(kt[i]))) for i in n]
        lab = [jnp.where(strict, g[i][:CHUNK, :PAIR], 0.0) for i in n]
        lak = [jnp.where(strict, g[i][:CHUNK, PAIR:], 0.0) for i in n]
        mrb = [jnp.where(incl, g[i][CHUNK:, :PAIR], 0.0) for i in n]
        mrk = [jnp.where(incl, g[i][CHUNK:, PAIR:], 0.0) for i in n]
        npow = [_bdot(lab[i], ex(lab[i])) for i in n]
        tinv = [eye + lab[i] for i in n]
        for _ in range(4):
            prod = [_bdot(_rows(tinv[i], npow[i]), ex(npow[i])) for i in n]
            tinv = [tinv[i] + prod[i][:CHUNK] for i in n]
            npow = [prod[i][CHUNK:] for i in n]
        tinv = [tinv[i] + _bdot(tinv[i], ex(npow[i])) for i in n]
        lmv = [_bdot(_rows(lak[i], mrk[i]), _pair_expand(v[i])) for i in n]
        pq = [_bdot(tinv[i], jnp.concatenate([_pair_expand(a[i]), ex(lmv[i][:CHUNK])], axis=1))
              for i in n]
        kv = [jnp.where(same_head, _bdot(kh[i].astype(F32).T, v[i]), 0.0) for i in n]
        bht = [bh[i].astype(F32).T for i in n]
        wcol = [jnp.broadcast_to(wc_ref[b, c, :, p * PAIR:(p + 1) * PAIR], (PAIR, PAIR)).T
                for b, p in inst]
        pr = [_bdot(_rows(pq[i][:, :PAIR].astype(BF16), r[i]), z[i]) for i in n]
        u = [pr[i][:CHUNK] + pq[i][:, PAIR:] for i in n]
        z = [wcol[i] * z[i] + jnp.where(same_head, _bdot(bht[i], u[i]), 0.0) + kv[i] for i in n]
        y = [pr[i][CHUNK:] + _bdot(mrb[i], ex(u[i])) + lmv[i][CHUNK:] for i in n]
        for i, (b, p) in enumerate(inst):
            y_ref[b, rows, p * PAIR:(p + 1) * PAIR] = y[i]
    for i, (b, p) in enumerate(inst):
        state[b, p] = z[i]


def _wkv(rt, at, bt, kt, bh, kh, v, wc, tile):
    b, s, d = rt.shape
    tok = pl.BlockSpec((b, tile, d), lambda j: (0, j, 0))
    return pl.pallas_call(
        _wkv_kernel,
        grid=(s // tile,),
        in_specs=[tok] * 7 + [pl.BlockSpec((b, tile // CHUNK, 1, d), lambda j: (0, j, 0, 0))],
        out_specs=tok,
        out_shape=jax.ShapeDtypeStruct((b, s, d), F32),
        scratch_shapes=[pltpu.VMEM((b, N_PAIRS, PAIR, PAIR), F32)],
        compiler_params=pltpu.CompilerParams(dimension_semantics=("arbitrary",),
                                             vmem_limit_bytes=VMEM_LIMIT),
        name="wkv",
    )(rt, at, bt, kt, bh, kh, v, wc)


def _mix_kernel(x_ref, y_ref, g_ref, bonus_ref, ob_ref, k_ref, v_ref, vec_ref, nm_ref, bg_ref,
                wgate_ref, wpa_ref, wpb_ref, wom_ref, nx_ref, wq_ref, wxo_ref, bd_ref, out_ref):
    x = x_ref[0]
    h = _rms(x, nm_ref[...]).astype(BF16)
    sg = _sigmoid(jnp.dot(h, wgate_ref[...], preferred_element_type=F32) + bg_ref[...])

    bd = bd_ref[...]
    y = y_ref[0]
    yc = y - _dot_sel_rhs(y, bd) * (1.0 / RWKV_HEAD)
    var = _bdot(yc * yc, bd) * (1.0 / RWKV_HEAD)
    yn = yc * lax.rsqrt(var + LN_X_EPS)
    oa = (yn * vec_ref[ROW_LNW:ROW_LNW + 1, :] + bonus_ref[0]) * g_ref[0]

    merged = (sg[:, :D_MODEL] * _bdot(oa, wpa_ref[...])
              + sg[:, D_MODEL:] * jnp.dot(ob_ref[0], wpb_ref[...], preferred_element_type=F32))
    x1 = x + _bdot(merged, wom_ref[...])

    q = _bdot(_rms(x1, nx_ref[...]), wq_ref[...]) * (XHEAD_DIM ** -0.5)
    heads = []
    for hh in range(N_XHEADS):
        cols = slice(hh * XHEAD_DIM, (hh + 1) * XHEAD_DIM)
        sc = _bdot_nt(q[:, cols], k_ref[0, :, cols])
        e = jnp.exp(sc - jnp.max(sc, axis=-1, keepdims=True))
        pr = e / jnp.sum(e, axis=-1, keepdims=True)
        heads.append(_bdot(pr, v_ref[0, :, cols]))
    o = jnp.concatenate(heads, axis=-1)
    out_ref[0] = x1 + _bdot(o, wxo_ref[...])


def _mix(x, y, g, bonus, ob, kmem, vmem, vec, nm, bg, wgate, wpa, wpb, wom, nx, wq, wxo, bd, tile):
    b, s, d = x.shape
    n_mem = kmem.shape[1]
    const = _const_spec
    tok = lambda w: pl.BlockSpec((1, tile, w), lambda i, j: (i, j, 0))
    memspec = pl.BlockSpec((1, n_mem, d), lambda i, j: (i, 0, 0))
    return pl.pallas_call(
        _mix_kernel,
        grid=(b, s // tile),
        in_specs=[tok(d), tok(D_RWKV), tok(D_RWKV), tok(D_RWKV), tok(D_CONV), memspec, memspec,
                  const(VEC_ROWS, D_RWKV), const(1, d), const(1, 2 * d), const(d, 2 * d),
                  const(D_RWKV, d), const(D_CONV, d), const(d, d), const(1, d), const(d, d),
                  const(d, d), const(D_RWKV, D_RWKV)],
        out_specs=tok(d),
        out_shape=jax.ShapeDtypeStruct((b, s, d), F32),
        compiler_params=pltpu.CompilerParams(dimension_semantics=("arbitrary", "arbitrary"),
                                             vmem_limit_bytes=VMEM_LIMIT),
        name="mix",
    )(x, y, g, bonus, ob, kmem, vmem, vec, nm, bg, wgate, wpa, wpb, wom, nx, wq, wxo, bd)


def _mlp_kernel(x_ref, nm_ref, wup_ref, wdown_ref, nf_ref, out_ref):
    x = x_ref[...]
    h = _rms(x, nm_ref[...]).astype(BF16)
    acc = x
    for c in range(D_FF // D_MODEL):
        cols = slice(c * D_MODEL, (c + 1) * D_MODEL)
        up = jnp.maximum(jnp.dot(h, wup_ref[:, cols], preferred_element_type=F32), 0.0)
        acc = acc + _bdot(up * up, wdown_ref[cols, :])
    out_ref[...] = _rms(acc, nf_ref[...])


def _mlp(x, nm, wup, wdown, nf, tile):
    n, d = x.shape
    const = _const_spec
    tok = pl.BlockSpec((tile, d), lambda i: (i, 0))
    return pl.pallas_call(
        _mlp_kernel,
        grid=(n // tile,),
        in_specs=[tok, const(1, d), const(d, D_FF), const(D_FF, d), const(1, d)],
        out_specs=tok,
        out_shape=jax.ShapeDtypeStruct((n, d), F32),
        compiler_params=pltpu.CompilerParams(dimension_semantics=("arbitrary",),
                                             vmem_limit_bytes=VMEM_LIMIT),
        name="mlp",
    )(x, nm, wup, wdown, nf)


def _pad_cols(w, width):
    return jnp.pad(w, ((0, 0), (0, width - w.shape[1])))


def _pad_rows(w, height):
    return jnp.pad(w, ((0, height - w.shape[0]), (0, 0)))


def _layer(x, kmem, vmem, norm_mix, w_in, b_gate, mu_shift, w0, w_lora_w, a0, w_lora_a, w_lora_g,
           k_k, k_a, r_k, ln_x_w, ln_x_b, conv_w, w_proj_a, w_proj_b, w_out_mix, norm_xattn,
           w_q, w_xo, bd, tri):
    c_wd = 3 * D_RWKV
    c_ad = c_wd + DECAY_LORA
    c_gd = c_ad + AAA_LORA
    c_conv = c_gd + GATE_LORA
    c_gate = c_conv + 3 * D_CONV
    wrw = jnp.concatenate([w_in[:, :c_wd], _pad_cols(w_in[:, c_wd:c_ad], LANES),
                           _pad_cols(w_in[:, c_ad:c_gd], LANES),
                           _pad_cols(w_in[:, c_gd:c_conv], 2 * LANES)], axis=1).astype(BF16)
    mu = jnp.concatenate([mu_shift[None, :c_wd], _pad_cols(mu_shift[None, c_wd:c_ad], LANES),
                          _pad_cols(mu_shift[None, c_ad:c_gd], LANES),
                          _pad_cols(mu_shift[None, c_gd:], 2 * LANES)], axis=1)
    wconv = w_in[:, c_conv:c_gate].astype(BF16)
    wgate = w_in[:, c_gate:].astype(BF16)
    lora = [_pad_rows(w, rows).astype(BF16)
            for w, rows in ((w_lora_w, LANES), (w_lora_a, LANES), (w_lora_g, 2 * LANES))]
    vec = jnp.stack([w0, a0, k_k, k_a, r_k.reshape(-1), ln_x_b, ln_x_w,
                     conv_w[0, 0], conv_w[1, 0], conv_w[2, 0]])
    vec = _pad_rows(vec, VEC_ROWS)

    rt, at, bt, kt, bh, kh, v, wc, g, bonus, ob = _inproj(
        x, norm_mix[None], wrw, wconv, mu, vec, lora, bd, tri, tile=TILE_INPROJ)
    y = _wkv(rt, at, bt, kt, bh, kh, v, wc, tile=TILE_WKV)
    return _mix(x, y, g, bonus, ob, kmem, vmem, vec, norm_mix[None], b_gate[None], wgate,
                w_proj_a.astype(BF16), w_proj_b.astype(BF16), w_out_mix.astype(BF16),
                norm_xattn[None], w_q.astype(BF16), w_xo.astype(BF16), bd, tile=TILE_MIX)


def kernel(x, mem, norm_mix, w_in, b_gate, mu_shift, w0, w_lora_w, a0, w_lora_a, w_lora_g, k_k, k_a, r_k, ln_x_w, ln_x_b, conv_w, w_proj_a, w_proj_b, w_out_mix, norm_xattn, norm_mem, w_q, w_kv, w_xo, norm_mlp, w_up, w_down, norm_final):
    assert w_in.shape[0] == 1, "the MLP kernel fuses the final norm: single-layer trunk only"
    bsz, s, d = x.shape
    head = jnp.arange(D_RWKV) // RWKV_HEAD
    bd = (head[:, None] == head[None, :]).astype(BF16)
    step = jnp.arange(CHUNK)
    tri = (step[None, :] <= step[:, None]).astype(BF16)
    kmem, vmem = _memkv(mem, norm_mem[0][None], w_kv[0].astype(BF16))
    x = _layer(x, kmem, vmem, norm_mix[0], w_in[0], b_gate[0], mu_shift[0], w0[0], w_lora_w[0],
               a0[0], w_lora_a[0], w_lora_g[0], k_k[0], k_a[0], r_k[0], ln_x_w[0], ln_x_b[0],
               conv_w[0], w_proj_a[0], w_proj_b[0], w_out_mix[0], norm_xattn[0], w_q[0],
               w_xo[0], bd, tri)
    return _mlp(x.reshape(bsz * s, d), norm_mlp[0][None], w_up[0].astype(BF16),
                w_down[0].astype(BF16), norm_final[None], tile=TILE_MLP).reshape(bsz, s, d)
```

```python
import functools
import itertools

import jax
import jax.numpy as jnp
from jax import lax
from jax.experimental import pallas as pl
from jax.experimental.pallas import tpu as pltpu

F32 = jnp.float32
BF16 = jnp.bfloat16

D_MODEL = 1024
D_RWKV = 512
RWKV_HEAD = 64
DECAY_LORA = 64
AAA_LORA = 64
GATE_LORA = 160
LN_X_EPS = 64e-5
D_CONV = 512
N_XHEADS = 4
XHEAD_DIM = D_MODEL // N_XHEADS
D_FF = 4 * D_MODEL
RMS_EPS = 1e-6

LANES = 128
CHUNK = 64
PAIR = 2 * RWKV_HEAD
N_PAIRS = D_RWKV // PAIR
WD_OFF, AD_OFF, GD_OFF = 3 * D_RWKV, 3 * D_RWKV + LANES, 3 * D_RWKV + 2 * LANES
RW_COLS = GD_OFF + 2 * LANES
VMEM_LIMIT = 56 * 1024 * 1024

ROW_W0, ROW_A0, ROW_KK, ROW_KA, ROW_RK, ROW_LNB, ROW_LNW, ROW_CONV = 0, 1, 2, 3, 4, 5, 6, 7
VEC_ROWS = 16

TILE_INPROJ = 512
TILE_WKV = 4 * CHUNK
PREP_STAGES_PER_CHAIN_STAGE = 3
TILE_MIX = 512
TILE_MLP = 1024


def _bdot(a, b):
    return jnp.dot(a.astype(BF16), b.astype(BF16), preferred_element_type=F32)


def _bdot_nt(a, b):
    return lax.dot_general(a.astype(BF16), b.astype(BF16), (((1,), (1,)), ((), ())),
                           preferred_element_type=F32)


def _split2(a):
    hi = a.astype(BF16)
    lo = (a - hi.astype(F32)).astype(BF16)
    return hi, lo


def _dot_sel_rhs(a, sel):
    hi, lo = _split2(a)
    d = functools.partial(jnp.dot, preferred_element_type=F32)
    return d(hi, sel) + d(lo, sel)


def _dot_sel_lhs(sel, a):
    hi, lo = _split2(a)
    d = functools.partial(jnp.dot, preferred_element_type=F32)
    return d(sel, hi) + d(sel, lo)


def _rms(x, g):
    return x * lax.rsqrt(jnp.mean(x * x, axis=-1, keepdims=True) + RMS_EPS) * g


def _sigmoid(x):
    return 1.0 / (1.0 + jnp.exp(-x))


def _softplus(x):
    return jnp.maximum(x, 0.0) + jnp.log(1.0 + jnp.exp(-jnp.abs(x)))


def _shift_rows(x, carry_rows, n):
    nc = carry_rows.shape[0]
    out = pltpu.roll(x, n, 0)
    rows = lax.broadcasted_iota(jnp.int32, (nc, 1), 0)
    head = out[:nc]
    for j in range(n):
        head = jnp.where(rows == j, carry_rows[nc - n + j:nc - n + j + 1, :], head)
    return jnp.concatenate([head, out[nc:]], axis=0)


def _const_spec(*shape):
    return pl.BlockSpec(shape, lambda *_: (0,) * len(shape), pipeline_mode=pl.Buffered(1))


def _memkv_kernel(mem_ref, g_ref, w_ref, k_ref, v_ref):
    m = _rms(mem_ref[0], g_ref[...])
    kv = _bdot(m, w_ref[...])
    k_ref[0] = kv[:, :D_MODEL].astype(BF16)
    v_ref[0] = kv[:, D_MODEL:].astype(BF16)


def _memkv(mem, g, w_kv):
    b, n, d = mem.shape
    return pl.pallas_call(
        _memkv_kernel,
        grid=(b,),
        in_specs=[pl.BlockSpec((1, n, d), lambda i: (i, 0, 0)),
                  pl.BlockSpec((1, d), lambda i: (0, 0)),
                  pl.BlockSpec((d, 2 * d), lambda i: (0, 0))],
        out_specs=[pl.BlockSpec((1, n, d), lambda i: (i, 0, 0))] * 2,
        out_shape=[jax.ShapeDtypeStruct((b, n, d), BF16)] * 2,
        compiler_params=pltpu.CompilerParams(dimension_semantics=("arbitrary",),
                                             vmem_limit_bytes=VMEM_LIMIT),
        name="memkv",
    )(mem, g, w_kv)


def _inproj_kernel(x_ref, nm_ref, wrw_ref, wconv_ref, mu_ref, vec_ref,
                   ww_ref, wa_ref, wg_ref, bd_ref, tri_ref,
                   rt_ref, at_ref, bt_ref, kt_ref, bh_ref, kh_ref, v_ref, wc_ref,
                   g_ref, bonus_ref, ob_ref,
                   pcarry, ucarry):
    @pl.when(pl.program_id(1) == 0)
    def _():
        pcarry[...] = jnp.zeros_like(pcarry)
        ucarry[...] = jnp.zeros_like(ucarry)

    t = x_ref.shape[1]
    h = _rms(x_ref[0], nm_ref[...]).astype(BF16)

    pc = jnp.dot(h, wconv_ref[...], preferred_element_type=F32)
    u = pc[:, D_CONV:2 * D_CONV] * pc[:, 2 * D_CONV:]
    uc = ucarry[...]
    conv = (vec_ref[ROW_CONV:ROW_CONV + 1, :] * _shift_rows(u, uc, 2)
            + vec_ref[ROW_CONV + 1:ROW_CONV + 2, :] * _shift_rows(u, uc, 1)
            + vec_ref[ROW_CONV + 2:ROW_CONV + 3, :] * u)
    ob_ref[0] = (pc[:, :D_CONV] * conv).astype(BF16)
    ucarry[...] = u[t - 8:, :]

    p = jnp.dot(h, wrw_ref[...], preferred_element_type=F32)
    ps = _shift_rows(p, pcarry[...], 1)
    pcarry[...] = p[t - 8:, :]
    xm = p + (ps - p) * mu_ref[...]
    r = xm[:, :D_RWKV]
    k = xm[:, D_RWKV:2 * D_RWKV]
    v = xm[:, 2 * D_RWKV:3 * D_RWKV]
    wd = xm[:, WD_OFF:AD_OFF]
    ad = xm[:, AD_OFF:GD_OFF]
    gd = xm[:, GD_OFF:]

    def vec(row):
        return vec_ref[row:row + 1, :]

    log_w = -_softplus(-(vec(ROW_W0) + _bdot(jnp.tanh(wd), ww_ref[...]))) - 0.5
    ld = -jnp.exp(log_w)
    a = _sigmoid(vec(ROW_A0) + _bdot(ad, wa_ref[...]))
    g_ref[0] = _bdot(_sigmoid(gd), wg_ref[...])
    bd = bd_ref[...]
    kk = k * vec(ROW_KK)
    kk = kk * lax.rsqrt(jnp.maximum(_dot_sel_rhs(kk * kk, bd), 1e-24))
    k2 = k * (1.0 + (a - 1.0) * vec(ROW_KA))
    bonus_ref[0] = _bdot(r * k2 * vec(ROW_RK), bd) * v + vec(ROW_LNB)
    v_ref[0] = v.astype(BF16)
    na = -kk
    nb = kk * a

    tri = tri_ref[...]
    for c in range(t // CHUNK):
        sl = slice(c * CHUNK, (c + 1) * CHUNK)
        ldc = ld[sl]
        cum = _dot_sel_lhs(tri, ldc)
        tot = cum[CHUNK - 1:CHUNK, :]
        e_inc = jnp.exp(cum)
        e_inv = jnp.exp(-cum)
        e_prev = jnp.exp(cum - ldc)
        e_end = jnp.exp(tot - cum)
        rt_ref[0, sl, :] = (r[sl] * e_inc).astype(BF16)
        at_ref[0, sl, :] = (na[sl] * e_prev).astype(BF16)
        bt_ref[0, sl, :] = (nb[sl] * e_inv).astype(BF16)
        kt_ref[0, sl, :] = (k2[sl] * e_inv).astype(BF16)
        bh_ref[0, sl, :] = (nb[sl] * e_end).astype(BF16)
        kh_ref[0, sl, :] = (k2[sl] * e_end).astype(BF16)
        wc_ref[0, c] = jnp.exp(tot)


def _inproj(x, nm, wrw, wconv, mu, vec, lora, bd, tri, tile):
    b, s, d = x.shape
    nt = s // tile
    const = _const_spec
    tok = lambda w: pl.BlockSpec((1, tile, w), lambda i, j: (i, j, 0))
    outs = ([jax.ShapeDtypeStruct((b, s, D_RWKV), BF16)] * 7
            + [jax.ShapeDtypeStruct((b, s // CHUNK, 1, D_RWKV), F32)]
            + [jax.ShapeDtypeStruct((b, s, D_RWKV), F32)] * 2
            + [jax.ShapeDtypeStruct((b, s, D_CONV), BF16)])
    out_specs = ([tok(D_RWKV)] * 7
                 + [pl.BlockSpec((1, tile // CHUNK, 1, D_RWKV), lambda i, j: (i, j, 0, 0))]
                 + [tok(D_RWKV)] * 2 + [tok(D_CONV)])
    return pl.pallas_call(
        _inproj_kernel,
        grid=(b, nt),
        in_specs=[tok(d), const(1, d), const(d, RW_COLS), const(d, 3 * D_CONV),
                  const(1, RW_COLS), const(VEC_ROWS, D_RWKV)]
                 + [const(*w.shape) for w in lora]
                 + [const(D_RWKV, D_RWKV), const(CHUNK, CHUNK)],
        out_specs=out_specs,
        out_shape=outs,
        scratch_shapes=[pltpu.VMEM((8, RW_COLS), F32), pltpu.VMEM((8, D_CONV), F32)],
        compiler_params=pltpu.CompilerParams(dimension_semantics=("arbitrary", "arbitrary"),
                                             vmem_limit_bytes=VMEM_LIMIT),
        name="inproj",
    )(x, nm, wrw, wconv, mu, vec, *lora, bd, tri)


def _pair_expand(x):
    even = lax.broadcasted_iota(jnp.int32, x.shape, 1) < RWKV_HEAD
    zero = jnp.zeros_like(x)
    return jnp.concatenate([jnp.where(even, x, zero), jnp.where(even, zero, x)], axis=0)


def _rows(*parts):
    return jnp.concatenate(parts, axis=0)


def _wkv_masks():
    row = lax.broadcasted_iota(jnp.int32, (CHUNK, PAIR), 0)
    col = lax.broadcasted_iota(jnp.int32, (CHUNK, PAIR), 1) % RWKV_HEAD
    row2 = lax.broadcasted_iota(jnp.int32, (PAIR, PAIR), 0) < RWKV_HEAD
    col2 = lax.broadcasted_iota(jnp.int32, (PAIR, PAIR), 1) < RWKV_HEAD
    return dict(strict=col < row, incl=col <= row, eye=(col == row).astype(F32),
                same_head=row2 == col2)


def _wkv_prep(refs, wc_ref, c, inst, m, out):
    rows = slice(c * CHUNK, (c + 1) * CHUNK)
    load = lambda ref: [ref[b, rows, p * PAIR:(p + 1) * PAIR] for b, p in inst]
    r, a, bt, kt, bh, kh, v = (load(ref) for ref in refs)
    n = range(len(inst))
    ex = lambda x: _pair_expand(x.astype(BF16))
    g = [_bdot_nt(_rows(a[i], r[i]), _rows(_pair_expand(bt[i]), _pair_expand(kt[i]))) for i in n]
    yield
    lab = [jnp.where(m["strict"], g[i][:CHUNK, :PAIR], 0.0) for i in n]
    lak = [jnp.where(m["strict"], g[i][:CHUNK, PAIR:], 0.0) for i in n]
    mrb = [jnp.where(m["incl"], g[i][CHUNK:, :PAIR], 0.0) for i in n]
    mrk = [jnp.where(m["incl"], g[i][CHUNK:, PAIR:], 0.0) for i in n]
    npow = [_bdot(lab[i], ex(lab[i])) for i in n]
    tinv = [m["eye"] + lab[i] for i in n]
    lmv = [_bdot(_rows(lak[i], mrk[i]), _pair_expand(v[i])) for i in n]
    yield
    for _ in range(4):
        prod = [_bdot(_rows(tinv[i], npow[i]), ex(npow[i])) for i in n]
        tinv = [tinv[i] + prod[i][:CHUNK] for i in n]
        npow = [prod[i][CHUNK:] for i in n]
        yield
    tinv = [tinv[i] + _bdot(tinv[i], ex(npow[i])) for i in n]
    yield
    pq = [_bdot(tinv[i], jnp.concatenate([_pair_expand(a[i]), ex(lmv[i][:CHUNK])], axis=1))
          for i in n]
    bkt = [_rows(bh[i], kh[i]).astype(F32).T.astype(BF16) for i in n]
    wcol = [jnp.broadcast_to(wc_ref[b, c, :, p * PAIR:(p + 1) * PAIR], (PAIR, PAIR)).T
            for b, p in inst]
    out.update(p=[pq[i][:, :PAIR].astype(BF16) for i in n], q=[pq[i][:, PAIR:] for i in n],
               r=r, v=v, bkt=bkt, wcol=wcol, mrb=mrb, mv=[lmv[i][CHUNK:] for i in n])
    yield


def _wkv_chain(pre, z, y_ref, c, inst, m):
    rows = slice(c * CHUNK, (c + 1) * CHUNK)
    n = range(len(inst))
    pr = [_bdot(_rows(pre["p"][i], pre["r"][i]), z[i]) for i in n]
    yield
    u = [(pr[i][:CHUNK] + pre["q"][i]).astype(BF16) for i in n]
    for i in n:
        z[i] = pre["wcol"][i] * z[i] + jnp.where(
            m["same_head"], _bdot(pre["bkt"][i], _rows(u[i], pre["v"][i])), 0.0)
    yield
    for i, (b, p) in enumerate(inst):
        y_ref[b, rows, p * PAIR:(p + 1) * PAIR] = (
            pr[i][CHUNK:] + _bdot(pre["mrb"][i], _pair_expand(u[i])) + pre["mv"][i])
    yield


def _interleave(primary, secondary, ratio):
    for _ in primary:
        for _ in itertools.islice(secondary, ratio):
            pass
    for _ in secondary:
        pass


def _wkv_kernel(rt_ref, at_ref, bt_ref, kt_ref, bh_ref, kh_ref, v_ref, wc_ref, y_ref, state):
    @pl.when(pl.program_id(0) == 0)
    def _():
        state[...] = jnp.zeros_like(state)

    nb, t, _ = rt_ref.shape
    nc = t // CHUNK
    inst = [(b, p) for b in range(nb) for p in range(N_PAIRS)]
    m = _wkv_masks()
    refs = (rt_ref, at_ref, bt_ref, kt_ref, bh_ref, kh_ref, v_ref)
    z = [state[b, p] for b, p in inst]
    pre = [dict() for _ in range(nc)]
    for _ in _wkv_prep(refs, wc_ref, 0, inst, m, pre[0]):
        pass
    for c in range(nc):
        nxt = _wkv_prep(refs, wc_ref, c + 1, inst, m, pre[c + 1]) if c + 1 < nc else iter(())
        _interleave(_wkv_chain(pre[c], z, y_ref, c, inst, m), nxt, PREP_STAGES_PER_CHAIN_STAGE)
    for i, (b, p) in enumerate(inst):
        state[b, p] = z[i]


def _wkv(rt, at, bt, kt, bh, kh, v, wc, tile):
    b, s, d = rt.shape
    tok = pl.BlockSpec((b, tile, d), lambda j: (0, j, 0))
    return pl.pallas_call(
        _wkv_kernel,
        grid=(s // tile,),
        in_specs=[tok] * 7 + [pl.BlockSpec((b, tile // CHUNK, 1, d), lambda j: (0, j, 0, 0))],
        out_specs=tok,
        out_shape=jax.ShapeDtypeStruct((b, s, d), F32),
        scratch_shapes=[pltpu.VMEM((b, N_PAIRS, PAIR, PAIR), F32)],
        compiler_params=pltpu.CompilerParams(dimension_semantics=("arbitrary",),
                                             vmem_limit_bytes=VMEM_LIMIT),
        name="wkv",
    )(rt, at, bt, kt, bh, kh, v, wc)


def _mix_kernel(x_ref, y_ref, g_ref, bonus_ref, ob_ref, k_ref, v_ref, vec_ref, nm_ref, bg_ref,
                wgate_ref, wpa_ref, wpb_ref, wom_ref, nx_ref, wq_ref, wxo_ref, bd_ref, out_ref):
    x = x_ref[0]
    h = _rms(x, nm_ref[...]).astype(BF16)
    sg = _sigmoid(jnp.dot(h, wgate_ref[...], preferred_element_type=F32) + bg_ref[...])

    bd = bd_ref[...]
    y = y_ref[0]
    yc = y - _dot_sel_rhs(y, bd) * (1.0 / RWKV_HEAD)
    var = _bdot(yc * yc, bd) * (1.0 / RWKV_HEAD)
    yn = yc * lax.rsqrt(var + LN_X_EPS)
    oa = (yn * vec_ref[ROW_LNW:ROW_LNW + 1, :] + bonus_ref[0]) * g_ref[0]

    merged = (sg[:, :D_MODEL] * _bdot(oa, wpa_ref[...])
              + sg[:, D_MODEL:] * jnp.dot(ob_ref[0], wpb_ref[...], preferred_element_type=F32))
    x1 = x + _bdot(merged, wom_ref[...])

    q = _bdot(_rms(x1, nx_ref[...]), wq_ref[...]) * (XHEAD_DIM ** -0.5)
    heads = []
    for hh in range(N_XHEADS):
        cols = slice(hh * XHEAD_DIM, (hh + 1) * XHEAD_DIM)
        sc = _bdot_nt(q[:, cols], k_ref[0, :, cols])
        e = jnp.exp(sc - jnp.max(sc, axis=-1, keepdims=True))
        pr = e / jnp.sum(e, axis=-1, keepdims=True)
        heads.append(_bdot(pr, v_ref[0, :, cols]))
    o = jnp.concatenate(heads, axis=-1)
    out_ref[0] = x1 + _bdot(o, wxo_ref[...])


def _mix(x, y, g, bonus, ob, kmem, vmem, vec, nm, bg, wgate, wpa, wpb, wom, nx, wq, wxo, bd, tile):
    b, s, d = x.shape
    n_mem = kmem.shape[1]
    const = _const_spec
    tok = lambda w: pl.BlockSpec((1, tile, w), lambda i, j: (i, j, 0))
    memspec = pl.BlockSpec((1, n_mem, d), lambda i, j: (i, 0, 0))
    return pl.pallas_call(
        _mix_kernel,
        grid=(b, s // tile),
        in_specs=[tok(d), tok(D_RWKV), tok(D_RWKV), tok(D_RWKV), tok(D_CONV), memspec, memspec,
                  const(VEC_ROWS, D_RWKV), const(1, d), const(1, 2 * d), const(d, 2 * d),
                  const(D_RWKV, d), const(D_CONV, d), const(d, d), const(1, d), const(d, d),
                  const(d, d), const(D_RWKV, D_RWKV)],
        out_specs=tok(d),
        out_shape=jax.ShapeDtypeStruct((b, s, d), F32),
        compiler_params=pltpu.CompilerParams(dimension_semantics=("arbitrary", "arbitrary"),
                                             vmem_limit_bytes=VMEM_LIMIT),
        name="mix",
    )(x, y, g, bonus, ob, kmem, vmem, vec, nm, bg, wgate, wpa, wpb, wom, nx, wq, wxo, bd)


def _mlp_kernel(x_ref, nm_ref, wup_ref, wdown_ref, nf_ref, out_ref):
    x = x_ref[...]
    h = _rms(x, nm_ref[...]).astype(BF16)
    acc = x
    for c in range(D_FF // D_MODEL):
        cols = slice(c * D_MODEL, (c + 1) * D_MODEL)
        up = jnp.maximum(jnp.dot(h, wup_ref[:, cols], preferred_element_type=F32), 0.0)
        acc = acc + _bdot(up * up, wdown_ref[cols, :])
    out_ref[...] = _rms(acc, nf_ref[...])


def _mlp(x, nm, wup, wdown, nf, tile):
    n, d = x.shape
    const = _const_spec
    tok = pl.BlockSpec((tile, d), lambda i: (i, 0))
    return pl.pallas_call(
        _mlp_kernel,
        grid=(n // tile,),
        in_specs=[tok, const(1, d), const(d, D_FF), const(D_FF, d), const(1, d)],
        out_specs=tok,
        out_shape=jax.ShapeDtypeStruct((n, d), F32),
        compiler_params=pltpu.CompilerParams(dimension_semantics=("arbitrary",),
                                             vmem_limit_bytes=VMEM_LIMIT),
        name="mlp",
    )(x, nm, wup, wdown, nf)


def _pad_cols(w, width):
    return jnp.pad(w, ((0, 0), (0, width - w.shape[1])))


def _pad_rows(w, height):
    return jnp.pad(w, ((0, height - w.shape[0]), (0, 0)))


def _layer(x, kmem, vmem, norm_mix, w_in, b_gate, mu_shift, w0, w_lora_w, a0, w_lora_a, w_lora_g,
           k_k, k_a, r_k, ln_x_w, ln_x_b, conv_w, w_proj_a, w_proj_b, w_out_mix, norm_xattn,
           w_q, w_xo, bd, tri):
    c_wd = 3 * D_RWKV
    c_ad = c_wd + DECAY_LORA
    c_gd = c_ad + AAA_LORA
    c_conv = c_gd + GATE_LORA
    c_gate = c_conv + 3 * D_CONV
    wrw = jnp.concatenate([w_in[:, :c_wd], _pad_cols(w_in[:, c_wd:c_ad], LANES),
                           _pad_cols(w_in[:, c_ad:c_gd], LANES),
                           _pad_cols(w_in[:, c_gd:c_conv], 2 * LANES)], axis=1).astype(BF16)
    mu = jnp.concatenate([mu_shift[None, :c_wd], _pad_cols(mu_shift[None, c_wd:c_ad], LANES),
                          _pad_cols(mu_shift[None, c_ad:c_gd], LANES),
                          _pad_cols(mu_shift[None, c_gd:], 2 * LANES)], axis=1)
    wconv = w_in[:, c_conv:c_gate].astype(BF16)
    wgate = w_in[:, c_gate:].astype(BF16)
    lora = [_pad_rows(w, rows).astype(BF16)
            for w, rows in ((w_lora_w, LANES), (w_lora_a, LANES), (w_lora_g, 2 * LANES))]
    vec = jnp.stack([w0, a0, k_k, k_a, r_k.reshape(-1), ln_x_b, ln_x_w,
                     conv_w[0, 0], conv_w[1, 0], conv_w[2, 0]])
    vec = _pad_rows(vec, VEC_ROWS)

    rt, at, bt, kt, bh, kh, v, wc, g, bonus, ob = _inproj(
        x, norm_mix[None], wrw, wconv, mu, vec, lora, bd, tri, tile=TILE_INPROJ)
    y = _wkv(rt, at, bt, kt, bh, kh, v, wc, tile=TILE_WKV)
    return _mix(x, y, g, bonus, ob, kmem, vmem, vec, norm_mix[None], b_gate[None], wgate,
                w_proj_a.astype(BF16), w_proj_b.astype(BF16), w_out_mix.astype(BF16),
                norm_xattn[None], w_q.astype(BF16), w_xo.astype(BF16), bd, tile=TILE_MIX)


def kernel(x, mem, norm_mix, w_in, b_gate, mu_shift, w0, w_lora_w, a0, w_lora_a, w_lora_g, k_k, k_a, r_k, ln_x_w, ln_x_b, conv_w, w_proj_a, w_proj_b, w_out_mix, norm_xattn, norm_mem, w_q, w_kv, w_xo, norm_mlp, w_up, w_down, norm_final):
    assert w_in.shape[0] == 1, "the MLP kernel fuses the final norm: single-layer trunk only"
    bsz, s, d = x.shape
    head = jnp.arange(D_RWKV) // RWKV_HEAD
    bd = (head[:, None] == head[None, :]).astype(BF16)
    step = jnp.arange(CHUNK)
    tri = (step[None, :] <= step[:, None]).astype(BF16)
    kmem, vmem = _memkv(mem, norm_mem[0][None], w_kv[0].astype(BF16))
    x = _layer(x, kmem, vmem, norm_mix[0], w_in[0], b_gate[0], mu_shift[0], w0[0], w_lora_w[0],
               a0[0], w_lora_a[0], w_lora_g[0], k_k[0], k_a[0], r_k[0], ln_x_w[0], ln_x_b[0],
               conv_w[0], w_proj_a[0], w_proj_b[0], w_out_mix[0], norm_xattn[0], w_q[0],
               w_xo[0], bd, tri)
    return _mlp(x.reshape(bsz * s, d), norm_mlp[0][None], w_up[0].astype(BF16),
                w_down[0].astype(BF16), norm_final[None], tile=TILE_MLP).reshape(bsz, s, d)
```

```python
import functools
import itertools

import jax
import jax.numpy as jnp
from jax import lax
from jax.experimental import pallas as pl
from jax.experimental.pallas import tpu as pltpu

F32 = jnp.float32
BF16 = jnp.bfloat16

D_MODEL = 1024
D_RWKV = 512
RWKV_HEAD = 64
DECAY_LORA = 64
AAA_LORA = 64
GATE_LORA = 160
LN_X_EPS = 64e-5
D_CONV = 512
N_XHEADS = 4
XHEAD_DIM = D_MODEL // N_XHEADS
D_FF = 4 * D_MODEL
RMS_EPS = 1e-6

LANES = 128
CHUNK = 64
PAIR = 2 * RWKV_HEAD
N_PAIRS = D_RWKV // PAIR
WD_OFF, AD_OFF, GD_OFF = 3 * D_RWKV, 3 * D_RWKV + LANES, 3 * D_RWKV + 2 * LANES
RW_COLS = GD_OFF + 2 * LANES
VMEM_LIMIT = 56 * 1024 * 1024

ROW_W0, ROW_A0, ROW_KK, ROW_KA, ROW_RK, ROW_LNB, ROW_LNW, ROW_CONV = 0, 1, 2, 3, 4, 5, 6, 7
VEC_ROWS = 16

TILE_INPROJ = 512
TILE_WKV = 4 * CHUNK
PREP_STAGES_PER_CHAIN_STAGE = 3
TILE_MIX = 512
TILE_MLP = 1024


def _bdot(a, b):
    return jnp.dot(a.astype(BF16), b.astype(BF16), preferred_element_type=F32)


def _bdot_nt(a, b):
    return lax.dot_general(a.astype(BF16), b.astype(BF16), (((1,), (1,)), ((), ())),
                           preferred_element_type=F32)


def _split2(a):
    hi = a.astype(BF16)
    lo = (a - hi.astype(F32)).astype(BF16)
    return hi, lo


def _head_sums(a):
    low = lax.broadcasted_iota(jnp.int32, (a.shape[0], LANES), 1) < RWKV_HEAD
    out = []
    for j in range(a.shape[1] // LANES):
        slab = a[:, j * LANES:(j + 1) * LANES]
        s_lo = jnp.sum(jnp.where(low, slab, 0.0), axis=-1, keepdims=True)
        s_hi = jnp.sum(jnp.where(low, 0.0, slab), axis=-1, keepdims=True)
        out.append(jnp.where(low, s_lo, s_hi))
    return jnp.concatenate(out, axis=-1)


def _dot_sel_lhs(sel, a):
    hi, lo = _split2(a)
    d = functools.partial(jnp.dot, preferred_element_type=F32)
    return d(sel, hi) + d(sel, lo)


def _rms(x, g):
    return x * lax.rsqrt(jnp.mean(x * x, axis=-1, keepdims=True) + RMS_EPS) * g


def _sigmoid(x):
    return 1.0 / (1.0 + jnp.exp(-x))


def _softplus(x):
    return jnp.maximum(x, 0.0) + jnp.log(1.0 + jnp.exp(-jnp.abs(x)))


def _shift_rows(x, carry_rows, n):
    nc = carry_rows.shape[0]
    out = pltpu.roll(x, n, 0)
    rows = lax.broadcasted_iota(jnp.int32, (nc, 1), 0)
    head = out[:nc]
    for j in range(n):
        head = jnp.where(rows == j, carry_rows[nc - n + j:nc - n + j + 1, :], head)
    return jnp.concatenate([head, out[nc:]], axis=0)


def _const_spec(*shape):
    return pl.BlockSpec(shape, lambda *_: (0,) * len(shape), pipeline_mode=pl.Buffered(1))


def _memkv_kernel(mem_ref, g_ref, w_ref, k_ref, v_ref):
    m = _rms(mem_ref[0], g_ref[...])
    kv = _bdot(m, w_ref[...])
    k_ref[0] = kv[:, :D_MODEL].astype(BF16)
    v_ref[0] = kv[:, D_MODEL:].astype(BF16)


def _memkv(mem, g, w_kv):
    b, n, d = mem.shape
    return pl.pallas_call(
        _memkv_kernel,
        grid=(b,),
        in_specs=[pl.BlockSpec((1, n, d), lambda i: (i, 0, 0)),
                  pl.BlockSpec((1, d), lambda i: (0, 0)),
                  pl.BlockSpec((d, 2 * d), lambda i: (0, 0))],
        out_specs=[pl.BlockSpec((1, n, d), lambda i: (i, 0, 0))] * 2,
        out_shape=[jax.ShapeDtypeStruct((b, n, d), BF16)] * 2,
        compiler_params=pltpu.CompilerParams(dimension_semantics=("arbitrary",),
                                             vmem_limit_bytes=VMEM_LIMIT),
        name="memkv",
    )(mem, g, w_kv)


def _inproj_kernel(x_ref, nm_ref, wrw_ref, wconv_ref, mu_ref, vec_ref,
                   ww_ref, wa_ref, wg_ref, tri_ref,
                   rt_ref, at_ref, bt_ref, kt_ref, bh_ref, kh_ref, v_ref, wc_ref,
                   g_ref, bonus_ref, ob_ref,
                   pcarry, ucarry):
    @pl.when(pl.program_id(1) == 0)
    def _():
        pcarry[...] = jnp.zeros_like(pcarry)
        ucarry[...] = jnp.zeros_like(ucarry)

    t = x_ref.shape[1]
    h = _rms(x_ref[0], nm_ref[...]).astype(BF16)

    pc = jnp.dot(h, wconv_ref[...], preferred_element_type=F32)
    u = pc[:, D_CONV:2 * D_CONV] * pc[:, 2 * D_CONV:]
    uc = ucarry[...]
    conv = (vec_ref[ROW_CONV:ROW_CONV + 1, :] * _shift_rows(u, uc, 2)
            + vec_ref[ROW_CONV + 1:ROW_CONV + 2, :] * _shift_rows(u, uc, 1)
            + vec_ref[ROW_CONV + 2:ROW_CONV + 3, :] * u)
    ob_ref[0] = (pc[:, :D_CONV] * conv).astype(BF16)
    ucarry[...] = u[t - 8:, :]

    p = jnp.dot(h, wrw_ref[...], preferred_element_type=F32)
    ps = _shift_rows(p, pcarry[...], 1)
    pcarry[...] = p[t - 8:, :]
    xm = p + (ps - p) * mu_ref[...]
    r = xm[:, :D_RWKV]
    k = xm[:, D_RWKV:2 * D_RWKV]
    v = xm[:, 2 * D_RWKV:3 * D_RWKV]
    wd = xm[:, WD_OFF:AD_OFF]
    ad = xm[:, AD_OFF:GD_OFF]
    gd = xm[:, GD_OFF:]

    def vec(row):
        return vec_ref[row:row + 1, :]

    log_w = -_softplus(-(vec(ROW_W0) + _bdot(jnp.tanh(wd), ww_ref[...]))) - 0.5
    ld = -jnp.exp(log_w)
    a = _sigmoid(vec(ROW_A0) + _bdot(ad, wa_ref[...]))
    g_ref[0] = _bdot(_sigmoid(gd), wg_ref[...])
    kk = k * vec(ROW_KK)
    kk = kk * lax.rsqrt(jnp.maximum(_head_sums(kk * kk), 1e-24))
    k2 = k * (1.0 + (a - 1.0) * vec(ROW_KA))
    bonus_ref[0] = _head_sums(r * k2 * vec(ROW_RK)) * v + vec(ROW_LNB)
    v_ref[0] = v.astype(BF16)
    na = -kk
    nb = kk * a

    tri = tri_ref[...]
    for c in range(t // CHUNK):
        sl = slice(c * CHUNK, (c + 1) * CHUNK)
        ldc = ld[sl]
        cum = _dot_sel_lhs(tri, ldc)
        tot = cum[CHUNK - 1:CHUNK, :]
        e_inc = jnp.exp(cum)
        e_inv = jnp.exp(-cum)
        e_prev = jnp.exp(cum - ldc)
        e_end = jnp.exp(tot - cum)
        rt_ref[0, sl, :] = (r[sl] * e_inc).astype(BF16)
        at_ref[0, sl, :] = (na[sl] * e_prev).astype(BF16)
        bt_ref[0, sl, :] = (nb[sl] * e_inv).astype(BF16)
        kt_ref[0, sl, :] = (k2[sl] * e_inv).astype(BF16)
        bh_ref[0, sl, :] = (nb[sl] * e_end).astype(BF16)
        kh_ref[0, sl, :] = (k2[sl] * e_end).astype(BF16)
        wc_ref[0, c] = jnp.exp(tot)


def _inproj(x, nm, wrw, wconv, mu, vec, lora, tri, tile):
    b, s, d = x.shape
    nt = s // tile
    const = _const_spec
    tok = lambda w: pl.BlockSpec((1, tile, w), lambda i, j: (i, j, 0))
    outs = ([jax.ShapeDtypeStruct((b, s, D_RWKV), BF16)] * 7
            + [jax.ShapeDtypeStruct((b, s // CHUNK, 1, D_RWKV), F32)]
            + [jax.ShapeDtypeStruct((b, s, D_RWKV), F32)] * 2
            + [jax.ShapeDtypeStruct((b, s, D_CONV), BF16)])
    out_specs = ([tok(D_RWKV)] * 7
                 + [pl.BlockSpec((1, tile // CHUNK, 1, D_RWKV), lambda i, j: (i, j, 0, 0))]
                 + [tok(D_RWKV)] * 2 + [tok(D_CONV)])
    return pl.pallas_call(
        _inproj_kernel,
        grid=(b, nt),
        in_specs=[tok(d), const(1, d), const(d, RW_COLS), const(d, 3 * D_CONV),
                  const(1, RW_COLS), const(VEC_ROWS, D_RWKV)]
                 + [const(*w.shape) for w in lora]
                 + [const(CHUNK, CHUNK)],
        out_specs=out_specs,
        out_shape=outs,
        scratch_shapes=[pltpu.VMEM((8, RW_COLS), F32), pltpu.VMEM((8, D_CONV), F32)],
        compiler_params=pltpu.CompilerParams(dimension_semantics=("arbitrary", "arbitrary"),
                                             vmem_limit_bytes=VMEM_LIMIT),
        name="inproj",
    )(x, nm, wrw, wconv, mu, vec, *lora, tri)


def _pair_expand(x):
    even = lax.broadcasted_iota(jnp.int32, x.shape, 1) < RWKV_HEAD
    zero = jnp.zeros_like(x)
    return jnp.concatenate([jnp.where(even, x, zero), jnp.where(even, zero, x)], axis=0)


def _rows(*parts):
    return jnp.concatenate(parts, axis=0)


def _wkv_masks():
    row = lax.broadcasted_iota(jnp.int32, (CHUNK, PAIR), 0)
    col = lax.broadcasted_iota(jnp.int32, (CHUNK, PAIR), 1) % RWKV_HEAD
    row2 = lax.broadcasted_iota(jnp.int32, (PAIR, PAIR), 0) < RWKV_HEAD
    col2 = lax.broadcasted_iota(jnp.int32, (PAIR, PAIR), 1) < RWKV_HEAD
    return dict(strict=col < row, incl=col <= row, eye=(col == row).astype(F32),
                same_head=row2 == col2)


def _wkv_prep(refs, wc_ref, c, inst, m, out):
    rows = slice(c * CHUNK, (c + 1) * CHUNK)
    load = lambda ref: [ref[b, rows, p * PAIR:(p + 1) * PAIR] for b, p in inst]
    r, a, bt, kt, bh, kh, v = (load(ref) for ref in refs)
    n = range(len(inst))
    ex = lambda x: _pair_expand(x.astype(BF16))
    g = [_bdot_nt(_rows(a[i], r[i]), _rows(_pair_expand(bt[i]), _pair_expand(kt[i]))) for i in n]
    yield
    lab = [jnp.where(m["strict"], g[i][:CHUNK, :PAIR], 0.0) for i in n]
    lak = [jnp.where(m["strict"], g[i][:CHUNK, PAIR:], 0.0) for i in n]
    mrb = [jnp.where(m["incl"], g[i][CHUNK:, :PAIR], 0.0) for i in n]
    mrk = [jnp.where(m["incl"], g[i][CHUNK:, PAIR:], 0.0) for i in n]
    npow = [_bdot(lab[i], ex(lab[i])) for i in n]
    tinv = [m["eye"] + lab[i] for i in n]
    lmv = [_bdot(_rows(lak[i], mrk[i]), _pair_expand(v[i])) for i in n]
    yield
    for _ in range(4):
        prod = [_bdot(_rows(tinv[i], npow[i]), ex(npow[i])) for i in n]
        tinv = [tinv[i] + prod[i][:CHUNK] for i in n]
        npow = [prod[i][CHUNK:] for i in n]
        yield
    tinv = [tinv[i] + _bdot(tinv[i], ex(npow[i])) for i in n]
    yield
    pq = [_bdot(tinv[i], jnp.concatenate([_pair_expand(a[i]), ex(lmv[i][:CHUNK])], axis=1))
          for i in n]
    bkt = [_rows(bh[i], kh[i]).astype(F32).T.astype(BF16) for i in n]
    wcol = [jnp.broadcast_to(wc_ref[b, c, :, p * PAIR:(p + 1) * PAIR], (PAIR, PAIR)).T
            for b, p in inst]
    out.update(p=[pq[i][:, :PAIR].astype(BF16) for i in n], q=[pq[i][:, PAIR:] for i in n],
               r=r, v=v, bkt=bkt, wcol=wcol, mrb=mrb, mv=[lmv[i][CHUNK:] for i in n])
    yield


def _wkv_chain(pre, z, y_ref, c, inst, m):
    rows = slice(c * CHUNK, (c + 1) * CHUNK)
    n = range(len(inst))
    pr = [_bdot(_rows(pre["p"][i], pre["r"][i]), z[i]) for i in n]
    yield
    u = [(pr[i][:CHUNK] + pre["q"][i]).astype(BF16) for i in n]
    for i in n:
        z[i] = pre["wcol"][i] * z[i] + jnp.where(
            m["same_head"], _bdot(pre["bkt"][i], _rows(u[i], pre["v"][i])), 0.0)
    yield
    for i, (b, p) in enumerate(inst):
        y_ref[b, rows, p * PAIR:(p + 1) * PAIR] = (
            pr[i][CHUNK:] + _bdot(pre["mrb"][i], _pair_expand(u[i])) + pre["mv"][i])
    yield


def _interleave(primary, secondary, ratio):
    for _ in primary:
        for _ in itertools.islice(secondary, ratio):
            pass
    for _ in secondary:
        pass


def _wkv_kernel(rt_ref, at_ref, bt_ref, kt_ref, bh_ref, kh_ref, v_ref, wc_ref, y_ref, state):
    @pl.when(pl.program_id(0) == 0)
    def _():
        state[...] = jnp.zeros_like(state)

    nb, t, _ = rt_ref.shape
    nc = t // CHUNK
    inst = [(b, p) for b in range(nb) for p in range(N_PAIRS)]
    m = _wkv_masks()
    refs = (rt_ref, at_ref, bt_ref, kt_ref, bh_ref, kh_ref, v_ref)
    z = [state[b, p] for b, p in inst]
    pre = [dict() for _ in range(nc)]
    for _ in _wkv_prep(refs, wc_ref, 0, inst, m, pre[0]):
        pass
    for c in range(nc):
        nxt = _wkv_prep(refs, wc_ref, c + 1, inst, m, pre[c + 1]) if c + 1 < nc else iter(())
        _interleave(_wkv_chain(pre[c], z, y_ref, c, inst, m), nxt, PREP_STAGES_PER_CHAIN_STAGE)
    for i, (b, p) in enumerate(inst):
        state[b, p] = z[i]


def _wkv(rt, at, bt, kt, bh, kh, v, wc, tile):
    b, s, d = rt.shape
    tok = pl.BlockSpec((b, tile, d), lambda j: (0, j, 0))
    return pl.pallas_call(
        _wkv_kernel,
        grid=(s // tile,),
        in_specs=[tok] * 7 + [pl.BlockSpec((b, tile // CHUNK, 1, d), lambda j: (0, j, 0, 0))],
        out_specs=tok,
        out_shape=jax.ShapeDtypeStruct((b, s, d), F32),
        scratch_shapes=[pltpu.VMEM((b, N_PAIRS, PAIR, PAIR), F32)],
        compiler_params=pltpu.CompilerParams(dimension_semantics=("arbitrary",),
                                             vmem_limit_bytes=VMEM_LIMIT),
        name="wkv",
    )(rt, at, bt, kt, bh, kh, v, wc)


def _mix_kernel(x_ref, y_ref, g_ref, bonus_ref, ob_ref, k_ref, v_ref, vec_ref, nm_ref, bg_ref,
                wgate_ref, wpa_ref, wpb_ref, wom_ref, nx_ref, wq_ref, wxo_ref, out_ref):
    x = x_ref[0]
    h = _rms(x, nm_ref[...]).astype(BF16)
    sg = _sigmoid(jnp.dot(h, wgate_ref[...], preferred_element_type=F32) + bg_ref[...])

    y = y_ref[0]
    yc = y - _head_sums(y) * (1.0 / RWKV_HEAD)
    var = _head_sums(yc * yc) * (1.0 / RWKV_HEAD)
    yn = yc * lax.rsqrt(var + LN_X_EPS)
    oa = (yn * vec_ref[ROW_LNW:ROW_LNW + 1, :] + bonus_ref[0]) * g_ref[0]

    merged = (sg[:, :D_MODEL] * _bdot(oa, wpa_ref[...])
              + sg[:, D_MODEL:] * jnp.dot(ob_ref[0], wpb_ref[...], preferred_element_type=F32))
    x1 = x + _bdot(merged, wom_ref[...])

    q = _bdot(_rms(x1, nx_ref[...]), wq_ref[...]) * (XHEAD_DIM ** -0.5)
    heads = []
    for hh in range(N_XHEADS):
        cols = slice(hh * XHEAD_DIM, (hh + 1) * XHEAD_DIM)
        sc = _bdot_nt(q[:, cols], k_ref[0, :, cols])
        e = jnp.exp(sc - jnp.max(sc, axis=-1, keepdims=True))
        pr = e / jnp.sum(e, axis=-1, keepdims=True)
        heads.append(_bdot(pr, v_ref[0, :, cols]))
    o = jnp.concatenate(heads, axis=-1)
    out_ref[0] = x1 + _bdot(o, wxo_ref[...])


def _mix(x, y, g, bonus, ob, kmem, vmem, vec, nm, bg, wgate, wpa, wpb, wom, nx, wq, wxo, tile):
    b, s, d = x.shape
    n_mem = kmem.shape[1]
    const = _const_spec
    tok = lambda w: pl.BlockSpec((1, tile, w), lambda i, j: (i, j, 0))
    memspec = pl.BlockSpec((1, n_mem, d), lambda i, j: (i, 0, 0))
    return pl.pallas_call(
        _mix_kernel,
        grid=(b, s // tile),
        in_specs=[tok(d), tok(D_RWKV), tok(D_RWKV), tok(D_RWKV), tok(D_CONV), memspec, memspec,
                  const(VEC_ROWS, D_RWKV), const(1, d), const(1, 2 * d), const(d, 2 * d),
                  const(D_RWKV, d), const(D_CONV, d), const(d, d), const(1, d), const(d, d),
                  const(d, d)],
        out_specs=tok(d),
        out_shape=jax.ShapeDtypeStruct((b, s, d), F32),
        compiler_params=pltpu.CompilerParams(dimension_semantics=("arbitrary", "arbitrary"),
                                             vmem_limit_bytes=VMEM_LIMIT),
        name="mix",
    )(x, y, g, bonus, ob, kmem, vmem, vec, nm, bg, wgate, wpa, wpb, wom, nx, wq, wxo)


def _mlp_kernel(x_ref, nm_ref, wup_ref, wdown_ref, nf_ref, out_ref):
    x = x_ref[...]
    h = _rms(x, nm_ref[...]).astype(BF16)
    acc = x
    for c in range(D_FF // D_MODEL):
        cols = slice(c * D_MODEL, (c + 1) * D_MODEL)
        up = jnp.maximum(jnp.dot(h, wup_ref[:, cols], preferred_element_type=F32), 0.0)
        acc = acc + _bdot(up * up, wdown_ref[cols, :])
    out_ref[...] = _rms(acc, nf_ref[...])


def _mlp(x, nm, wup, wdown, nf, tile):
    n, d = x.shape
    const = _const_spec
    tok = pl.BlockSpec((tile, d), lambda i: (i, 0))
    return pl.pallas_call(
        _mlp_kernel,
        grid=(n // tile,),
        in_specs=[tok, const(1, d), const(d, D_FF), const(D_FF, d), const(1, d)],
        out_specs=tok,
        out_shape=jax.ShapeDtypeStruct((n, d), F32),
        compiler_params=pltpu.CompilerParams(dimension_semantics=("arbitrary",),
                                             vmem_limit_bytes=VMEM_LIMIT),
        name="mlp",
    )(x, nm, wup, wdown, nf)


def _pad_cols(w, width):
    return jnp.pad(w, ((0, 0), (0, width - w.shape[1])))


def _pad_rows(w, height):
    return jnp.pad(w, ((0, height - w.shape[0]), (0, 0)))


def _layer(x, kmem, vmem, norm_mix, w_in, b_gate, mu_shift, w0, w_lora_w, a0, w_lora_a, w_lora_g,
           k_k, k_a, r_k, ln_x_w, ln_x_b, conv_w, w_proj_a, w_proj_b, w_out_mix, norm_xattn,
           w_q, w_xo, tri):
    c_wd = 3 * D_RWKV
    c_ad = c_wd + DECAY_LORA
    c_gd = c_ad + AAA_LORA
    c_conv = c_gd + GATE_LORA
    c_gate = c_conv + 3 * D_CONV
    wrw = jnp.concatenate([w_in[:, :c_wd], _pad_cols(w_in[:, c_wd:c_ad], LANES),
                           _pad_cols(w_in[:, c_ad:c_gd], LANES),
                           _pad_cols(w_in[:, c_gd:c_conv], 2 * LANES)], axis=1).astype(BF16)
    mu = jnp.concatenate([mu_shift[None, :c_wd], _pad_cols(mu_shift[None, c_wd:c_ad], LANES),
                          _pad_cols(mu_shift[None, c_ad:c_gd], LANES),
                          _pad_cols(mu_shift[None, c_gd:], 2 * LANES)], axis=1)
    wconv = w_in[:, c_conv:c_gate].astype(BF16)
    wgate = w_in[:, c_gate:].astype(BF16)
    lora = [_pad_rows(w, rows).astype(BF16)
            for w, rows in ((w_lora_w, LANES), (w_lora_a, LANES), (w_lora_g, 2 * LANES))]
    vec = jnp.stack([w0, a0, k_k, k_a, r_k.reshape(-1), ln_x_b, ln_x_w,
                     conv_w[0, 0], conv_w[1, 0], conv_w[2, 0]])
    vec = _pad_rows(vec, VEC_ROWS)

    rt, at, bt, kt, bh, kh, v, wc, g, bonus, ob = _inproj(
        x, norm_mix[None], wrw, wconv, mu, vec, lora, tri, tile=TILE_INPROJ)
    y = _wkv(rt, at, bt, kt, bh, kh, v, wc, tile=TILE_WKV)
    return _mix(x, y, g, bonus, ob, kmem, vmem, vec, norm_mix[None], b_gate[None], wgate,
                w_proj_a.astype(BF16), w_proj_b.astype(BF16), w_out_mix.astype(BF16),
                norm_xattn[None], w_q.astype(BF16), w_xo.astype(BF16), tile=TILE_MIX)


def kernel(x, mem, norm_mix, w_in, b_gate, mu_shift, w0, w_lora_w, a0, w_lora_a, w_lora_g, k_k, k_a, r_k, ln_x_w, ln_x_b, conv_w, w_proj_a, w_proj_b, w_out_mix, norm_xattn, norm_mem, w_q, w_kv, w_xo, norm_mlp, w_up, w_down, norm_final):
    assert w_in.shape[0] == 1, "the MLP kernel fuses the final norm: single-layer trunk only"
    bsz, s, d = x.shape
    step = jnp.arange(CHUNK)
    tri = (step[None, :] <= step[:, None]).astype(BF16)
    kmem, vmem = _memkv(mem, norm_mem[0][None], w_kv[0].astype(BF16))
    x = _layer(x, kmem, vmem, norm_mix[0], w_in[0], b_gate[0], mu_shift[0], w0[0], w_lora_w[0],
               a0[0], w_lora_a[0], w_lora_g[0], k_k[0], k_a[0], r_k[0], ln_x_w[0], ln_x_b[0],
               conv_w[0], w_proj_a[0], w_proj_b[0], w_out_mix[0], norm_xattn[0], w_q[0],
               w_xo[0], tri)
    return _mlp(x.reshape(bsz * s, d), norm_mlp[0][None], w_up[0].astype(BF16),
                w_down[0].astype(BF16), norm_final[None], tile=TILE_MLP).reshape(bsz, s, d)
```

```python
import functools
import itertools
import math

import jax
import jax.numpy as jnp
from jax import lax
from jax.experimental import pallas as pl
from jax.experimental.pallas import tpu as pltpu

F32 = jnp.float32
BF16 = jnp.bfloat16

D_MODEL = 1024
D_RWKV = 512
RWKV_HEAD = 64
DECAY_LORA = 64
AAA_LORA = 64
GATE_LORA = 160
LN_X_EPS = 64e-5
D_CONV = 512
N_XHEADS = 4
XHEAD_DIM = D_MODEL // N_XHEADS
D_FF = 4 * D_MODEL
RMS_EPS = 1e-6
EXP_MINUS_HALF = math.exp(-0.5)

LANES = 128
CHUNK = 64
PAIR = 2 * RWKV_HEAD
N_PAIRS = D_RWKV // PAIR
WD_OFF, AD_OFF, GD_OFF = 3 * D_RWKV, 3 * D_RWKV + LANES, 3 * D_RWKV + 2 * LANES
RW_COLS = GD_OFF + 2 * LANES
VMEM_LIMIT = 56 * 1024 * 1024

ROW_W0, ROW_A0, ROW_KK, ROW_KA, ROW_RK, ROW_LNB, ROW_LNW, ROW_CONV = 0, 1, 2, 3, 4, 5, 6, 7
VEC_ROWS = 16

TILE_INPROJ = 512
TILE_WKV = 4 * CHUNK
PREP_STAGES_PER_CHAIN_STAGE = 3
TILE_MIX = 512
TILE_MLP = 1024


def _bdot(a, b):
    return jnp.dot(a.astype(BF16), b.astype(BF16), preferred_element_type=F32)


def _bdot_nt(a, b):
    return lax.dot_general(a.astype(BF16), b.astype(BF16), (((1,), (1,)), ((), ())),
                           preferred_element_type=F32)


def _split2(a):
    hi = a.astype(BF16)
    lo = (a - hi.astype(F32)).astype(BF16)
    return hi, lo


def _head_sums(a):
    low = lax.broadcasted_iota(jnp.int32, (a.shape[0], LANES), 1) < RWKV_HEAD
    out = []
    for j in range(a.shape[1] // LANES):
        slab = a[:, j * LANES:(j + 1) * LANES]
        s_lo = jnp.sum(jnp.where(low, slab, 0.0), axis=-1, keepdims=True)
        s_hi = jnp.sum(jnp.where(low, 0.0, slab), axis=-1, keepdims=True)
        out.append(jnp.where(low, s_lo, s_hi))
    return jnp.concatenate(out, axis=-1)


def _dot_sel_lhs(sel, a):
    hi, lo = _split2(a)
    d = functools.partial(jnp.dot, preferred_element_type=F32)
    return d(sel, hi) + d(sel, lo)


def _rms(x, g):
    return x * lax.rsqrt(jnp.mean(x * x, axis=-1, keepdims=True) + RMS_EPS) * g


def _sigmoid(x):
    return 1.0 / (1.0 + jnp.exp(-x))


def _shift_rows(x, carry_rows, n):
    nc = carry_rows.shape[0]
    out = pltpu.roll(x, n, 0)
    rows = lax.broadcasted_iota(jnp.int32, (nc, 1), 0)
    head = out[:nc]
    for j in range(n):
        head = jnp.where(rows == j, carry_rows[nc - n + j:nc - n + j + 1, :], head)
    return jnp.concatenate([head, out[nc:]], axis=0)


def _const_spec(*shape):
    return pl.BlockSpec(shape, lambda *_: (0,) * len(shape), pipeline_mode=pl.Buffered(1))


def _memkv_kernel(mem_ref, g_ref, w_ref, k_ref, v_ref):
    m = _rms(mem_ref[0], g_ref[...])
    kv = _bdot(m, w_ref[...])
    k_ref[0] = kv[:, :D_MODEL].astype(BF16)
    v_ref[0] = kv[:, D_MODEL:].astype(BF16)


def _memkv(mem, g, w_kv):
    b, n, d = mem.shape
    return pl.pallas_call(
        _memkv_kernel,
        grid=(b,),
        in_specs=[pl.BlockSpec((1, n, d), lambda i: (i, 0, 0)),
                  pl.BlockSpec((1, d), lambda i: (0, 0)),
                  pl.BlockSpec((d, 2 * d), lambda i: (0, 0))],
        out_specs=[pl.BlockSpec((1, n, d), lambda i: (i, 0, 0))] * 2,
        out_shape=[jax.ShapeDtypeStruct((b, n, d), BF16)] * 2,
        compiler_params=pltpu.CompilerParams(dimension_semantics=("arbitrary",),
                                             vmem_limit_bytes=VMEM_LIMIT),
        name="memkv",
    )(mem, g, w_kv)


def _inproj_kernel(x_ref, nm_ref, wrw_ref, wconv_ref, mu_ref, vec_ref,
                   ww_ref, wa_ref, wg_ref, tri_ref,
                   rt_ref, at_ref, bt_ref, kt_ref, bh_ref, kh_ref, v_ref, wc_ref,
                   g_ref, bonus_ref, ob_ref,
                   pcarry, ucarry):
    @pl.when(pl.program_id(1) == 0)
    def _():
        pcarry[...] = jnp.zeros_like(pcarry)
        ucarry[...] = jnp.zeros_like(ucarry)

    t = x_ref.shape[1]
    h = _rms(x_ref[0], nm_ref[...]).astype(BF16)

    pc = jnp.dot(h, wconv_ref[...], preferred_element_type=F32)
    u = pc[:, D_CONV:2 * D_CONV] * pc[:, 2 * D_CONV:]
    uc = ucarry[...]
    conv = (vec_ref[ROW_CONV:ROW_CONV + 1, :] * _shift_rows(u, uc, 2)
            + vec_ref[ROW_CONV + 1:ROW_CONV + 2, :] * _shift_rows(u, uc, 1)
            + vec_ref[ROW_CONV + 2:ROW_CONV + 3, :] * u)
    ob_ref[0] = (pc[:, :D_CONV] * conv).astype(BF16)
    ucarry[...] = u[t - 8:, :]

    p = jnp.dot(h, wrw_ref[...], preferred_element_type=F32)
    ps = _shift_rows(p, pcarry[...], 1)
    pcarry[...] = p[t - 8:, :]
    xm = p + (ps - p) * mu_ref[...]
    r = xm[:, :D_RWKV]
    k = xm[:, D_RWKV:2 * D_RWKV]
    v = xm[:, 2 * D_RWKV:3 * D_RWKV]
    wd = xm[:, WD_OFF:AD_OFF]
    ad = xm[:, AD_OFF:GD_OFF]
    gd = xm[:, GD_OFF:]

    def vec(row):
        return vec_ref[row:row + 1, :]

    ld = -EXP_MINUS_HALF * _sigmoid(vec(ROW_W0) + _bdot(jnp.tanh(wd), ww_ref[...]))
    a = _sigmoid(vec(ROW_A0) + _bdot(ad, wa_ref[...]))
    g_ref[0] = _bdot(_sigmoid(gd), wg_ref[...])
    kk = k * vec(ROW_KK)
    kk = kk * lax.rsqrt(jnp.maximum(_head_sums(kk * kk), 1e-24))
    k2 = k * (1.0 + (a - 1.0) * vec(ROW_KA))
    bonus_ref[0] = _head_sums(r * k2 * vec(ROW_RK)) * v + vec(ROW_LNB)
    v_ref[0] = v.astype(BF16)
    na = -kk
    nb = kk * a

    tri = tri_ref[...]
    for c in range(t // CHUNK):
        sl = slice(c * CHUNK, (c + 1) * CHUNK)
        ldc = ld[sl]
        cum = _dot_sel_lhs(tri, ldc)
        tot = cum[CHUNK - 1:CHUNK, :]
        e_inc = jnp.exp(cum)
        e_inv = 1.0 / e_inc
        e_prev = jnp.exp(cum - ldc)
        e_tot = jnp.exp(tot)
        e_end = e_tot * e_inv
        rt_ref[0, sl, :] = (r[sl] * e_inc).astype(BF16)
        at_ref[0, sl, :] = (na[sl] * e_prev).astype(BF16)
        bt_ref[0, sl, :] = (nb[sl] * e_inv).astype(BF16)
        kt_ref[0, sl, :] = (k2[sl] * e_inv).astype(BF16)
        bh_ref[0, sl, :] = (nb[sl] * e_end).astype(BF16)
        kh_ref[0, sl, :] = (k2[sl] * e_end).astype(BF16)
        wc_ref[0, c] = e_tot


def _inproj(x, nm, wrw, wconv, mu, vec, lora, tri, tile):
    b, s, d = x.shape
    nt = s // tile
    const = _const_spec
    tok = lambda w: pl.BlockSpec((1, tile, w), lambda i, j: (i, j, 0))
    outs = ([jax.ShapeDtypeStruct((b, s, D_RWKV), BF16)] * 7
            + [jax.ShapeDtypeStruct((b, s // CHUNK, 1, D_RWKV), F32)]
            + [jax.ShapeDtypeStruct((b, s, D_RWKV), F32)] * 2
            + [jax.ShapeDtypeStruct((b, s, D_CONV), BF16)])
    out_specs = ([tok(D_RWKV)] * 7
                 + [pl.BlockSpec((1, tile // CHUNK, 1, D_RWKV), lambda i, j: (i, j, 0, 0))]
                 + [tok(D_RWKV)] * 2 + [tok(D_CONV)])
    return pl.pallas_call(
        _inproj_kernel,
        grid=(b, nt),
        in_specs=[tok(d), const(1, d), const(d, RW_COLS), const(d, 3 * D_CONV),
                  const(1, RW_COLS), const(VEC_ROWS, D_RWKV)]
                 + [const(*w.shape) for w in lora]
                 + [const(CHUNK, CHUNK)],
        out_specs=out_specs,
        out_shape=outs,
        scratch_shapes=[pltpu.VMEM((8, RW_COLS), F32), pltpu.VMEM((8, D_CONV), F32)],
        compiler_params=pltpu.CompilerParams(dimension_semantics=("arbitrary", "arbitrary"),
                                             vmem_limit_bytes=VMEM_LIMIT),
        name="inproj",
    )(x, nm, wrw, wconv, mu, vec, *lora, tri)


def _pair_expand(x):
    even = lax.broadcasted_iota(jnp.int32, x.shape, 1) < RWKV_HEAD
    zero = jnp.zeros_like(x)
    return jnp.concatenate([jnp.where(even, x, zero), jnp.where(even, zero, x)], axis=0)


def _rows(*parts):
    return jnp.concatenate(parts, axis=0)


def _wkv_masks():
    row = lax.broadcasted_iota(jnp.int32, (CHUNK, PAIR), 0)
    col = lax.broadcasted_iota(jnp.int32, (CHUNK, PAIR), 1) % RWKV_HEAD
    row2 = lax.broadcasted_iota(jnp.int32, (PAIR, PAIR), 0) < RWKV_HEAD
    col2 = lax.broadcasted_iota(jnp.int32, (PAIR, PAIR), 1) < RWKV_HEAD
    return dict(strict=col < row, incl=col <= row, eye=(col == row).astype(F32),
                same_head=row2 == col2)


def _wkv_prep(refs, wc_ref, c, inst, m, out):
    rows = slice(c * CHUNK, (c + 1) * CHUNK)
    load = lambda ref: [ref[b, rows, p * PAIR:(p + 1) * PAIR] for b, p in inst]
    r, a, bt, kt, bh, kh, v = (load(ref) for ref in refs)
    n = range(len(inst))
    ex = lambda x: _pair_expand(x.astype(BF16))
    g = [_bdot_nt(_rows(a[i], r[i]), _rows(_pair_expand(bt[i]), _pair_expand(kt[i]))) for i in n]
    yield
    lab = [jnp.where(m["strict"], g[i][:CHUNK, :PAIR], 0.0) for i in n]
    lak = [jnp.where(m["strict"], g[i][:CHUNK, PAIR:], 0.0) for i in n]
    mrb = [jnp.where(m["incl"], g[i][CHUNK:, :PAIR], 0.0) for i in n]
    mrk = [jnp.where(m["incl"], g[i][CHUNK:, PAIR:], 0.0) for i in n]
    npow = [_bdot(lab[i], ex(lab[i])) for i in n]
    tinv = [m["eye"] + lab[i] for i in n]
    lmv = [_bdot(_rows(lak[i], mrk[i]), _pair_expand(v[i])) for i in n]
    yield
    for _ in range(4):
        prod = [_bdot(_rows(tinv[i], npow[i]), ex(npow[i])) for i in n]
        tinv = [tinv[i] + prod[i][:CHUNK] for i in n]
        npow = [prod[i][CHUNK:] for i in n]
        yield
    tinv = [tinv[i] + _bdot(tinv[i], ex(npow[i])) for i in n]
    yield
    pq = [_bdot(tinv[i], jnp.concatenate([_pair_expand(a[i]), ex(lmv[i][:CHUNK])], axis=1))
          for i in n]
    bkt = [_rows(bh[i], kh[i]).astype(F32).T.astype(BF16) for i in n]
    wcol = [jnp.broadcast_to(wc_ref[b, c, :, p * PAIR:(p + 1) * PAIR], (PAIR, PAIR)).T
            for b, p in inst]
    out.update(p=[pq[i][:, :PAIR].astype(BF16) for i in n], q=[pq[i][:, PAIR:] for i in n],
               r=r, v=v, bkt=bkt, wcol=wcol, mrb=mrb, mv=[lmv[i][CHUNK:] for i in n])
    yield


def _wkv_chain(pre, z, y_ref, c, inst, m):
    rows = slice(c * CHUNK, (c + 1) * CHUNK)
    n = range(len(inst))
    pr = [_bdot(_rows(pre["p"][i], pre["r"][i]), z[i]) for i in n]
    yield
    u = [(pr[i][:CHUNK] + pre["q"][i]).astype(BF16) for i in n]
    for i in n:
        z[i] = pre["wcol"][i] * z[i] + jnp.where(
            m["same_head"], _bdot(pre["bkt"][i], _rows(u[i], pre["v"][i])), 0.0)
    yield
    for i, (b, p) in enumerate(inst):
        y_ref[b, rows, p * PAIR:(p + 1) * PAIR] = (
            pr[i][CHUNK:] + _bdot(pre["mrb"][i], _pair_expand(u[i])) + pre["mv"][i])
    yield


def _interleave(primary, secondary, ratio):
    for _ in primary:
        for _ in itertools.islice(secondary, ratio):
            pass
    for _ in secondary:
        pass


def _wkv_kernel(rt_ref, at_ref, bt_ref, kt_ref, bh_ref, kh_ref, v_ref, wc_ref, y_ref, state):
    @pl.when(pl.program_id(0) == 0)
    def _():
        state[...] = jnp.zeros_like(state)

    nb, t, _ = rt_ref.shape
    nc = t // CHUNK
    inst = [(b, p) for b in range(nb) for p in range(N_PAIRS)]
    m = _wkv_masks()
    refs = (rt_ref, at_ref, bt_ref, kt_ref, bh_ref, kh_ref, v_ref)
    z = [state[b, p] for b, p in inst]
    pre = [dict() for _ in range(nc)]
    for _ in _wkv_prep(refs, wc_ref, 0, inst, m, pre[0]):
        pass
    for c in range(nc):
        nxt = _wkv_prep(refs, wc_ref, c + 1, inst, m, pre[c + 1]) if c + 1 < nc else iter(())
        _interleave(_wkv_chain(pre[c], z, y_ref, c, inst, m), nxt, PREP_STAGES_PER_CHAIN_STAGE)
    for i, (b, p) in enumerate(inst):
        state[b, p] = z[i]


def _wkv(rt, at, bt, kt, bh, kh, v, wc, tile):
    b, s, d = rt.shape
    tok = pl.BlockSpec((b, tile, d), lambda j: (0, j, 0))
    return pl.pallas_call(
        _wkv_kernel,
        grid=(s // tile,),
        in_specs=[tok] * 7 + [pl.BlockSpec((b, tile // CHUNK, 1, d), lambda j: (0, j, 0, 0))],
        out_specs=tok,
        out_shape=jax.ShapeDtypeStruct((b, s, d), F32),
        scratch_shapes=[pltpu.VMEM((b, N_PAIRS, PAIR, PAIR), F32)],
        compiler_params=pltpu.CompilerParams(dimension_semantics=("arbitrary",),
                                             vmem_limit_bytes=VMEM_LIMIT),
        name="wkv",
    )(rt, at, bt, kt, bh, kh, v, wc)


def _mix_kernel(x_ref, y_ref, g_ref, bonus_ref, ob_ref, k_ref, v_ref, vec_ref, nm_ref, bg_ref,
                wgate_ref, wpa_ref, wpb_ref, wom_ref, nx_ref, wq_ref, wxo_ref, out_ref):
    x = x_ref[0]
    h = _rms(x, nm_ref[...]).astype(BF16)
    sg = _sigmoid(jnp.dot(h, wgate_ref[...], preferred_element_type=F32) + bg_ref[...])

    y = y_ref[0]
    yc = y - _head_sums(y) * (1.0 / RWKV_HEAD)
    var = _head_sums(yc * yc) * (1.0 / RWKV_HEAD)
    yn = yc * lax.rsqrt(var + LN_X_EPS)
    oa = (yn * vec_ref[ROW_LNW:ROW_LNW + 1, :] + bonus_ref[0]) * g_ref[0]

    merged = (sg[:, :D_MODEL] * _bdot(oa, wpa_ref[...])
              + sg[:, D_MODEL:] * jnp.dot(ob_ref[0], wpb_ref[...], preferred_element_type=F32))
    x1 = x + _bdot(merged, wom_ref[...])

    q = _bdot(_rms(x1, nx_ref[...]), wq_ref[...]) * (XHEAD_DIM ** -0.5)
    heads = []
    for hh in range(N_XHEADS):
        cols = slice(hh * XHEAD_DIM, (hh + 1) * XHEAD_DIM)
        sc = _bdot_nt(q[:, cols], k_ref[0, :, cols])
        e = jnp.exp(sc - jnp.max(sc, axis=-1, keepdims=True))
        pr = e / jnp.sum(e, axis=-1, keepdims=True)
        heads.append(_bdot(pr, v_ref[0, :, cols]))
    o = jnp.concatenate(heads, axis=-1)
    out_ref[0] = x1 + _bdot(o, wxo_ref[...])


def _mix(x, y, g, bonus, ob, kmem, vmem, vec, nm, bg, wgate, wpa, wpb, wom, nx, wq, wxo, tile):
    b, s, d = x.shape
    n_mem = kmem.shape[1]
    const = _const_spec
    tok = lambda w: pl.BlockSpec((1, tile, w), lambda i, j: (i, j, 0))
    memspec = pl.BlockSpec((1, n_mem, d), lambda i, j: (i, 0, 0))
    return pl.pallas_call(
        _mix_kernel,
        grid=(b, s // tile),
        in_specs=[tok(d), tok(D_RWKV), tok(D_RWKV), tok(D_RWKV), tok(D_CONV), memspec, memspec,
                  const(VEC_ROWS, D_RWKV), const(1, d), const(1, 2 * d), const(d, 2 * d),
                  const(D_RWKV, d), const(D_CONV, d), const(d, d), const(1, d), const(d, d),
                  const(d, d)],
        out_specs=tok(d),
        out_shape=jax.ShapeDtypeStruct((b, s, d), F32),
        compiler_params=pltpu.CompilerParams(dimension_semantics=("arbitrary", "arbitrary"),
                                             vmem_limit_bytes=VMEM_LIMIT),
        name="mix",
    )(x, y, g, bonus, ob, kmem, vmem, vec, nm, bg, wgate, wpa, wpb, wom, nx, wq, wxo)


def _mlp_kernel(x_ref, nm_ref, wup_ref, wdown_ref, nf_ref, out_ref):
    x = x_ref[...]
    h = _rms(x, nm_ref[...]).astype(BF16)
    acc = x
    for c in range(D_FF // D_MODEL):
        cols = slice(c * D_MODEL, (c + 1) * D_MODEL)
        up = jnp.maximum(jnp.dot(h, wup_ref[:, cols], preferred_element_type=F32), 0.0)
        acc = acc + _bdot(up * up, wdown_ref[cols, :])
    out_ref[...] = _rms(acc, nf_ref[...])


def _mlp(x, nm, wup, wdown, nf, tile):
    n, d = x.shape
    const = _const_spec
    tok = pl.BlockSpec((tile, d), lambda i: (i, 0))
    return pl.pallas_call(
        _mlp_kernel,
        grid=(n // tile,),
        in_specs=[tok, const(1, d), const(d, D_FF), const(D_FF, d), const(1, d)],
        out_specs=tok,
        out_shape=jax.ShapeDtypeStruct((n, d), F32),
        compiler_params=pltpu.CompilerParams(dimension_semantics=("arbitrary",),
                                             vmem_limit_bytes=VMEM_LIMIT),
        name="mlp",
    )(x, nm, wup, wdown, nf)


def _pad_cols(w, width):
    return jnp.pad(w, ((0, 0), (0, width - w.shape[1])))


def _pad_rows(w, height):
    return jnp.pad(w, ((0, height - w.shape[0]), (0, 0)))


def _layer(x, kmem, vmem, norm_mix, w_in, b_gate, mu_shift, w0, w_lora_w, a0, w_lora_a, w_lora_g,
           k_k, k_a, r_k, ln_x_w, ln_x_b, conv_w, w_proj_a, w_proj_b, w_out_mix, norm_xattn,
           w_q, w_xo, tri):
    c_wd = 3 * D_RWKV
    c_ad = c_wd + DECAY_LORA
    c_gd = c_ad + AAA_LORA
    c_conv = c_gd + GATE_LORA
    c_gate = c_conv + 3 * D_CONV
    wrw = jnp.concatenate([w_in[:, :c_wd], _pad_cols(w_in[:, c_wd:c_ad], LANES),
                           _pad_cols(w_in[:, c_ad:c_gd], LANES),
                           _pad_cols(w_in[:, c_gd:c_conv], 2 * LANES)], axis=1).astype(BF16)
    mu = jnp.concatenate([mu_shift[None, :c_wd], _pad_cols(mu_shift[None, c_wd:c_ad], LANES),
                          _pad_cols(mu_shift[None, c_ad:c_gd], LANES),
                          _pad_cols(mu_shift[None, c_gd:], 2 * LANES)], axis=1)
    wconv = w_in[:, c_conv:c_gate].astype(BF16)
    wgate = w_in[:, c_gate:].astype(BF16)
    lora = [_pad_rows(w, rows).astype(BF16)
            for w, rows in ((w_lora_w, LANES), (w_lora_a, LANES), (w_lora_g, 2 * LANES))]
    vec = jnp.stack([w0, a0, k_k, k_a, r_k.reshape(-1), ln_x_b, ln_x_w,
                     conv_w[0, 0], conv_w[1, 0], conv_w[2, 0]])
    vec = _pad_rows(vec, VEC_ROWS)

    rt, at, bt, kt, bh, kh, v, wc, g, bonus, ob = _inproj(
        x, norm_mix[None], wrw, wconv, mu, vec, lora, tri, tile=TILE_INPROJ)
    y = _wkv(rt, at, bt, kt, bh, kh, v, wc, tile=TILE_WKV)
    return _mix(x, y, g, bonus, ob, kmem, vmem, vec, norm_mix[None], b_gate[None], wgate,
                w_proj_a.astype(BF16), w_proj_b.astype(BF16), w_out_mix.astype(BF16),
                norm_xattn[None], w_q.astype(BF16), w_xo.astype(BF16), tile=TILE_MIX)


def kernel(x, mem, norm_mix, w_in, b_gate, mu_shift, w0, w_lora_w, a0, w_lora_a, w_lora_g, k_k, k_a, r_k, ln_x_w, ln_x_b, conv_w, w_proj_a, w_proj_b, w_out_mix, norm_xattn, norm_mem, w_q, w_kv, w_xo, norm_mlp, w_up, w_down, norm_final):
    assert w_in.shape[0] == 1, "the MLP kernel fuses the final norm: single-layer trunk only"
    bsz, s, d = x.shape
    step = jnp.arange(CHUNK)
    tri = (step[None, :] <= step[:, None]).astype(BF16)
    kmem, vmem = _memkv(mem, norm_mem[0][None], w_kv[0].astype(BF16))
    x = _layer(x, kmem, vmem, norm_mix[0], w_in[0], b_gate[0], mu_shift[0], w0[0], w_lora_w[0],
               a0[0], w_lora_a[0], w_lora_g[0], k_k[0], k_a[0], r_k[0], ln_x_w[0], ln_x_b[0],
               conv_w[0], w_proj_a[0], w_proj_b[0], w_out_mix[0], norm_xattn[0], w_q[0],
               w_xo[0], tri)
    return _mlp(x.reshape(bsz * s, d), norm_mlp[0][None], w_up[0].astype(BF16),
                w_down[0].astype(BF16), norm_final[None], tile=TILE_MLP).reshape(bsz, s, d)
```

```python
import functools
import itertools
import math

import jax
import jax.numpy as jnp
from jax import lax
from jax.experimental import pallas as pl
from jax.experimental.pallas import tpu as pltpu

F32 = jnp.float32
BF16 = jnp.bfloat16

D_MODEL = 1024
D_RWKV = 512
RWKV_HEAD = 64
DECAY_LORA = 64
AAA_LORA = 64
GATE_LORA = 160
LN_X_EPS = 64e-5
D_CONV = 512
N_XHEADS = 4
XHEAD_DIM = D_MODEL // N_XHEADS
D_FF = 4 * D_MODEL
RMS_EPS = 1e-6
EXP_MINUS_HALF = math.exp(-0.5)

LANES = 128
CHUNK = 64
PAIR = 2 * RWKV_HEAD
N_PAIRS = D_RWKV // PAIR
WA_OFF = 3 * D_RWKV
GD_OFF = WA_OFF + DECAY_LORA + AAA_LORA
RW_COLS = GD_OFF + 2 * LANES
assert DECAY_LORA + AAA_LORA == LANES and GATE_LORA <= 2 * LANES
VMEM_LIMIT = 56 * 1024 * 1024

ROW_W0, ROW_A0, ROW_KK, ROW_KA, ROW_RK, ROW_LNB, ROW_LNW, ROW_CONV = 0, 1, 2, 3, 4, 5, 6, 7
VEC_ROWS = 16

TILE_INPROJ = 512
TILE_WKV = 8 * CHUNK
PREP_STAGES_PER_CHAIN_STAGE = 3
TILE_MIX = 1024
TILE_MLP = 1024


def _bdot(a, b):
    return jnp.dot(a.astype(BF16), b.astype(BF16), preferred_element_type=F32)


def _bdot_nt(a, b):
    return lax.dot_general(a.astype(BF16), b.astype(BF16), (((1,), (1,)), ((), ())),
                           preferred_element_type=F32)


def _split2(a):
    hi = a.astype(BF16)
    lo = (a - hi.astype(F32)).astype(BF16)
    return hi, lo


def _head_sums(a):
    low = lax.broadcasted_iota(jnp.int32, (a.shape[0], LANES), 1) < RWKV_HEAD
    out = []
    for j in range(a.shape[1] // LANES):
        slab = a[:, j * LANES:(j + 1) * LANES]
        s_lo = jnp.sum(jnp.where(low, slab, 0.0), axis=-1, keepdims=True)
        s_hi = jnp.sum(jnp.where(low, 0.0, slab), axis=-1, keepdims=True)
        out.append(jnp.where(low, s_lo, s_hi))
    return jnp.concatenate(out, axis=-1)


def _dot_sel_lhs(sel, a):
    hi, lo = _split2(a)
    d = functools.partial(jnp.dot, preferred_element_type=F32)
    return d(sel, hi) + d(sel, lo)


def _rms(x, g):
    return x * lax.rsqrt(jnp.mean(x * x, axis=-1, keepdims=True) + RMS_EPS) * g


def _sigmoid(x):
    return 1.0 / (1.0 + jnp.exp(-x))


def _shift_rows(x, carry_rows, n):
    nc = carry_rows.shape[0]
    out = pltpu.roll(x, n, 0)
    rows = lax.broadcasted_iota(jnp.int32, (nc, 1), 0)
    head = out[:nc]
    for j in range(n):
        head = jnp.where(rows == j, carry_rows[nc - n + j:nc - n + j + 1, :], head)
    return jnp.concatenate([head, out[nc:]], axis=0)


def _const_spec(*shape):
    return pl.BlockSpec(shape, lambda *_: (0,) * len(shape), pipeline_mode=pl.Buffered(1))


def _memkv_kernel(mem_ref, g_ref, w_ref, k_ref, v_ref):
    m = _rms(mem_ref[0], g_ref[...])
    kv = _bdot(m, w_ref[...])
    k_ref[0] = kv[:, :D_MODEL].astype(BF16)
    v_ref[0] = kv[:, D_MODEL:].astype(BF16)


def _memkv(mem, g, w_kv):
    b, n, d = mem.shape
    return pl.pallas_call(
        _memkv_kernel,
        grid=(b,),
        in_specs=[pl.BlockSpec((1, n, d), lambda i: (i, 0, 0)),
                  pl.BlockSpec((1, d), lambda i: (0, 0)),
                  pl.BlockSpec((d, 2 * d), lambda i: (0, 0))],
        out_specs=[pl.BlockSpec((1, n, d), lambda i: (i, 0, 0))] * 2,
        out_shape=[jax.ShapeDtypeStruct((b, n, d), BF16)] * 2,
        compiler_params=pltpu.CompilerParams(dimension_semantics=("arbitrary",),
                                             vmem_limit_bytes=VMEM_LIMIT),
        name="memkv",
    )(mem, g, w_kv)


def _inproj_kernel(x_ref, nm_ref, wrw_ref, wconv_ref, mu_ref, vec_ref,
                   ww_ref, wa_ref, wg_ref, tri_ref,
                   rt_ref, at_ref, bt_ref, kt_ref, bh_ref, kh_ref, v_ref, wc_ref,
                   g_ref, bonus_ref, ob_ref,
                   pcarry, ucarry):
    @pl.when(pl.program_id(1) == 0)
    def _():
        pcarry[...] = jnp.zeros_like(pcarry)
        ucarry[...] = jnp.zeros_like(ucarry)

    t = x_ref.shape[1]
    h = _rms(x_ref[0], nm_ref[...]).astype(BF16)

    pc = jnp.dot(h, wconv_ref[...], preferred_element_type=F32)
    u = pc[:, D_CONV:2 * D_CONV] * pc[:, 2 * D_CONV:]
    uc = ucarry[...]
    conv = (vec_ref[ROW_CONV:ROW_CONV + 1, :] * _shift_rows(u, uc, 2)
            + vec_ref[ROW_CONV + 1:ROW_CONV + 2, :] * _shift_rows(u, uc, 1)
            + vec_ref[ROW_CONV + 2:ROW_CONV + 3, :] * u)
    ob_ref[0] = (pc[:, :D_CONV] * conv).astype(BF16)
    ucarry[...] = u[t - 8:, :]

    p = jnp.dot(h, wrw_ref[...], preferred_element_type=F32)
    ps = _shift_rows(p, pcarry[...], 1)
    pcarry[...] = p[t - 8:, :]
    xm = p + (ps - p) * mu_ref[...]
    r = xm[:, :D_RWKV]
    k = xm[:, D_RWKV:2 * D_RWKV]
    v = xm[:, 2 * D_RWKV:3 * D_RWKV]
    wa = xm[:, WA_OFF:GD_OFF]
    gd = xm[:, GD_OFF:]

    def vec(row):
        return vec_ref[row:row + 1, :]

    ld = -EXP_MINUS_HALF * _sigmoid(vec(ROW_W0) + _bdot(jnp.tanh(wa), ww_ref[...]))
    a = _sigmoid(vec(ROW_A0) + _bdot(wa, wa_ref[...]))
    g_ref[0] = _bdot(_sigmoid(gd), wg_ref[...])
    kk = k * vec(ROW_KK)
    kk = kk * lax.rsqrt(jnp.maximum(_head_sums(kk * kk), 1e-24))
    k2 = k * (1.0 + (a - 1.0) * vec(ROW_KA))
    bonus_ref[0] = _head_sums(r * k2 * vec(ROW_RK)) * v + vec(ROW_LNB)
    v_ref[0] = v.astype(BF16)
    na = -kk
    nb = kk * a

    tri = tri_ref[...]
    for c in range(t // CHUNK):
        sl = slice(c * CHUNK, (c + 1) * CHUNK)
        ldc = ld[sl]
        cum = _dot_sel_lhs(tri, ldc)
        tot = cum[CHUNK - 1:CHUNK, :]
        e_inc = jnp.exp(cum)
        e_inv = 1.0 / e_inc
        e_prev = jnp.exp(cum - ldc)
        e_tot = jnp.exp(tot)
        e_end = e_tot * e_inv
        rt_ref[0, sl, :] = (r[sl] * e_inc).astype(BF16)
        at_ref[0, sl, :] = (na[sl] * e_prev).astype(BF16)
        bt_ref[0, sl, :] = (nb[sl] * e_inv).astype(BF16)
        kt_ref[0, sl, :] = (k2[sl] * e_inv).astype(BF16)
        bh_ref[0, sl, :] = (nb[sl] * e_end).astype(BF16)
        kh_ref[0, sl, :] = (k2[sl] * e_end).astype(BF16)
        wc_ref[0, c] = e_tot


def _inproj(x, nm, wrw, wconv, mu, vec, lora, tri, tile):
    b, s, d = x.shape
    nt = s // tile
    const = _const_spec
    tok = lambda w: pl.BlockSpec((1, tile, w), lambda i, j: (i, j, 0))
    outs = ([jax.ShapeDtypeStruct((b, s, D_RWKV), BF16)] * 7
            + [jax.ShapeDtypeStruct((b, s // CHUNK, 1, D_RWKV), F32)]
            + [jax.ShapeDtypeStruct((b, s, D_RWKV), F32)] * 2
            + [jax.ShapeDtypeStruct((b, s, D_CONV), BF16)])
    out_specs = ([tok(D_RWKV)] * 7
                 + [pl.BlockSpec((1, tile // CHUNK, 1, D_RWKV), lambda i, j: (i, j, 0, 0))]
                 + [tok(D_RWKV)] * 2 + [tok(D_CONV)])
    return pl.pallas_call(
        _inproj_kernel,
        grid=(b, nt),
        in_specs=[tok(d), const(1, d), const(d, RW_COLS), const(d, 3 * D_CONV),
                  const(1, RW_COLS), const(VEC_ROWS, D_RWKV)]
                 + [const(*w.shape) for w in lora]
                 + [const(CHUNK, CHUNK)],
        out_specs=out_specs,
        out_shape=outs,
        scratch_shapes=[pltpu.VMEM((8, RW_COLS), F32), pltpu.VMEM((8, D_CONV), F32)],
        compiler_params=pltpu.CompilerParams(dimension_semantics=("arbitrary", "arbitrary"),
                                             vmem_limit_bytes=VMEM_LIMIT),
        name="inproj",
    )(x, nm, wrw, wconv, mu, vec, *lora, tri)


def _pair_expand(x):
    even = lax.broadcasted_iota(jnp.int32, x.shape, 1) < RWKV_HEAD
    zero = jnp.zeros_like(x)
    return jnp.concatenate([jnp.where(even, x, zero), jnp.where(even, zero, x)], axis=0)


def _rows(*parts):
    return jnp.concatenate(parts, axis=0)


def _wkv_masks():
    row = lax.broadcasted_iota(jnp.int32, (CHUNK, PAIR), 0)
    col = lax.broadcasted_iota(jnp.int32, (CHUNK, PAIR), 1) % RWKV_HEAD
    row2 = lax.broadcasted_iota(jnp.int32, (PAIR, PAIR), 0) < RWKV_HEAD
    col2 = lax.broadcasted_iota(jnp.int32, (PAIR, PAIR), 1) < RWKV_HEAD
    return dict(strict=col < row, incl=col <= row, eye=(col == row).astype(F32),
                same_head=row2 == col2)


def _wkv_prep(refs, wc_ref, c, inst, m, out):
    rows = slice(c * CHUNK, (c + 1) * CHUNK)
    load = lambda ref: [ref[b, rows, p * PAIR:(p + 1) * PAIR] for b, p in inst]
    r, a, bt, kt, bh, kh, v = (load(ref) for ref in refs)
    n = range(len(inst))
    ex = lambda x: _pair_expand(x.astype(BF16))
    g = [_bdot_nt(_rows(a[i], r[i]), _rows(_pair_expand(bt[i]), _pair_expand(kt[i]))) for i in n]
    yield
    lab = [jnp.where(m["strict"], g[i][:CHUNK, :PAIR], 0.0) for i in n]
    lak = [jnp.where(m["strict"], g[i][:CHUNK, PAIR:], 0.0) for i in n]
    mrb = [jnp.where(m["incl"], g[i][CHUNK:, :PAIR], 0.0) for i in n]
    mrk = [jnp.where(m["incl"], g[i][CHUNK:, PAIR:], 0.0) for i in n]
    npow = [_bdot(lab[i], ex(lab[i])) for i in n]
    tinv = [m["eye"] + lab[i] for i in n]
    lmv = [_bdot(_rows(lak[i], mrk[i]), _pair_expand(v[i])) for i in n]
    yield
    for _ in range(4):
        prod = [_bdot(_rows(tinv[i], npow[i]), ex(npow[i])) for i in n]
        tinv = [tinv[i] + prod[i][:CHUNK] for i in n]
        npow = [prod[i][CHUNK:] for i in n]
        yield
    tinv = [tinv[i] + _bdot(tinv[i], ex(npow[i])) for i in n]
    yield
    pq = [_bdot(tinv[i], jnp.concatenate([_pair_expand(a[i]), ex(lmv[i][:CHUNK])], axis=1))
          for i in n]
    bkt = [_rows(bh[i], kh[i]).astype(F32).T.astype(BF16) for i in n]
    wcol = [jnp.broadcast_to(wc_ref[b, c, :, p * PAIR:(p + 1) * PAIR], (PAIR, PAIR)).T
            for b, p in inst]
    out.update(p=[pq[i][:, :PAIR].astype(BF16) for i in n], q=[pq[i][:, PAIR:] for i in n],
               r=r, v=v, bkt=bkt, wcol=wcol, mrb=mrb, mv=[lmv[i][CHUNK:] for i in n])
    yield


def _wkv_chain(pre, z, y_ref, c, inst, m):
    rows = slice(c * CHUNK, (c + 1) * CHUNK)
    n = range(len(inst))
    pr = [_bdot(_rows(pre["p"][i], pre["r"][i]), z[i]) for i in n]
    yield
    u = [(pr[i][:CHUNK] + pre["q"][i]).astype(BF16) for i in n]
    for i in n:
        z[i] = pre["wcol"][i] * z[i] + jnp.where(
            m["same_head"], _bdot(pre["bkt"][i], _rows(u[i], pre["v"][i])), 0.0)
    yield
    for i, (b, p) in enumerate(inst):
        y_ref[b, rows, p * PAIR:(p + 1) * PAIR] = (
            pr[i][CHUNK:] + _bdot(pre["mrb"][i], _pair_expand(u[i])) + pre["mv"][i])
    yield


def _interleave(primary, secondary, ratio):
    for _ in primary:
        for _ in itertools.islice(secondary, ratio):
            pass
    for _ in secondary:
        pass


def _cast_stages(src_refs, dst_refs):
    for src, dst in zip(src_refs, dst_refs):
        dst[...] = src[...].astype(dst.dtype)
        yield


def _wkv_kernel(rt_ref, at_ref, bt_ref, kt_ref, bh_ref, kh_ref, v_ref, wc_ref, *rest):
    nw = (len(rest) - 2) // 2
    w_refs, y_ref, wb_refs, state = rest[:nw], rest[nw], rest[nw + 1:-1], rest[-1]

    @pl.when(pl.program_id(0) == 0)
    def _():
        state[...] = jnp.zeros_like(state)

    casts = _cast_stages(w_refs, wb_refs)
    nb, t, _ = rt_ref.shape
    nc = t // CHUNK
    inst = [(b, p) for b in range(nb) for p in range(N_PAIRS)]
    m = _wkv_masks()
    refs = (rt_ref, at_ref, bt_ref, kt_ref, bh_ref, kh_ref, v_ref)
    z = [state[b, p] for b, p in inst]
    pre = [dict() for _ in range(nc)]
    for _ in _wkv_prep(refs, wc_ref, 0, inst, m, pre[0]):
        pass
    for c in range(nc):
        nxt = _wkv_prep(refs, wc_ref, c + 1, inst, m, pre[c + 1]) if c + 1 < nc else iter(())
        _interleave(_wkv_chain(pre[c], z, y_ref, c, inst, m), nxt, PREP_STAGES_PER_CHAIN_STAGE)
        next(casts, None)
    for _ in casts:
        pass
    for i, (b, p) in enumerate(inst):
        state[b, p] = z[i]


def _wkv(rt, at, bt, kt, bh, kh, v, wc, weights, tile):
    b, s, d = rt.shape
    steps = s // tile
    tok = pl.BlockSpec((b, tile, d), lambda j: (0, j, 0))
    wspecs = [pl.BlockSpec((w.shape[0] // steps, w.shape[1]), lambda j: (j, 0)) for w in weights]
    out = pl.pallas_call(
        _wkv_kernel,
        grid=(steps,),
        in_specs=[tok] * 7 + [pl.BlockSpec((b, tile // CHUNK, 1, d), lambda j: (0, j, 0, 0))] + wspecs,
        out_specs=[tok] + wspecs,
        out_shape=[jax.ShapeDtypeStruct((b, s, d), F32)]
                  + [jax.ShapeDtypeStruct(w.shape, BF16) for w in weights],
        scratch_shapes=[pltpu.VMEM((b, N_PAIRS, PAIR, PAIR), F32)],
        compiler_params=pltpu.CompilerParams(dimension_semantics=("arbitrary",),
                                             vmem_limit_bytes=VMEM_LIMIT),
        name="wkv",
    )(rt, at, bt, kt, bh, kh, v, wc, *weights)
    return out[0], out[1:]


def _mix_kernel(x_ref, y_ref, g_ref, bonus_ref, ob_ref, k_ref, v_ref, vec_ref, nm_ref, bg_ref,
                wgate_ref, wpa_ref, wpb_ref, wom_ref, nx_ref, wq_ref, wxo_ref, out_ref):
    x = x_ref[0]
    h = _rms(x, nm_ref[...]).astype(BF16)
    sg = _sigmoid(jnp.dot(h, wgate_ref[...], preferred_element_type=F32) + bg_ref[...])

    y = y_ref[0]
    yc = y - _head_sums(y) * (1.0 / RWKV_HEAD)
    var = _head_sums(yc * yc) * (1.0 / RWKV_HEAD)
    yn = yc * lax.rsqrt(var + LN_X_EPS)
    oa = (yn * vec_ref[ROW_LNW:ROW_LNW + 1, :] + bonus_ref[0]) * g_ref[0]

    merged = (sg[:, :D_MODEL] * _bdot(oa, wpa_ref[...])
              + sg[:, D_MODEL:] * jnp.dot(ob_ref[0], wpb_ref[...], preferred_element_type=F32))
    x1 = x + _bdot(merged, wom_ref[...])

    q = _bdot(_rms(x1, nx_ref[...]), wq_ref[...]) * (XHEAD_DIM ** -0.5)
    heads = []
    for hh in range(N_XHEADS):
        cols = slice(hh * XHEAD_DIM, (hh + 1) * XHEAD_DIM)
        sc = _bdot_nt(q[:, cols], k_ref[0, :, cols])
        e = jnp.exp(sc - jnp.max(sc, axis=-1, keepdims=True))
        pr = e / jnp.sum(e, axis=-1, keepdims=True)
        heads.append(_bdot(pr, v_ref[0, :, cols]))
    o = jnp.concatenate(heads, axis=-1)
    out_ref[0] = x1 + _bdot(o, wxo_ref[...])


def _mix(x, y, g, bonus, ob, kmem, vmem, vec, nm, bg, wgate, wpa, wpb, wom, nx, wq, wxo, tile):
    b, s, d = x.shape
    n_mem = kmem.shape[1]
    const = _const_spec
    tok = lambda w: pl.BlockSpec((1, tile, w), lambda i, j: (i, j, 0))
    memspec = pl.BlockSpec((1, n_mem, d), lambda i, j: (i, 0, 0))
    return pl.pallas_call(
        _mix_kernel,
        grid=(b, s // tile),
        in_specs=[tok(d), tok(D_RWKV), tok(D_RWKV), tok(D_RWKV), tok(D_CONV), memspec, memspec,
                  const(VEC_ROWS, D_RWKV), const(1, d), const(1, 2 * d), const(d, 2 * d),
                  const(D_RWKV, d), const(D_CONV, d), const(d, d), const(1, d), const(d, d),
                  const(d, d)],
        out_specs=tok(d),
        out_shape=jax.ShapeDtypeStruct((b, s, d), F32),
        compiler_params=pltpu.CompilerParams(dimension_semantics=("arbitrary", "arbitrary"),
                                             vmem_limit_bytes=VMEM_LIMIT),
        name="mix",
    )(x, y, g, bonus, ob, kmem, vmem, vec, nm, bg, wgate, wpa, wpb, wom, nx, wq, wxo)


def _mlp_kernel(x_ref, nm_ref, wup_ref, wdown_ref, nf_ref, out_ref):
    x = x_ref[...]
    h = _rms(x, nm_ref[...]).astype(BF16)
    acc = x
    for c in range(D_FF // D_MODEL):
        cols = slice(c * D_MODEL, (c + 1) * D_MODEL)
        up = jnp.maximum(jnp.dot(h, wup_ref[:, cols], preferred_element_type=F32), 0.0)
        acc = acc + _bdot(up * up, wdown_ref[cols, :])
    out_ref[...] = _rms(acc, nf_ref[...])


def _mlp(x, nm, wup, wdown, nf, tile):
    n, d = x.shape
    const = _const_spec
    tok = pl.BlockSpec((tile, d), lambda i: (i, 0))
    return pl.pallas_call(
        _mlp_kernel,
        grid=(n // tile,),
        in_specs=[tok, const(1, d), const(d, D_FF), const(D_FF, d), const(1, d)],
        out_specs=tok,
        out_shape=jax.ShapeDtypeStruct((n, d), F32),
        compiler_params=pltpu.CompilerParams(dimension_semantics=("arbitrary",),
                                             vmem_limit_bytes=VMEM_LIMIT),
        name="mlp",
    )(x, nm, wup, wdown, nf)


def _pad_cols(w, width):
    return jnp.pad(w, ((0, 0), (0, width - w.shape[1])))


def _pad_rows(w, height):
    return jnp.pad(w, ((0, height - w.shape[0]), (0, 0)))


def _layer(x, mem, norm_mix, w_in, b_gate, mu_shift, w0, w_lora_w, a0, w_lora_a, w_lora_g,
           k_k, k_a, r_k, ln_x_w, ln_x_b, conv_w, w_proj_a, w_proj_b, w_out_mix, norm_xattn,
           norm_mem, w_q, w_kv, w_xo, norm_mlp, w_up, w_down, norm_final):
    bsz, s, d = x.shape
    c_conv = RW_COLS - 2 * LANES + GATE_LORA
    c_gate = c_conv + 3 * D_CONV
    w_in = w_in.astype(BF16)
    wconv = w_in[:, c_conv:c_gate]
    wgate = w_in[:, c_gate:]
    mu = _pad_cols(mu_shift[None], RW_COLS)
    lora = [jnp.pad(w, ((top, rows - top - w.shape[0]), (0, 0))).astype(BF16)
            for w, top, rows in ((w_lora_w, 0, LANES), (w_lora_a, DECAY_LORA, LANES),
                                 (w_lora_g, 0, 2 * LANES))]
    vec = jnp.stack([w0, a0, k_k, k_a, r_k.reshape(-1), ln_x_b, ln_x_w,
                     conv_w[0, 0], conv_w[1, 0], conv_w[2, 0]])
    vec = _pad_rows(vec, VEC_ROWS)
    step = jnp.arange(CHUNK)
    tri = (step[None, :] <= step[:, None]).astype(BF16)

    kmem, vmem = _memkv(mem, norm_mem[None], w_kv.astype(BF16))
    rt, at, bt, kt, bh, kh, v, wc, g, bonus, ob = _inproj(
        x, norm_mix[None], w_in, wconv, mu, vec, lora, tri, tile=TILE_INPROJ)
    y, (wpa, wpb, wom, wq, wxo, wup, wdown) = _wkv(
        rt, at, bt, kt, bh, kh, v, wc, (w_proj_a, w_proj_b, w_out_mix, w_q, w_xo, w_up, w_down),
        tile=TILE_WKV)
    x = _mix(x, y, g, bonus, ob, kmem, vmem, vec, norm_mix[None], b_gate[None], wgate,
             wpa, wpb, wom, norm_xattn[None], wq, wxo, tile=TILE_MIX)
    return _mlp(x.reshape(bsz * s, d), norm_mlp[None], wup, wdown, norm_final[None],
                tile=TILE_MLP).reshape(bsz, s, d)


def kernel(x, mem, norm_mix, w_in, b_gate, mu_shift, w0, w_lora_w, a0, w_lora_a, w_lora_g, k_k, k_a, r_k, ln_x_w, ln_x_b, conv_w, w_proj_a, w_proj_b, w_out_mix, norm_xattn, norm_mem, w_q, w_kv, w_xo, norm_mlp, w_up, w_down, norm_final):
    assert w_in.shape[0] == 1, "the MLP kernel fuses the final norm: single-layer trunk only"
    per_layer = (norm_mix, w_in, b_gate, mu_shift, w0, w_lora_w, a0, w_lora_a, w_lora_g, k_k, k_a,
                 r_k, ln_x_w, ln_x_b, conv_w, w_proj_a, w_proj_b, w_out_mix, norm_xattn, norm_mem,
                 w_q, w_kv, w_xo, norm_mlp, w_up, w_down)
    return _layer(x, mem, *(p[0] for p in per_layer), norm_final)
```

```python
import functools
import itertools
import math

import jax
import jax.numpy as jnp
from jax import lax
from jax.experimental import pallas as pl
from jax.experimental.pallas import tpu as pltpu

F32 = jnp.float32
BF16 = jnp.bfloat16

D_MODEL = 1024
D_RWKV = 512
RWKV_HEAD = 64
DECAY_LORA = 64
AAA_LORA = 64
GATE_LORA = 160
LN_X_EPS = 64e-5
D_CONV = 512
N_XHEADS = 4
XHEAD_DIM = D_MODEL // N_XHEADS
D_FF = 4 * D_MODEL
RMS_EPS = 1e-6
EXP_MINUS_HALF = math.exp(-0.5)

LANES = 128
CHUNK = 64
PAIR = 2 * RWKV_HEAD
N_PAIRS = D_RWKV // PAIR
WA_OFF = 3 * D_RWKV
GD_OFF = WA_OFF + DECAY_LORA + AAA_LORA
RW_COLS = GD_OFF + 2 * LANES
assert DECAY_LORA + AAA_LORA == LANES and GATE_LORA <= 2 * LANES
VMEM_LIMIT = 56 * 1024 * 1024

ROW_W0, ROW_A0, ROW_KK, ROW_KA, ROW_RK, ROW_LNB, ROW_LNW, ROW_CONV = 0, 1, 2, 3, 4, 5, 6, 7
VEC_ROWS = 16

TILE_INPROJ = 512
TILE_WKV = 8 * CHUNK
PREP_STAGES_PER_CHAIN_STAGE = 3
TILE_MIX = 1024
TILE_MLP = 1024


def _bdot(a, b):
    return jnp.dot(a.astype(BF16), b.astype(BF16), preferred_element_type=F32)


def _bdot_nt(a, b):
    return lax.dot_general(a.astype(BF16), b.astype(BF16), (((1,), (1,)), ((), ())),
                           preferred_element_type=F32)


def _split2(a):
    hi = a.astype(BF16)
    lo = (a - hi.astype(F32)).astype(BF16)
    return hi, lo


def _head_sums(a):
    low = lax.broadcasted_iota(jnp.int32, (a.shape[0], LANES), 1) < RWKV_HEAD
    out = []
    for j in range(a.shape[1] // LANES):
        slab = a[:, j * LANES:(j + 1) * LANES]
        s_lo = jnp.sum(jnp.where(low, slab, 0.0), axis=-1, keepdims=True)
        s_hi = jnp.sum(jnp.where(low, 0.0, slab), axis=-1, keepdims=True)
        out.append(jnp.where(low, s_lo, s_hi))
    return jnp.concatenate(out, axis=-1)


def _dot_sel_lhs(sel, a):
    hi, lo = _split2(a)
    d = functools.partial(jnp.dot, preferred_element_type=F32)
    return d(sel, hi) + d(sel, lo)


def _rms(x, g):
    return x * lax.rsqrt(jnp.mean(x * x, axis=-1, keepdims=True) + RMS_EPS) * g


def _sigmoid(x):
    return 1.0 / (1.0 + jnp.exp(-x))


def _shift_rows(x, carry_rows, n):
    nc = carry_rows.shape[0]
    out = pltpu.roll(x, n, 0)
    rows = lax.broadcasted_iota(jnp.int32, (nc, 1), 0)
    head = out[:nc]
    for j in range(n):
        head = jnp.where(rows == j, carry_rows[nc - n + j:nc - n + j + 1, :], head)
    return jnp.concatenate([head, out[nc:]], axis=0)


def _const_spec(*shape):
    return pl.BlockSpec(shape, lambda *_: (0,) * len(shape), pipeline_mode=pl.Buffered(1))


def _memkv_kernel(mem_ref, g_ref, w_ref, k_ref, v_ref):
    m = _rms(mem_ref[0], g_ref[...])
    kv = _bdot(m, w_ref[...])
    k_ref[0] = kv[:, :D_MODEL].astype(BF16)
    v_ref[0] = kv[:, D_MODEL:].astype(BF16)


def _memkv(mem, g, w_kv):
    b, n, d = mem.shape
    return pl.pallas_call(
        _memkv_kernel,
        grid=(b,),
        in_specs=[pl.BlockSpec((1, n, d), lambda i: (i, 0, 0)),
                  pl.BlockSpec((1, d), lambda i: (0, 0)),
                  pl.BlockSpec((d, 2 * d), lambda i: (0, 0))],
        out_specs=[pl.BlockSpec((1, n, d), lambda i: (i, 0, 0))] * 2,
        out_shape=[jax.ShapeDtypeStruct((b, n, d), BF16)] * 2,
        compiler_params=pltpu.CompilerParams(dimension_semantics=("arbitrary",),
                                             vmem_limit_bytes=VMEM_LIMIT),
        name="memkv",
    )(mem, g, w_kv)


def _inproj_kernel(x_ref, nm_ref, wrw_ref, wconv_ref, mu_ref, vec_ref,
                   ww_ref, wa_ref, wg_ref, tri_ref,
                   rt_ref, at_ref, bt_ref, kt_ref, bh_ref, kh_ref, v_ref, wc_ref,
                   g_ref, bonus_ref, ob_ref,
                   pcarry, ucarry):
    @pl.when(pl.program_id(1) == 0)
    def _():
        pcarry[...] = jnp.zeros_like(pcarry)
        ucarry[...] = jnp.zeros_like(ucarry)

    t = x_ref.shape[1]
    h = _rms(x_ref[0], nm_ref[...]).astype(BF16)

    pc = jnp.dot(h, wconv_ref[...], preferred_element_type=F32)
    u = pc[:, D_CONV:2 * D_CONV] * pc[:, 2 * D_CONV:]
    uc = ucarry[...]
    conv = (vec_ref[ROW_CONV:ROW_CONV + 1, :] * _shift_rows(u, uc, 2)
            + vec_ref[ROW_CONV + 1:ROW_CONV + 2, :] * _shift_rows(u, uc, 1)
            + vec_ref[ROW_CONV + 2:ROW_CONV + 3, :] * u)
    ob_ref[0] = (pc[:, :D_CONV] * conv).astype(BF16)
    ucarry[...] = u[t - 8:, :]

    p = jnp.dot(h, wrw_ref[...], preferred_element_type=F32)
    ps = _shift_rows(p, pcarry[...], 1)
    pcarry[...] = p[t - 8:, :]
    xm = p + (ps - p) * mu_ref[...]
    r = xm[:, :D_RWKV]
    k = xm[:, D_RWKV:2 * D_RWKV]
    v = xm[:, 2 * D_RWKV:3 * D_RWKV]
    wa = xm[:, WA_OFF:GD_OFF]
    gd = xm[:, GD_OFF:]

    def vec(row):
        return vec_ref[row:row + 1, :]

    ld = -EXP_MINUS_HALF * _sigmoid(vec(ROW_W0) + _bdot(jnp.tanh(wa), ww_ref[...]))
    a = _sigmoid(vec(ROW_A0) + _bdot(wa, wa_ref[...]))
    g_ref[0] = _bdot(_sigmoid(gd), wg_ref[...])
    kk = k * vec(ROW_KK)
    kk = kk * lax.rsqrt(jnp.maximum(_head_sums(kk * kk), 1e-24))
    k2 = k * (1.0 + (a - 1.0) * vec(ROW_KA))
    bonus_ref[0] = _head_sums(r * k2 * vec(ROW_RK)) * v + vec(ROW_LNB)
    v_ref[0] = v.astype(BF16)
    na = -kk
    nb = kk * a

    tri = tri_ref[...]
    for c in range(t // CHUNK):
        sl = slice(c * CHUNK, (c + 1) * CHUNK)
        ldc = ld[sl]
        cum = _dot_sel_lhs(tri, ldc)
        tot = cum[CHUNK - 1:CHUNK, :]
        e_inc = jnp.exp(cum)
        e_inv = 1.0 / e_inc
        e_prev = jnp.exp(cum - ldc)
        e_tot = jnp.exp(tot)
        e_end = e_tot * e_inv
        rt_ref[0, sl, :] = (r[sl] * e_inc).astype(BF16)
        at_ref[0, sl, :] = (na[sl] * e_prev).astype(BF16)
        bt_ref[0, sl, :] = (nb[sl] * e_inv).astype(BF16)
        kt_ref[0, sl, :] = (k2[sl] * e_inv).astype(BF16)
        bh_ref[0, sl, :] = (nb[sl] * e_end).astype(BF16)
        kh_ref[0, sl, :] = (k2[sl] * e_end).astype(BF16)
        wc_ref[0, c] = e_tot


def _inproj(x, nm, wrw, wconv, mu, vec, lora, tri, tile):
    b, s, d = x.shape
    nt = s // tile
    const = _const_spec
    tok = lambda w: pl.BlockSpec((1, tile, w), lambda i, j: (i, j, 0))
    outs = ([jax.ShapeDtypeStruct((b, s, D_RWKV), BF16)] * 7
            + [jax.ShapeDtypeStruct((b, s // CHUNK, 1, D_RWKV), F32)]
            + [jax.ShapeDtypeStruct((b, s, D_RWKV), F32)] * 2
            + [jax.ShapeDtypeStruct((b, s, D_CONV), BF16)])
    out_specs = ([tok(D_RWKV)] * 7
                 + [pl.BlockSpec((1, tile // CHUNK, 1, D_RWKV), lambda i, j: (i, j, 0, 0))]
                 + [tok(D_RWKV)] * 2 + [tok(D_CONV)])
    return pl.pallas_call(
        _inproj_kernel,
        grid=(b, nt),
        in_specs=[tok(d), const(1, d), const(d, RW_COLS), const(d, 3 * D_CONV),
                  const(1, RW_COLS), const(VEC_ROWS, D_RWKV)]
                 + [const(*w.shape) for w in lora]
                 + [const(CHUNK, CHUNK)],
        out_specs=out_specs,
        out_shape=outs,
        scratch_shapes=[pltpu.VMEM((8, RW_COLS), F32), pltpu.VMEM((8, D_CONV), F32)],
        compiler_params=pltpu.CompilerParams(dimension_semantics=("arbitrary", "arbitrary"),
                                             vmem_limit_bytes=VMEM_LIMIT),
        name="inproj",
    )(x, nm, wrw, wconv, mu, vec, *lora, tri)


def _pair_expand(x):
    even = lax.broadcasted_iota(jnp.int32, x.shape, 1) < RWKV_HEAD
    zero = jnp.zeros_like(x)
    return jnp.concatenate([jnp.where(even, x, zero), jnp.where(even, zero, x)], axis=0)


def _rows(*parts):
    return jnp.concatenate(parts, axis=0)


def _wkv_masks():
    row = lax.broadcasted_iota(jnp.int32, (CHUNK, PAIR), 0)
    col = lax.broadcasted_iota(jnp.int32, (CHUNK, PAIR), 1) % RWKV_HEAD
    row2 = lax.broadcasted_iota(jnp.int32, (PAIR, PAIR), 0) < RWKV_HEAD
    col2 = lax.broadcasted_iota(jnp.int32, (PAIR, PAIR), 1) < RWKV_HEAD
    return dict(strict=col < row, incl=col <= row, eye=(col == row).astype(F32),
                same_head=row2 == col2)


def _wkv_prep(refs, wc_ref, c, inst, m, out):
    rows = slice(c * CHUNK, (c + 1) * CHUNK)
    load = lambda ref: [ref[b, rows, p * PAIR:(p + 1) * PAIR] for b, p in inst]
    r, a, bt, kt, bh, kh, v = (load(ref) for ref in refs)
    n = range(len(inst))
    ex = lambda x: _pair_expand(x.astype(BF16))
    g = [_bdot_nt(_rows(a[i], r[i]), _rows(_pair_expand(bt[i]), _pair_expand(kt[i]))) for i in n]
    yield
    lab = [jnp.where(m["strict"], g[i][:CHUNK, :PAIR], 0.0) for i in n]
    lak = [jnp.where(m["strict"], g[i][:CHUNK, PAIR:], 0.0) for i in n]
    mrb = [jnp.where(m["incl"], g[i][CHUNK:, :PAIR], 0.0) for i in n]
    mrk = [jnp.where(m["incl"], g[i][CHUNK:, PAIR:], 0.0) for i in n]
    npow = [_bdot(lab[i], ex(lab[i])) for i in n]
    tinv = [m["eye"] + lab[i] for i in n]
    lmv = [_bdot(_rows(lak[i], mrk[i]), _pair_expand(v[i])) for i in n]
    yield
    for _ in range(4):
        prod = [_bdot(_rows(tinv[i], npow[i]), ex(npow[i])) for i in n]
        tinv = [tinv[i] + prod[i][:CHUNK] for i in n]
        npow = [prod[i][CHUNK:] for i in n]
        yield
    tinv = [tinv[i] + _bdot(tinv[i], ex(npow[i])) for i in n]
    yield
    pq = [_bdot(tinv[i], jnp.concatenate([_pair_expand(a[i]), ex(lmv[i][:CHUNK])], axis=1))
          for i in n]
    bkt = [_rows(bh[i], kh[i]).astype(F32).T.astype(BF16) for i in n]
    wcol = [jnp.broadcast_to(wc_ref[b, c, :, p * PAIR:(p + 1) * PAIR], (PAIR, PAIR)).T
            for b, p in inst]
    out.update(p=[pq[i][:, :PAIR].astype(BF16) for i in n], q=[pq[i][:, PAIR:] for i in n],
               r=r, v=v, bkt=bkt, wcol=wcol, mrb=mrb, mv=[lmv[i][CHUNK:] for i in n])
    yield


def _wkv_chain(pre, z, y_ref, c, inst, m):
    rows = slice(c * CHUNK, (c + 1) * CHUNK)
    n = range(len(inst))
    pr = [_bdot(_rows(pre["p"][i], pre["r"][i]), z[i]) for i in n]
    yield
    u = [(pr[i][:CHUNK] + pre["q"][i]).astype(BF16) for i in n]
    for i in n:
        z[i] = pre["wcol"][i] * z[i] + jnp.where(
            m["same_head"], _bdot(pre["bkt"][i], _rows(u[i], pre["v"][i])), 0.0)
    yield
    for i, (b, p) in enumerate(inst):
        y_ref[b, rows, p * PAIR:(p + 1) * PAIR] = (
            pr[i][CHUNK:] + _bdot(pre["mrb"][i], _pair_expand(u[i])) + pre["mv"][i])
    yield


def _interleave(primary, secondary, ratio):
    for _ in primary:
        for _ in itertools.islice(secondary, ratio):
            pass
    for _ in secondary:
        pass


def _cast_stages(src_refs, dst_refs):
    for src, dst in zip(src_refs, dst_refs):
        dst[...] = src[:, src.shape[1] - dst.shape[1]:].astype(dst.dtype)
        yield


def _wkv_kernel(rt_ref, at_ref, bt_ref, kt_ref, bh_ref, kh_ref, v_ref, wc_ref, *rest):
    nw = (len(rest) - 2) // 2
    w_refs, y_ref, wb_refs, state = rest[:nw], rest[nw], rest[nw + 1:-1], rest[-1]

    @pl.when(pl.program_id(0) == 0)
    def _():
        state[...] = jnp.zeros_like(state)

    casts = _cast_stages(w_refs, wb_refs)
    nb, t, _ = rt_ref.shape
    nc = t // CHUNK
    inst = [(b, p) for b in range(nb) for p in range(N_PAIRS)]
    m = _wkv_masks()
    refs = (rt_ref, at_ref, bt_ref, kt_ref, bh_ref, kh_ref, v_ref)
    z = [state[b, p] for b, p in inst]
    pre = [dict() for _ in range(nc)]
    for _ in _wkv_prep(refs, wc_ref, 0, inst, m, pre[0]):
        pass
    for c in range(nc):
        nxt = _wkv_prep(refs, wc_ref, c + 1, inst, m, pre[c + 1]) if c + 1 < nc else iter(())
        _interleave(_wkv_chain(pre[c], z, y_ref, c, inst, m), nxt, PREP_STAGES_PER_CHAIN_STAGE)
        next(casts, None)
    for _ in casts:
        pass
    for i, (b, p) in enumerate(inst):
        state[b, p] = z[i]


def _wkv(rt, at, bt, kt, bh, kh, v, wc, weights, tile):
    b, s, d = rt.shape
    steps = s // tile
    tok = pl.BlockSpec((b, tile, d), lambda j: (0, j, 0))
    rows = lambda w, cols: pl.BlockSpec((w.shape[0] // steps, cols), lambda j: (j, 0))
    out = pl.pallas_call(
        _wkv_kernel,
        grid=(steps,),
        in_specs=[tok] * 7 + [pl.BlockSpec((b, tile // CHUNK, 1, d), lambda j: (0, j, 0, 0))]
                 + [rows(w, w.shape[1]) for w, _ in weights],
        out_specs=[tok] + [rows(w, cols) for w, cols in weights],
        out_shape=[jax.ShapeDtypeStruct((b, s, d), F32)]
                  + [jax.ShapeDtypeStruct((w.shape[0], cols), BF16) for w, cols in weights],
        scratch_shapes=[pltpu.VMEM((b, N_PAIRS, PAIR, PAIR), F32)],
        compiler_params=pltpu.CompilerParams(dimension_semantics=("arbitrary",),
                                             vmem_limit_bytes=VMEM_LIMIT),
        name="wkv",
    )(rt, at, bt, kt, bh, kh, v, wc, *(w for w, _ in weights))
    return out[0], out[1:]


def _mix_kernel(x_ref, y_ref, g_ref, bonus_ref, ob_ref, k_ref, v_ref, vec_ref, nm_ref, bg_ref,
                wgate_ref, wpa_ref, wpb_ref, wom_ref, nx_ref, wq_ref, wxo_ref, out_ref):
    x = x_ref[0]
    h = _rms(x, nm_ref[...]).astype(BF16)
    sg = _sigmoid(jnp.dot(h, wgate_ref[...], preferred_element_type=F32) + bg_ref[...])

    y = y_ref[0]
    yc = y - _head_sums(y) * (1.0 / RWKV_HEAD)
    var = _head_sums(yc * yc) * (1.0 / RWKV_HEAD)
    yn = yc * lax.rsqrt(var + LN_X_EPS)
    oa = (yn * vec_ref[ROW_LNW:ROW_LNW + 1, :] + bonus_ref[0]) * g_ref[0]

    merged = (sg[:, :D_MODEL] * _bdot(oa, wpa_ref[...])
              + sg[:, D_MODEL:] * jnp.dot(ob_ref[0], wpb_ref[...], preferred_element_type=F32))
    x1 = x + _bdot(merged, wom_ref[...])

    q = _bdot(_rms(x1, nx_ref[...]), wq_ref[...]) * (XHEAD_DIM ** -0.5)
    heads = []
    for hh in range(N_XHEADS):
        cols = slice(hh * XHEAD_DIM, (hh + 1) * XHEAD_DIM)
        sc = _bdot_nt(q[:, cols], k_ref[0, :, cols])
        e = jnp.exp(sc - jnp.max(sc, axis=-1, keepdims=True))
        pr = e / jnp.sum(e, axis=-1, keepdims=True)
        heads.append(_bdot(pr, v_ref[0, :, cols]))
    o = jnp.concatenate(heads, axis=-1)
    out_ref[0] = x1 + _bdot(o, wxo_ref[...])


def _mix(x, y, g, bonus, ob, kmem, vmem, vec, nm, bg, wgate, wpa, wpb, wom, nx, wq, wxo, tile):
    b, s, d = x.shape
    n_mem = kmem.shape[1]
    const = _const_spec
    tok = lambda w: pl.BlockSpec((1, tile, w), lambda i, j: (i, j, 0))
    memspec = pl.BlockSpec((1, n_mem, d), lambda i, j: (i, 0, 0))
    return pl.pallas_call(
        _mix_kernel,
        grid=(b, s // tile),
        in_specs=[tok(d), tok(D_RWKV), tok(D_RWKV), tok(D_RWKV), tok(D_CONV), memspec, memspec,
                  const(VEC_ROWS, D_RWKV), const(1, d), const(1, 2 * d), const(d, 2 * d),
                  const(D_RWKV, d), const(D_CONV, d), const(d, d), const(1, d), const(d, d),
                  const(d, d)],
        out_specs=tok(d),
        out_shape=jax.ShapeDtypeStruct((b, s, d), F32),
        compiler_params=pltpu.CompilerParams(dimension_semantics=("arbitrary", "arbitrary"),
                                             vmem_limit_bytes=VMEM_LIMIT),
        name="mix",
    )(x, y, g, bonus, ob, kmem, vmem, vec, nm, bg, wgate, wpa, wpb, wom, nx, wq, wxo)


def _mlp_kernel(x_ref, nm_ref, wup_ref, wdown_ref, nf_ref, out_ref):
    x = x_ref[...]
    h = _rms(x, nm_ref[...]).astype(BF16)
    acc = x
    for c in range(D_FF // D_MODEL):
        cols = slice(c * D_MODEL, (c + 1) * D_MODEL)
        up = jnp.maximum(jnp.dot(h, wup_ref[:, cols], preferred_element_type=F32), 0.0)
        acc = acc + _bdot(up * up, wdown_ref[cols, :])
    out_ref[...] = _rms(acc, nf_ref[...])


def _mlp(x, nm, wup, wdown, nf, tile):
    n, d = x.shape
    const = _const_spec
    tok = pl.BlockSpec((tile, d), lambda i: (i, 0))
    return pl.pallas_call(
        _mlp_kernel,
        grid=(n // tile,),
        in_specs=[tok, const(1, d), const(d, D_FF), const(D_FF, d), const(1, d)],
        out_specs=tok,
        out_shape=jax.ShapeDtypeStruct((n, d), F32),
        compiler_params=pltpu.CompilerParams(dimension_semantics=("arbitrary",),
                                             vmem_limit_bytes=VMEM_LIMIT),
        name="mlp",
    )(x, nm, wup, wdown, nf)


def _pad_cols(w, width):
    return jnp.pad(w, ((0, 0), (0, width - w.shape[1])))


def _pad_rows(w, height):
    return jnp.pad(w, ((0, height - w.shape[0]), (0, 0)))


def _layer(x, mem, norm_mix, w_in, b_gate, mu_shift, w0, w_lora_w, a0, w_lora_a, w_lora_g,
           k_k, k_a, r_k, ln_x_w, ln_x_b, conv_w, w_proj_a, w_proj_b, w_out_mix, norm_xattn,
           norm_mem, w_q, w_kv, w_xo, norm_mlp, w_up, w_down, norm_final):
    bsz, s, d = x.shape
    c_conv = RW_COLS - 2 * LANES + GATE_LORA
    c_gate = c_conv + 3 * D_CONV
    w_rc = w_in[:, :c_gate].astype(BF16)
    wconv = w_rc[:, c_conv:]
    mu = _pad_cols(mu_shift[None], RW_COLS)
    lora = [jnp.pad(w, ((top, rows - top - w.shape[0]), (0, 0))).astype(BF16)
            for w, top, rows in ((w_lora_w, 0, LANES), (w_lora_a, DECAY_LORA, LANES),
                                 (w_lora_g, 0, 2 * LANES))]
    vec = jnp.stack([w0, a0, k_k, k_a, r_k.reshape(-1), ln_x_b, ln_x_w,
                     conv_w[0, 0], conv_w[1, 0], conv_w[2, 0]])
    vec = _pad_rows(vec, VEC_ROWS)
    step = jnp.arange(CHUNK)
    tri = (step[None, :] <= step[:, None]).astype(BF16)

    rt, at, bt, kt, bh, kh, v, wc, g, bonus, ob = _inproj(
        x, norm_mix[None], w_rc, wconv, mu, vec, lora, tri, tile=TILE_INPROJ)
    later = (w_proj_a, w_proj_b, w_out_mix, w_q, w_xo, w_up, w_down, w_kv)
    y, (wpa, wpb, wom, wq, wxo, wup, wdown, wkv, wgate) = _wkv(
        rt, at, bt, kt, bh, kh, v, wc,
        [(w, w.shape[1]) for w in later] + [(w_in, w_in.shape[1] - c_gate)], tile=TILE_WKV)
    kmem, vmem = _memkv(mem, norm_mem[None], wkv)
    x = _mix(x, y, g, bonus, ob, kmem, vmem, vec, norm_mix[None], b_gate[None], wgate,
             wpa, wpb, wom, norm_xattn[None], wq, wxo, tile=TILE_MIX)
    return _mlp(x.reshape(bsz * s, d), norm_mlp[None], wup, wdown, norm_final[None],
                tile=TILE_MLP).reshape(bsz, s, d)


def kernel(x, mem, norm_mix, w_in, b_gate, mu_shift, w0, w_lora_w, a0, w_lora_a, w_lora_g, k_k, k_a, r_k, ln_x_w, ln_x_b, conv_w, w_proj_a, w_proj_b, w_out_mix, norm_xattn, norm_mem, w_q, w_kv, w_xo, norm_mlp, w_up, w_down, norm_final):
    assert w_in.shape[0] == 1, "the MLP kernel fuses the final norm: single-layer trunk only"
    per_layer = (norm_mix, w_in, b_gate, mu_shift, w0, w_lora_w, a0, w_lora_a, w_lora_g, k_k, k_a,
                 r_k, ln_x_w, ln_x_b, conv_w, w_proj_a, w_proj_b, w_out_mix, norm_xattn, norm_mem,
                 w_q, w_kv, w_xo, norm_mlp, w_up, w_down)
    return _layer(x, mem, *(p[0] for p in per_layer), norm_final)
```

```python
import functools
import itertools
import math

import jax
import jax.numpy as jnp
from jax import lax
from jax.experimental import pallas as pl
from jax.experimental.pallas import tpu as pltpu

F32 = jnp.float32
BF16 = jnp.bfloat16

D_MODEL = 1024
D_RWKV = 512
RWKV_HEAD = 64
DECAY_LORA = 64
AAA_LORA = 64
GATE_LORA = 160
LN_X_EPS = 64e-5
D_CONV = 512
N_XHEADS = 4
XHEAD_DIM = D_MODEL // N_XHEADS
D_FF = 4 * D_MODEL
RMS_EPS = 1e-6
EXP_MINUS_HALF = math.exp(-0.5)

LANES = 128
CHUNK = 64
PAIR = 2 * RWKV_HEAD
N_PAIRS = D_RWKV // PAIR
WA_OFF = 3 * D_RWKV
GD_OFF = WA_OFF + DECAY_LORA + AAA_LORA
RW_COLS = GD_OFF + 2 * LANES
assert DECAY_LORA + AAA_LORA == LANES and GATE_LORA <= 2 * LANES
VMEM_LIMIT = 56 * 1024 * 1024

ROW_W0, ROW_A0, ROW_KK, ROW_KA, ROW_RK, ROW_LNB, ROW_LNW, ROW_CONV = 0, 1, 2, 3, 4, 5, 6, 7
VEC_ROWS = 16

TILE_INPROJ = 1024
TILE_WKV = 8 * CHUNK
PREP_STAGES_PER_CHAIN_STAGE = 3
TILE_MIX = 1024
TILE_MLP = 1024


def _bdot(a, b):
    return jnp.dot(a.astype(BF16), b.astype(BF16), preferred_element_type=F32)


def _bdot_nt(a, b):
    return lax.dot_general(a.astype(BF16), b.astype(BF16), (((1,), (1,)), ((), ())),
                           preferred_element_type=F32)


def _split2(a):
    hi = a.astype(BF16)
    lo = (a - hi.astype(F32)).astype(BF16)
    return hi, lo


def _head_sums(a):
    low = lax.broadcasted_iota(jnp.int32, (a.shape[0], LANES), 1) < RWKV_HEAD
    out = []
    for j in range(a.shape[1] // LANES):
        slab = a[:, j * LANES:(j + 1) * LANES]
        s_lo = jnp.sum(jnp.where(low, slab, 0.0), axis=-1, keepdims=True)
        s_hi = jnp.sum(jnp.where(low, 0.0, slab), axis=-1, keepdims=True)
        out.append(jnp.where(low, s_lo, s_hi))
    return jnp.concatenate(out, axis=-1)


def _dot_sel_lhs(sel, a):
    hi, lo = _split2(a)
    d = functools.partial(jnp.dot, preferred_element_type=F32)
    return d(sel, hi) + d(sel, lo)


def _rms(x, g):
    return x * lax.rsqrt(jnp.mean(x * x, axis=-1, keepdims=True) + RMS_EPS) * g


def _sigmoid(x):
    return 1.0 / (1.0 + jnp.exp(-x))


def _shift_rows(x, carry_rows, n):
    nc = carry_rows.shape[0]
    out = pltpu.roll(x, n, 0)
    rows = lax.broadcasted_iota(jnp.int32, (nc, 1), 0)
    head = out[:nc]
    for j in range(n):
        head = jnp.where(rows == j, carry_rows[nc - n + j:nc - n + j + 1, :], head)
    return jnp.concatenate([head, out[nc:]], axis=0)


def _const_spec(*shape):
    return pl.BlockSpec(shape, lambda *_: (0,) * len(shape), pipeline_mode=pl.Buffered(1))


def _memkv_kernel(mem_ref, g_ref, w_ref, k_ref, v_ref):
    m = _rms(mem_ref[0], g_ref[...])
    kv = _bdot(m, w_ref[...])
    k_ref[0] = kv[:, :D_MODEL].astype(BF16)
    v_ref[0] = kv[:, D_MODEL:].astype(BF16)


def _memkv(mem, g, w_kv):
    b, n, d = mem.shape
    return pl.pallas_call(
        _memkv_kernel,
        grid=(b,),
        in_specs=[pl.BlockSpec((1, n, d), lambda i: (i, 0, 0)),
                  pl.BlockSpec((1, d), lambda i: (0, 0)),
                  pl.BlockSpec((d, 2 * d), lambda i: (0, 0))],
        out_specs=[pl.BlockSpec((1, n, d), lambda i: (i, 0, 0))] * 2,
        out_shape=[jax.ShapeDtypeStruct((b, n, d), BF16)] * 2,
        compiler_params=pltpu.CompilerParams(dimension_semantics=("arbitrary",),
                                             vmem_limit_bytes=VMEM_LIMIT),
        name="memkv",
    )(mem, g, w_kv)


def _inproj_kernel(x_ref, nm_ref, wrw_ref, wconv_ref, mu_ref, vec_ref,
                   ww_ref, wa_ref, wg_ref, tri_ref,
                   rt_ref, at_ref, bt_ref, kt_ref, bh_ref, kh_ref, v_ref, wc_ref,
                   g_ref, bonus_ref, ob_ref,
                   pcarry, ucarry):
    @pl.when(pl.program_id(1) == 0)
    def _():
        pcarry[...] = jnp.zeros_like(pcarry)
        ucarry[...] = jnp.zeros_like(ucarry)

    t = x_ref.shape[1]
    h = _rms(x_ref[0], nm_ref[...]).astype(BF16)

    pc = jnp.dot(h, wconv_ref[...], preferred_element_type=F32)
    u = pc[:, D_CONV:2 * D_CONV] * pc[:, 2 * D_CONV:]
    uc = ucarry[...]
    conv = (vec_ref[ROW_CONV:ROW_CONV + 1, :] * _shift_rows(u, uc, 2)
            + vec_ref[ROW_CONV + 1:ROW_CONV + 2, :] * _shift_rows(u, uc, 1)
            + vec_ref[ROW_CONV + 2:ROW_CONV + 3, :] * u)
    ob_ref[0] = (pc[:, :D_CONV] * conv).astype(BF16)
    ucarry[...] = u[t - 8:, :]

    p = jnp.dot(h, wrw_ref[...], preferred_element_type=F32)
    ps = _shift_rows(p, pcarry[...], 1)
    pcarry[...] = p[t - 8:, :]
    xm = p + (ps - p) * mu_ref[...]
    r = xm[:, :D_RWKV]
    k = xm[:, D_RWKV:2 * D_RWKV]
    v = xm[:, 2 * D_RWKV:3 * D_RWKV]
    wa = xm[:, WA_OFF:GD_OFF]
    gd = xm[:, GD_OFF:]

    def vec(row):
        return vec_ref[row:row + 1, :]

    ld = -EXP_MINUS_HALF * _sigmoid(vec(ROW_W0) + _bdot(jnp.tanh(wa), ww_ref[...]))
    a = _sigmoid(vec(ROW_A0) + _bdot(wa, wa_ref[...]))
    g_ref[0] = _bdot(_sigmoid(gd), wg_ref[...])
    kk = k * vec(ROW_KK)
    kk = kk * lax.rsqrt(jnp.maximum(_head_sums(kk * kk), 1e-24))
    k2 = k * (1.0 + (a - 1.0) * vec(ROW_KA))
    bonus_ref[0] = _head_sums(r * k2 * vec(ROW_RK)) * v + vec(ROW_LNB)
    v_ref[0] = v.astype(BF16)
    na = -kk
    nb = kk * a

    tri = tri_ref[...]
    for c in range(t // CHUNK):
        sl = slice(c * CHUNK, (c + 1) * CHUNK)
        ldc = ld[sl]
        cum = _dot_sel_lhs(tri, ldc)
        tot = cum[CHUNK - 1:CHUNK, :]
        e_inc = jnp.exp(cum)
        e_inv = 1.0 / e_inc
        e_prev = jnp.exp(cum - ldc)
        e_tot = jnp.exp(tot)
        e_end = e_tot * e_inv
        rt_ref[0, sl, :] = (r[sl] * e_inc).astype(BF16)
        at_ref[0, sl, :] = (na[sl] * e_prev).astype(BF16)
        bt_ref[0, sl, :] = (nb[sl] * e_inv).astype(BF16)
        kt_ref[0, sl, :] = (k2[sl] * e_inv).astype(BF16)
        bh_ref[0, sl, :] = (nb[sl] * e_end).astype(BF16)
        kh_ref[0, sl, :] = (k2[sl] * e_end).astype(BF16)
        wc_ref[0, c] = e_tot


def _inproj(x, nm, wrw, wconv, mu, vec, lora, tri, tile):
    b, s, d = x.shape
    nt = s // tile
    const = _const_spec
    tok = lambda w: pl.BlockSpec((1, tile, w), lambda i, j: (i, j, 0))
    outs = ([jax.ShapeDtypeStruct((b, s, D_RWKV), BF16)] * 7
            + [jax.ShapeDtypeStruct((b, s // CHUNK, 1, D_RWKV), F32)]
            + [jax.ShapeDtypeStruct((b, s, D_RWKV), F32)] * 2
            + [jax.ShapeDtypeStruct((b, s, D_CONV), BF16)])
    out_specs = ([tok(D_RWKV)] * 7
                 + [pl.BlockSpec((1, tile // CHUNK, 1, D_RWKV), lambda i, j: (i, j, 0, 0))]
                 + [tok(D_RWKV)] * 2 + [tok(D_CONV)])
    return pl.pallas_call(
        _inproj_kernel,
        grid=(b, nt),
        in_specs=[tok(d), const(1, d), const(d, RW_COLS), const(d, 3 * D_CONV),
                  const(1, RW_COLS), const(VEC_ROWS, D_RWKV)]
                 + [const(*w.shape) for w in lora]
                 + [const(CHUNK, CHUNK)],
        out_specs=out_specs,
        out_shape=outs,
        scratch_shapes=[pltpu.VMEM((8, RW_COLS), F32), pltpu.VMEM((8, D_CONV), F32)],
        compiler_params=pltpu.CompilerParams(dimension_semantics=("arbitrary", "arbitrary"),
                                             vmem_limit_bytes=VMEM_LIMIT),
        name="inproj",
    )(x, nm, wrw, wconv, mu, vec, *lora, tri)


def _pair_expand(x):
    even = lax.broadcasted_iota(jnp.int32, x.shape, 1) < RWKV_HEAD
    zero = jnp.zeros_like(x)
    return jnp.concatenate([jnp.where(even, x, zero), jnp.where(even, zero, x)], axis=0)


def _rows(*parts):
    return jnp.concatenate(parts, axis=0)


def _wkv_masks():
    row = lax.broadcasted_iota(jnp.int32, (CHUNK, PAIR), 0)
    col = lax.broadcasted_iota(jnp.int32, (CHUNK, PAIR), 1) % RWKV_HEAD
    row2 = lax.broadcasted_iota(jnp.int32, (PAIR, PAIR), 0) < RWKV_HEAD
    col2 = lax.broadcasted_iota(jnp.int32, (PAIR, PAIR), 1) < RWKV_HEAD
    return dict(strict=col < row, incl=col <= row, eye=(col == row).astype(F32),
                same_head=row2 == col2)


def _wkv_prep(refs, wc_ref, c, inst, m, out):
    rows = slice(c * CHUNK, (c + 1) * CHUNK)
    load = lambda ref: [ref[b, rows, p * PAIR:(p + 1) * PAIR] for b, p in inst]
    r, a, bt, kt, bh, kh, v = (load(ref) for ref in refs)
    n = range(len(inst))
    ex = lambda x: _pair_expand(x.astype(BF16))
    g = [_bdot_nt(_rows(a[i], r[i]), _rows(_pair_expand(bt[i]), _pair_expand(kt[i]))) for i in n]
    yield
    lab = [jnp.where(m["strict"], g[i][:CHUNK, :PAIR], 0.0) for i in n]
    lak = [jnp.where(m["strict"], g[i][:CHUNK, PAIR:], 0.0) for i in n]
    mrb = [jnp.where(m["incl"], g[i][CHUNK:, :PAIR], 0.0) for i in n]
    mrk = [jnp.where(m["incl"], g[i][CHUNK:, PAIR:], 0.0) for i in n]
    npow = [_bdot(lab[i], ex(lab[i])) for i in n]
    tinv = [m["eye"] + lab[i] for i in n]
    lmv = [_bdot(_rows(lak[i], mrk[i]), _pair_expand(v[i])) for i in n]
    yield
    for _ in range(4):
        prod = [_bdot(_rows(tinv[i], npow[i]), ex(npow[i])) for i in n]
        tinv = [tinv[i] + prod[i][:CHUNK] for i in n]
        npow = [prod[i][CHUNK:] for i in n]
        yield
    tinv = [tinv[i] + _bdot(tinv[i], ex(npow[i])) for i in n]
    yield
    pq = [_bdot(tinv[i], jnp.concatenate([_pair_expand(a[i]), ex(lmv[i][:CHUNK])], axis=1))
          for i in n]
    bkt = [_rows(bh[i], kh[i]).astype(F32).T.astype(BF16) for i in n]
    wcol = [jnp.broadcast_to(wc_ref[b, c, :, p * PAIR:(p + 1) * PAIR], (PAIR, PAIR)).T
            for b, p in inst]
    out.update(p=[pq[i][:, :PAIR].astype(BF16) for i in n], q=[pq[i][:, PAIR:] for i in n],
               r=r, v=v, bkt=bkt, wcol=wcol, mrb=mrb, mv=[lmv[i][CHUNK:] for i in n])
    yield


def _wkv_chain(pre, z, y_ref, c, inst, m):
    rows = slice(c * CHUNK, (c + 1) * CHUNK)
    n = range(len(inst))
    pr = [_bdot(_rows(pre["p"][i], pre["r"][i]), z[i]) for i in n]
    yield
    u = [(pr[i][:CHUNK] + pre["q"][i]).astype(BF16) for i in n]
    for i in n:
        z[i] = pre["wcol"][i] * z[i] + jnp.where(
            m["same_head"], _bdot(pre["bkt"][i], _rows(u[i], pre["v"][i])), 0.0)
    yield
    for i, (b, p) in enumerate(inst):
        y_ref[b, rows, p * PAIR:(p + 1) * PAIR] = (
            pr[i][CHUNK:] + _bdot(pre["mrb"][i], _pair_expand(u[i])) + pre["mv"][i])
    yield


def _interleave(primary, secondary, ratio):
    for _ in primary:
        for _ in itertools.islice(secondary, ratio):
            pass
    for _ in secondary:
        pass


def _cast_stages(src_refs, dst_refs):
    for src, dst in zip(src_refs, dst_refs):
        dst[...] = src[...].astype(dst.dtype)
        yield


def _wkv_kernel(rt_ref, at_ref, bt_ref, kt_ref, bh_ref, kh_ref, v_ref, wc_ref, *rest):
    nw = (len(rest) - 2) // 2
    w_refs, y_ref, wb_refs, state = rest[:nw], rest[nw], rest[nw + 1:-1], rest[-1]

    @pl.when(pl.program_id(0) == 0)
    def _():
        state[...] = jnp.zeros_like(state)

    casts = _cast_stages(w_refs, wb_refs)
    nb, t, _ = rt_ref.shape
    nc = t // CHUNK
    inst = [(b, p) for b in range(nb) for p in range(N_PAIRS)]
    m = _wkv_masks()
    refs = (rt_ref, at_ref, bt_ref, kt_ref, bh_ref, kh_ref, v_ref)
    z = [state[b, p] for b, p in inst]
    pre = [dict() for _ in range(nc)]
    for _ in _wkv_prep(refs, wc_ref, 0, inst, m, pre[0]):
        pass
    for c in range(nc):
        nxt = _wkv_prep(refs, wc_ref, c + 1, inst, m, pre[c + 1]) if c + 1 < nc else iter(())
        _interleave(_wkv_chain(pre[c], z, y_ref, c, inst, m), nxt, PREP_STAGES_PER_CHAIN_STAGE)
        next(casts, None)
    for _ in casts:
        pass
    for i, (b, p) in enumerate(inst):
        state[b, p] = z[i]


def _wkv(rt, at, bt, kt, bh, kh, v, wc, weights, tile):
    b, s, d = rt.shape
    steps = s // tile
    tok = pl.BlockSpec((b, tile, d), lambda j: (0, j, 0))
    wspecs = [pl.BlockSpec((w.shape[0] // steps, w.shape[1]), lambda j: (j, 0)) for w in weights]
    out = pl.pallas_call(
        _wkv_kernel,
        grid=(steps,),
        in_specs=[tok] * 7 + [pl.BlockSpec((b, tile // CHUNK, 1, d), lambda j: (0, j, 0, 0))] + wspecs,
        out_specs=[tok] + wspecs,
        out_shape=[jax.ShapeDtypeStruct((b, s, d), F32)]
                  + [jax.ShapeDtypeStruct(w.shape, BF16) for w in weights],
        scratch_shapes=[pltpu.VMEM((b, N_PAIRS, PAIR, PAIR), F32)],
        compiler_params=pltpu.CompilerParams(dimension_semantics=("arbitrary",),
                                             vmem_limit_bytes=VMEM_LIMIT),
        name="wkv",
    )(rt, at, bt, kt, bh, kh, v, wc, *weights)
    return out[0], out[1:]


def _mix_kernel(x_ref, y_ref, g_ref, bonus_ref, ob_ref, k_ref, v_ref, vec_ref, nm_ref, bg_ref,
                wgate_ref, wpa_ref, wpb_ref, wom_ref, nx_ref, wq_ref, wxo_ref, out_ref):
    x = x_ref[0]
    h = _rms(x, nm_ref[...]).astype(BF16)
    sg = _sigmoid(jnp.dot(h, wgate_ref[...], preferred_element_type=F32) + bg_ref[...])

    y = y_ref[0]
    yc = y - _head_sums(y) * (1.0 / RWKV_HEAD)
    var = _head_sums(yc * yc) * (1.0 / RWKV_HEAD)
    yn = yc * lax.rsqrt(var + LN_X_EPS)
    oa = (yn * vec_ref[ROW_LNW:ROW_LNW + 1, :] + bonus_ref[0]) * g_ref[0]

    merged = (sg[:, :D_MODEL] * _bdot(oa, wpa_ref[...])
              + sg[:, D_MODEL:] * jnp.dot(ob_ref[0], wpb_ref[...], preferred_element_type=F32))
    x1 = x + _bdot(merged, wom_ref[...])

    q = _bdot(_rms(x1, nx_ref[...]), wq_ref[...]) * (XHEAD_DIM ** -0.5)
    heads = []
    for hh in range(N_XHEADS):
        cols = slice(hh * XHEAD_DIM, (hh + 1) * XHEAD_DIM)
        sc = _bdot_nt(q[:, cols], k_ref[0, :, cols])
        e = jnp.exp(sc - jnp.max(sc, axis=-1, keepdims=True))
        pr = e / jnp.sum(e, axis=-1, keepdims=True)
        heads.append(_bdot(pr, v_ref[0, :, cols]))
    o = jnp.concatenate(heads, axis=-1)
    out_ref[0] = x1 + _bdot(o, wxo_ref[...])


def _mix(x, y, g, bonus, ob, kmem, vmem, vec, nm, bg, wgate, wpa, wpb, wom, nx, wq, wxo, tile):
    b, s, d = x.shape
    n_mem = kmem.shape[1]
    const = _const_spec
    tok = lambda w: pl.BlockSpec((1, tile, w), lambda i, j: (i, j, 0))
    memspec = pl.BlockSpec((1, n_mem, d), lambda i, j: (i, 0, 0))
    return pl.pallas_call(
        _mix_kernel,
        grid=(b, s // tile),
        in_specs=[tok(d), tok(D_RWKV), tok(D_RWKV), tok(D_RWKV), tok(D_CONV), memspec, memspec,
                  const(VEC_ROWS, D_RWKV), const(1, d), const(1, 2 * d), const(d, 2 * d),
                  const(D_RWKV, d), const(D_CONV, d), const(d, d), const(1, d), const(d, d),
                  const(d, d)],
        out_specs=tok(d),
        out_shape=jax.ShapeDtypeStruct((b, s, d), F32),
        compiler_params=pltpu.CompilerParams(dimension_semantics=("arbitrary", "arbitrary"),
                                             vmem_limit_bytes=VMEM_LIMIT),
        name="mix",
    )(x, y, g, bonus, ob, kmem, vmem, vec, nm, bg, wgate, wpa, wpb, wom, nx, wq, wxo)


def _mlp_kernel(x_ref, nm_ref, wup_ref, wdown_ref, nf_ref, out_ref):
    x = x_ref[...]
    h = _rms(x, nm_ref[...]).astype(BF16)
    acc = x
    for c in range(D_FF // D_MODEL):
        cols = slice(c * D_MODEL, (c + 1) * D_MODEL)
        up = jnp.maximum(jnp.dot(h, wup_ref[:, cols], preferred_element_type=F32), 0.0)
        acc = acc + _bdot(up * up, wdown_ref[cols, :])
    out_ref[...] = _rms(acc, nf_ref[...])


def _mlp(x, nm, wup, wdown, nf, tile):
    n, d = x.shape
    const = _const_spec
    tok = pl.BlockSpec((tile, d), lambda i: (i, 0))
    return pl.pallas_call(
        _mlp_kernel,
        grid=(n // tile,),
        in_specs=[tok, const(1, d), const(d, D_FF), const(D_FF, d), const(1, d)],
        out_specs=tok,
        out_shape=jax.ShapeDtypeStruct((n, d), F32),
        compiler_params=pltpu.CompilerParams(dimension_semantics=("arbitrary",),
                                             vmem_limit_bytes=VMEM_LIMIT),
        name="mlp",
    )(x, nm, wup, wdown, nf)


def _pad_cols(w, width):
    return jnp.pad(w, ((0, 0), (0, width - w.shape[1])))


def _pad_rows(w, height):
    return jnp.pad(w, ((0, height - w.shape[0]), (0, 0)))


def _layer(x, mem, norm_mix, w_in, b_gate, mu_shift, w0, w_lora_w, a0, w_lora_a, w_lora_g,
           k_k, k_a, r_k, ln_x_w, ln_x_b, conv_w, w_proj_a, w_proj_b, w_out_mix, norm_xattn,
           norm_mem, w_q, w_kv, w_xo, norm_mlp, w_up, w_down, norm_final):
    bsz, s, d = x.shape
    c_conv = RW_COLS - 2 * LANES + GATE_LORA
    c_gate = c_conv + 3 * D_CONV
    w_in = w_in.astype(BF16)
    wconv = w_in[:, c_conv:c_gate]
    wgate = w_in[:, c_gate:]
    mu = _pad_cols(mu_shift[None], RW_COLS)
    lora = [jnp.pad(w, ((top, rows - top - w.shape[0]), (0, 0))).astype(BF16)
            for w, top, rows in ((w_lora_w, 0, LANES), (w_lora_a, DECAY_LORA, LANES),
                                 (w_lora_g, 0, 2 * LANES))]
    vec = jnp.stack([w0, a0, k_k, k_a, r_k.reshape(-1), ln_x_b, ln_x_w,
                     conv_w[0, 0], conv_w[1, 0], conv_w[2, 0]])
    vec = _pad_rows(vec, VEC_ROWS)
    step = jnp.arange(CHUNK)
    tri = (step[None, :] <= step[:, None]).astype(BF16)

    rt, at, bt, kt, bh, kh, v, wc, g, bonus, ob = _inproj(
        x, norm_mix[None], w_in, wconv, mu, vec, lora, tri, tile=TILE_INPROJ)
    later = (w_proj_a, w_proj_b, w_out_mix, w_q, w_xo, w_up, w_down, w_kv)
    y, (wpa, wpb, wom, wq, wxo, wup, wdown, wkv) = _wkv(
        rt, at, bt, kt, bh, kh, v, wc, later, tile=TILE_WKV)
    kmem, vmem = _memkv(mem, norm_mem[None], wkv)
    x = _mix(x, y, g, bonus, ob, kmem, vmem, vec, norm_mix[None], b_gate[None], wgate,
             wpa, wpb, wom, norm_xattn[None], wq, wxo, tile=TILE_MIX)
    return _mlp(x.reshape(bsz * s, d), norm_mlp[None], wup, wdown, norm_final[None],
                tile=TILE_MLP).reshape(bsz, s, d)


def kernel(x, mem, norm_mix, w_in, b_gate, mu_shift, w0, w_lora_w, a0, w_lora_a, w_lora_g, k_k, k_a, r_k, ln_x_w, ln_x_b, conv_w, w_proj_a, w_proj_b, w_out_mix, norm_xattn, norm_mem, w_q, w_kv, w_xo, norm_mlp, w_up, w_down, norm_final):
    assert w_in.shape[0] == 1, "the MLP kernel fuses the final norm: single-layer trunk only"
    per_layer = (norm_mix, w_in, b_gate, mu_shift, w0, w_lora_w, a0, w_lora_a, w_lora_g, k_k, k_a,
                 r_k, ln_x_w, ln_x_b, conv_w, w_proj_a, w_proj_b, w_out_mix, norm_xattn, norm_mem,
                 w_q, w_kv, w_xo, norm_mlp, w_up, w_down)
    return _layer(x, mem, *(p[0] for p in per_layer), norm_final)
```

```python
import functools
import itertools
import math

import jax
import jax.numpy as jnp
from jax import lax
from jax.experimental import pallas as pl
from jax.experimental.pallas import tpu as pltpu

F32 = jnp.float32
BF16 = jnp.bfloat16

D_MODEL = 1024
D_RWKV = 512
RWKV_HEAD = 64
DECAY_LORA = 64
AAA_LORA = 64
GATE_LORA = 160
LN_X_EPS = 64e-5
D_CONV = 512
N_XHEADS = 4
XHEAD_DIM = D_MODEL // N_XHEADS
D_FF = 4 * D_MODEL
RMS_EPS = 1e-6
EXP_MINUS_HALF = math.exp(-0.5)

LANES = 128
SUBLANES = 8
CHUNK = 64
PAIR = 2 * RWKV_HEAD
N_PAIRS = D_RWKV // PAIR
WA_OFF = 3 * D_RWKV
GD_OFF = WA_OFF + DECAY_LORA + AAA_LORA
RW_COLS = GD_OFF + 2 * LANES
assert DECAY_LORA + AAA_LORA == LANES and GATE_LORA <= 2 * LANES
VMEM_LIMIT = 56 * 1024 * 1024

ROW_W0, ROW_A0, ROW_KK, ROW_KA, ROW_RK, ROW_LNB, ROW_LNW, ROW_CONV = 0, 1, 2, 3, 4, 5, 6, 7
VEC_ROWS = 16

TILE_INPROJ = 1024
TILE_WKV = 16 * CHUNK
TILE_MIX = 1024
TILE_MLP = 1024
PREP_STAGES_PER_CHAIN_STAGE = 3


def _bdot(a, b):
    return jnp.dot(a.astype(BF16), b.astype(BF16), preferred_element_type=F32)


def _bdot_nt(a, b):
    return lax.dot_general(a.astype(BF16), b.astype(BF16), (((1,), (1,)), ((), ())),
                           preferred_element_type=F32)


def _split2(a):
    hi = a.astype(BF16)
    lo = (a - hi.astype(F32)).astype(BF16)
    return hi, lo


def _head_sums(a):
    low = lax.broadcasted_iota(jnp.int32, (a.shape[0], LANES), 1) < RWKV_HEAD
    out = []
    for j in range(a.shape[1] // LANES):
        slab = a[:, j * LANES:(j + 1) * LANES]
        s_lo = jnp.sum(jnp.where(low, slab, 0.0), axis=-1, keepdims=True)
        s_hi = jnp.sum(jnp.where(low, 0.0, slab), axis=-1, keepdims=True)
        out.append(jnp.where(low, s_lo, s_hi))
    return jnp.concatenate(out, axis=-1)


def _dot_sel_lhs(sel, a):
    hi, lo = _split2(a)
    d = functools.partial(jnp.dot, preferred_element_type=F32)
    return d(sel, hi) + d(sel, lo)


def _rms(x, g):
    return x * lax.rsqrt(jnp.mean(x * x, axis=-1, keepdims=True) + RMS_EPS) * g


def _sigmoid(x):
    return 1.0 / (1.0 + jnp.exp(-x))


def _shift_rows(x, carry_rows, n):
    nc = carry_rows.shape[0]
    out = pltpu.roll(x, n, 0)
    rows = lax.broadcasted_iota(jnp.int32, (nc, 1), 0)
    head = out[:nc]
    for j in range(n):
        head = jnp.where(rows == j, carry_rows[nc - n + j:nc - n + j + 1, :], head)
    return jnp.concatenate([head, out[nc:]], axis=0)


def _const_spec(*shape):
    return pl.BlockSpec(shape, lambda *_: (0,) * len(shape), pipeline_mode=pl.Buffered(1))


def _memkv_kernel(mem_ref, g_ref, w_ref, k_ref, v_ref):
    m = _rms(mem_ref[0], g_ref[...])
    kv = _bdot(m, w_ref[...])
    k_ref[0] = kv[:, :D_MODEL].astype(BF16)
    v_ref[0] = kv[:, D_MODEL:].astype(BF16)


def _memkv(mem, g, w_kv):
    b, n, d = mem.shape
    return pl.pallas_call(
        _memkv_kernel,
        grid=(b,),
        in_specs=[pl.BlockSpec((1, n, d), lambda i: (i, 0, 0)),
                  pl.BlockSpec((1, d), lambda i: (0, 0)),
                  pl.BlockSpec((d, 2 * d), lambda i: (0, 0))],
        out_specs=[pl.BlockSpec((1, n, d), lambda i: (i, 0, 0))] * 2,
        out_shape=[jax.ShapeDtypeStruct((b, n, d), BF16)] * 2,
        compiler_params=pltpu.CompilerParams(dimension_semantics=("arbitrary",),
                                             vmem_limit_bytes=VMEM_LIMIT),
        name="memkv",
    )(mem, g, w_kv)


def _inproj_kernel(x_ref, nm_ref, wrw_ref, wconv_ref, mu_ref, vec_ref,
                   ww_ref, wa_ref, wg_ref, tri_ref,
                   rt_ref, at_ref, bt_ref, kt_ref, v_ref, wc_ref,
                   g_ref, bonus_ref, ob_ref,
                   pcarry, ucarry):
    @pl.when(pl.program_id(1) == 0)
    def _():
        pcarry[...] = jnp.zeros_like(pcarry)
        ucarry[...] = jnp.zeros_like(ucarry)

    t = x_ref.shape[1]
    h = _rms(x_ref[0], nm_ref[...]).astype(BF16)

    pc = jnp.dot(h, wconv_ref[...], preferred_element_type=F32)
    u = pc[:, D_CONV:2 * D_CONV] * pc[:, 2 * D_CONV:]
    uc = ucarry[...]
    conv = (vec_ref[ROW_CONV:ROW_CONV + 1, :] * _shift_rows(u, uc, 2)
            + vec_ref[ROW_CONV + 1:ROW_CONV + 2, :] * _shift_rows(u, uc, 1)
            + vec_ref[ROW_CONV + 2:ROW_CONV + 3, :] * u)
    ob_ref[0] = (pc[:, :D_CONV] * conv).astype(BF16)
    ucarry[...] = u[t - SUBLANES:, :]

    p = jnp.dot(h, wrw_ref[...], preferred_element_type=F32)
    ps = _shift_rows(p, pcarry[...], 1)
    pcarry[...] = p[t - SUBLANES:, :]
    xm = p + (ps - p) * mu_ref[...]
    r = xm[:, :D_RWKV]
    k = xm[:, D_RWKV:2 * D_RWKV]
    v = xm[:, 2 * D_RWKV:3 * D_RWKV]
    wa = xm[:, WA_OFF:GD_OFF]
    gd = xm[:, GD_OFF:]

    def vec(row):
        return vec_ref[row:row + 1, :]

    ld = -EXP_MINUS_HALF * _sigmoid(vec(ROW_W0) + _bdot(jnp.tanh(wa), ww_ref[...]))
    a = _sigmoid(vec(ROW_A0) + _bdot(wa, wa_ref[...]))
    g_ref[0] = _bdot(_sigmoid(gd), wg_ref[...])
    kk = k * vec(ROW_KK)
    kk = kk * lax.rsqrt(jnp.maximum(_head_sums(kk * kk), 1e-24))
    k2 = k * (1.0 + (a - 1.0) * vec(ROW_KA))
    bonus_ref[0] = _head_sums(r * k2 * vec(ROW_RK)) * v + vec(ROW_LNB)
    v_ref[0] = v.astype(BF16)
    na = -kk
    nb = kk * a

    tri = tri_ref[...]
    for c in range(t // CHUNK):
        sl = slice(c * CHUNK, (c + 1) * CHUNK)
        ldc = ld[sl]
        cum = _dot_sel_lhs(tri, ldc)
        tot = cum[CHUNK - 1:CHUNK, :]
        e_inc = jnp.exp(cum)
        e_inv = 1.0 / e_inc
        e_prev = jnp.exp(cum - ldc)
        rt_ref[0, sl, :] = (r[sl] * e_inc).astype(BF16)
        at_ref[0, sl, :] = (na[sl] * e_prev).astype(BF16)
        bt_ref[0, sl, :] = (nb[sl] * e_inv).astype(BF16)
        kt_ref[0, sl, :] = (k2[sl] * e_inv).astype(BF16)
        wc_ref[0, c] = jnp.exp(tot)


def _inproj(x, nm, wrw, wconv, mu, vec, lora, tri, tile):
    b, s, d = x.shape
    nt = s // tile
    const = _const_spec
    tok = lambda w: pl.BlockSpec((1, tile, w), lambda i, j: (i, j, 0))
    outs = ([jax.ShapeDtypeStruct((b, s, D_RWKV), BF16)] * 5
            + [jax.ShapeDtypeStruct((b, s // CHUNK, 1, D_RWKV), F32)]
            + [jax.ShapeDtypeStruct((b, s, D_RWKV), F32)] * 2
            + [jax.ShapeDtypeStruct((b, s, D_CONV), BF16)])
    out_specs = ([tok(D_RWKV)] * 5
                 + [pl.BlockSpec((1, tile // CHUNK, 1, D_RWKV), lambda i, j: (i, j, 0, 0))]
                 + [tok(D_RWKV)] * 2 + [tok(D_CONV)])
    return pl.pallas_call(
        _inproj_kernel,
        grid=(b, nt),
        in_specs=[tok(d), const(1, d), const(d, RW_COLS), const(d, 3 * D_CONV),
                  const(1, RW_COLS), const(VEC_ROWS, D_RWKV)]
                 + [const(*w.shape) for w in lora]
                 + [const(CHUNK, CHUNK)],
        out_specs=out_specs,
        out_shape=outs,
        scratch_shapes=[pltpu.VMEM((SUBLANES, RW_COLS), F32), pltpu.VMEM((SUBLANES, D_CONV), F32)],
        compiler_params=pltpu.CompilerParams(dimension_semantics=("arbitrary", "arbitrary"),
                                             vmem_limit_bytes=VMEM_LIMIT),
        name="inproj",
    )(x, nm, wrw, wconv, mu, vec, *lora, tri)


def _pair_expand(x):
    even = lax.broadcasted_iota(jnp.int32, x.shape, 1) < RWKV_HEAD
    zero = jnp.zeros_like(x)
    return jnp.concatenate([jnp.where(even, x, zero), jnp.where(even, zero, x)], axis=0)


def _rows(*parts):
    return jnp.concatenate(parts, axis=0)


def _wkv_masks():
    row = lax.broadcasted_iota(jnp.int32, (CHUNK, PAIR), 0)
    col = lax.broadcasted_iota(jnp.int32, (CHUNK, PAIR), 1) % RWKV_HEAD
    row2 = lax.broadcasted_iota(jnp.int32, (PAIR, PAIR), 0) < RWKV_HEAD
    col2 = lax.broadcasted_iota(jnp.int32, (PAIR, PAIR), 1) < RWKV_HEAD
    return dict(strict=col < row, incl=col <= row, eye=(col == row).astype(F32),
                same_head=row2 == col2)


def _wkv_prep(refs, wc_ref, c, inst, m, out):
    rows = slice(c * CHUNK, (c + 1) * CHUNK)
    load = lambda ref: [ref[b, rows, p * PAIR:(p + 1) * PAIR] for b, p in inst]
    r, a, bt, kt, v = (load(ref) for ref in refs)
    n = range(len(inst))
    ex = lambda x: _pair_expand(x.astype(BF16))
    g = [_bdot_nt(_rows(a[i], r[i]), _rows(_pair_expand(bt[i]), _pair_expand(kt[i]))) for i in n]
    yield
    lab = [jnp.where(m["strict"], g[i][:CHUNK, :PAIR], 0.0) for i in n]
    lak = [jnp.where(m["strict"], g[i][:CHUNK, PAIR:], 0.0) for i in n]
    mrb = [jnp.where(m["incl"], g[i][CHUNK:, :PAIR], 0.0) for i in n]
    mrk = [jnp.where(m["incl"], g[i][CHUNK:, PAIR:], 0.0) for i in n]
    npow = [_bdot(lab[i], ex(lab[i])) for i in n]
    tinv = [m["eye"] + lab[i] for i in n]
    lmv = [_bdot(_rows(lak[i], mrk[i]), _pair_expand(v[i])) for i in n]
    yield
    for _ in range(4):
        prod = [_bdot(_rows(tinv[i], npow[i]), ex(npow[i])) for i in n]
        tinv = [tinv[i] + prod[i][:CHUNK] for i in n]
        npow = [prod[i][CHUNK:] for i in n]
        yield
    tinv = [tinv[i] + _bdot(tinv[i], ex(npow[i])) for i in n]
    yield
    pq = [_bdot(tinv[i], jnp.concatenate([_pair_expand(a[i]), ex(lmv[i][:CHUNK])], axis=1))
          for i in n]
    bkt = [_rows(bt[i], kt[i]).astype(F32).T.astype(BF16) for i in n]
    wcol = [jnp.broadcast_to(wc_ref[b, c, :, p * PAIR:(p + 1) * PAIR], (PAIR, PAIR)).T
            for b, p in inst]
    out.update(p=[pq[i][:, :PAIR].astype(BF16) for i in n], q=[pq[i][:, PAIR:] for i in n],
               r=r, v=v, bkt=bkt, wcol=wcol, mrb=mrb, mv=[lmv[i][CHUNK:] for i in n])
    yield


def _wkv_chain(pre, z, y_ref, c, inst, m):
    rows = slice(c * CHUNK, (c + 1) * CHUNK)
    n = range(len(inst))
    pr = [_bdot(_rows(pre["p"][i], pre["r"][i]), z[i]) for i in n]
    yield
    u = [(pr[i][:CHUNK] + pre["q"][i]).astype(BF16) for i in n]
    for i in n:
        z[i] = pre["wcol"][i] * (z[i] + jnp.where(
            m["same_head"], _bdot(pre["bkt"][i], _rows(u[i], pre["v"][i])), 0.0))
    yield
    for i, (b, p) in enumerate(inst):
        y_ref[b, rows, p * PAIR:(p + 1) * PAIR] = (
            pr[i][CHUNK:] + _bdot(pre["mrb"][i], _pair_expand(u[i])) + pre["mv"][i])
    yield


def _interleave(primary, secondary, ratio):
    for _ in primary:
        for _ in itertools.islice(secondary, ratio):
            pass
    for _ in secondary:
        pass


def _cast_stages(src_refs, dst_refs):
    for src, dst in zip(src_refs, dst_refs):
        dst[...] = src[...].astype(dst.dtype)
        yield


def _wkv_kernel(rt_ref, at_ref, bt_ref, kt_ref, v_ref, wc_ref, *rest):
    nw = (len(rest) - 2) // 2
    w_refs, y_ref, wb_refs, state = rest[:nw], rest[nw], rest[nw + 1:-1], rest[-1]

    @pl.when(pl.program_id(0) == 0)
    def _():
        state[...] = jnp.zeros_like(state)

    casts = _cast_stages(w_refs, wb_refs)
    nb, t, _ = rt_ref.shape
    nc = t // CHUNK
    inst = [(b, p) for b in range(nb) for p in range(N_PAIRS)]
    m = _wkv_masks()
    refs = (rt_ref, at_ref, bt_ref, kt_ref, v_ref)
    z = [state[b, p] for b, p in inst]
    pre = [dict() for _ in range(nc)]
    for _ in _wkv_prep(refs, wc_ref, 0, inst, m, pre[0]):
        pass
    for c in range(nc):
        nxt = _wkv_prep(refs, wc_ref, c + 1, inst, m, pre[c + 1]) if c + 1 < nc else iter(())
        _interleave(_wkv_chain(pre[c], z, y_ref, c, inst, m), nxt, PREP_STAGES_PER_CHAIN_STAGE)
        next(casts, None)
    for _ in casts:
        pass
    for i, (b, p) in enumerate(inst):
        state[b, p] = z[i]


def _wkv(rt, at, bt, kt, v, wc, weights, tile):
    b, s, d = rt.shape
    steps = s // tile
    tok = pl.BlockSpec((b, tile, d), lambda j: (0, j, 0))
    wspecs = [pl.BlockSpec((w.shape[0] // steps, w.shape[1]), lambda j: (j, 0)) for w in weights]
    out = pl.pallas_call(
        _wkv_kernel,
        grid=(steps,),
        in_specs=[tok] * 5 + [pl.BlockSpec((b, tile // CHUNK, 1, d), lambda j: (0, j, 0, 0))] + wspecs,
        out_specs=[tok] + wspecs,
        out_shape=[jax.ShapeDtypeStruct((b, s, d), F32)]
                  + [jax.ShapeDtypeStruct(w.shape, BF16) for w in weights],
        scratch_shapes=[pltpu.VMEM((b, N_PAIRS, PAIR, PAIR), F32)],
        compiler_params=pltpu.CompilerParams(dimension_semantics=("arbitrary",),
                                             vmem_limit_bytes=VMEM_LIMIT),
        name="wkv",
    )(rt, at, bt, kt, v, wc, *weights)
    return out[0], out[1:]


def _mix_kernel(x_ref, y_ref, g_ref, bonus_ref, ob_ref, k_ref, v_ref, vec_ref, nm_ref, bg_ref,
                wgate_ref, wpa_ref, wpb_ref, wom_ref, nx_ref, wq_ref, wxo_ref, out_ref):
    x = x_ref[0]
    h = _rms(x, nm_ref[...]).astype(BF16)
    sg = _sigmoid(jnp.dot(h, wgate_ref[...], preferred_element_type=F32) + bg_ref[...])

    y = y_ref[0]
    yc = y - _head_sums(y) * (1.0 / RWKV_HEAD)
    var = _head_sums(yc * yc) * (1.0 / RWKV_HEAD)
    yn = yc * lax.rsqrt(var + LN_X_EPS)
    oa = (yn * vec_ref[ROW_LNW:ROW_LNW + 1, :] + bonus_ref[0]) * g_ref[0]

    merged = (sg[:, :D_MODEL] * _bdot(oa, wpa_ref[...])
              + sg[:, D_MODEL:] * jnp.dot(ob_ref[0], wpb_ref[...], preferred_element_type=F32))
    x1 = x + _bdot(merged, wom_ref[...])

    q = _bdot(_rms(x1, nx_ref[...]), wq_ref[...]) * (XHEAD_DIM ** -0.5)
    heads = []
    for hh in range(N_XHEADS):
        cols = slice(hh * XHEAD_DIM, (hh + 1) * XHEAD_DIM)
        sc = _bdot_nt(q[:, cols], k_ref[0, :, cols])
        e = jnp.exp(sc - jnp.max(sc, axis=-1, keepdims=True))
        pr = e / jnp.sum(e, axis=-1, keepdims=True)
        heads.append(_bdot(pr, v_ref[0, :, cols]))
    o = jnp.concatenate(heads, axis=-1)
    out_ref[0] = x1 + _bdot(o, wxo_ref[...])


def _mix(x, y, g, bonus, ob, kmem, vmem, vec, nm, bg, wgate, wpa, wpb, wom, nx, wq, wxo, tile):
    b, s, d = x.shape
    n_mem = kmem.shape[1]
    const = _const_spec
    tok = lambda w: pl.BlockSpec((1, tile, w), lambda i, j: (i, j, 0))
    memspec = pl.BlockSpec((1, n_mem, d), lambda i, j: (i, 0, 0))
    return pl.pallas_call(
        _mix_kernel,
        grid=(b, s // tile),
        in_specs=[tok(d), tok(D_RWKV), tok(D_RWKV), tok(D_RWKV), tok(D_CONV), memspec, memspec,
                  const(VEC_ROWS, D_RWKV), const(1, d), const(1, 2 * d), const(d, 2 * d),
                  const(D_RWKV, d), const(D_CONV, d), const(d, d), const(1, d), const(d, d),
                  const(d, d)],
        out_specs=tok(d),
        out_shape=jax.ShapeDtypeStruct((b, s, d), F32),
        compiler_params=pltpu.CompilerParams(dimension_semantics=("arbitrary", "arbitrary"),
                                             vmem_limit_bytes=VMEM_LIMIT),
        name="mix",
    )(x, y, g, bonus, ob, kmem, vmem, vec, nm, bg, wgate, wpa, wpb, wom, nx, wq, wxo)


def _mlp_kernel(x_ref, nm_ref, wup_ref, wdown_ref, nf_ref, out_ref):
    x = x_ref[...]
    h = _rms(x, nm_ref[...]).astype(BF16)
    acc = x
    for c in range(D_FF // D_MODEL):
        cols = slice(c * D_MODEL, (c + 1) * D_MODEL)
        up = jnp.maximum(jnp.dot(h, wup_ref[:, cols], preferred_element_type=F32), 0.0)
        acc = acc + _bdot(up * up, wdown_ref[cols, :])
    out_ref[...] = _rms(acc, nf_ref[...])


def _mlp(x, nm, wup, wdown, nf, tile):
    n, d = x.shape
    const = _const_spec
    tok = pl.BlockSpec((tile, d), lambda i: (i, 0))
    return pl.pallas_call(
        _mlp_kernel,
        grid=(n // tile,),
        in_specs=[tok, const(1, d), const(d, D_FF), const(D_FF, d), const(1, d)],
        out_specs=tok,
        out_shape=jax.ShapeDtypeStruct((n, d), F32),
        compiler_params=pltpu.CompilerParams(dimension_semantics=("arbitrary",),
                                             vmem_limit_bytes=VMEM_LIMIT),
        name="mlp",
    )(x, nm, wup, wdown, nf)


def _pad_cols(w, width):
    return jnp.pad(w, ((0, 0), (0, width - w.shape[1])))


def _pad_rows(w, height):
    return jnp.pad(w, ((0, height - w.shape[0]), (0, 0)))


def _layer(x, mem, norm_mix, w_in, b_gate, mu_shift, w0, w_lora_w, a0, w_lora_a, w_lora_g,
           k_k, k_a, r_k, ln_x_w, ln_x_b, conv_w, w_proj_a, w_proj_b, w_out_mix, norm_xattn,
           norm_mem, w_q, w_kv, w_xo, norm_mlp, w_up, w_down, norm_final):
    bsz, s, d = x.shape
    c_conv = RW_COLS - 2 * LANES + GATE_LORA
    c_gate = c_conv + 3 * D_CONV
    w_in = w_in.astype(BF16)
    wconv = w_in[:, c_conv:c_gate]
    wgate = w_in[:, c_gate:]
    mu = _pad_cols(mu_shift[None], RW_COLS)
    lora = [jnp.pad(w, ((top, rows - top - w.shape[0]), (0, 0))).astype(BF16)
            for w, top, rows in ((w_lora_w, 0, LANES), (w_lora_a, DECAY_LORA, LANES),
                                 (w_lora_g, 0, 2 * LANES))]
    vec = jnp.stack([w0, a0, k_k, k_a, r_k.reshape(-1), ln_x_b, ln_x_w,
                     conv_w[0, 0], conv_w[1, 0], conv_w[2, 0]])
    vec = _pad_rows(vec, VEC_ROWS)
    step = jnp.arange(CHUNK)
    tri = (step[None, :] <= step[:, None]).astype(BF16)

    rt, at, bt, kt, v, wc, g, bonus, ob = _inproj(
        x, norm_mix[None], w_in, wconv, mu, vec, lora, tri, tile=TILE_INPROJ)
    later = (w_proj_a, w_proj_b, w_out_mix, w_q, w_xo, w_up, w_down, w_kv)
    y, (wpa, wpb, wom, wq, wxo, wup, wdown, wkv) = _wkv(
        rt, at, bt, kt, v, wc, later, tile=TILE_WKV)
    kmem, vmem = _memkv(mem, norm_mem[None], wkv)
    x = _mix(x, y, g, bonus, ob, kmem, vmem, vec, norm_mix[None], b_gate[None], wgate,
             wpa, wpb, wom, norm_xattn[None], wq, wxo, tile=TILE_MIX)
    return _mlp(x.reshape(bsz * s, d), norm_mlp[None], wup, wdown, norm_final[None],
                tile=TILE_MLP).reshape(bsz, s, d)


def kernel(x, mem, norm_mix, w_in, b_gate, mu_shift, w0, w_lora_w, a0, w_lora_a, w_lora_g, k_k, k_a, r_k, ln_x_w, ln_x_b, conv_w, w_proj_a, w_proj_b, w_out_mix, norm_xattn, norm_mem, w_q, w_kv, w_xo, norm_mlp, w_up, w_down, norm_final):
    assert w_in.shape[0] == 1, "the MLP kernel fuses the final norm: single-layer trunk only"
    per_layer = (norm_mix, w_in, b_gate, mu_shift, w0, w_lora_w, a0, w_lora_a, w_lora_g, k_k, k_a,
                 r_k, ln_x_w, ln_x_b, conv_w, w_proj_a, w_proj_b, w_out_mix, norm_xattn, norm_mem,
                 w_q, w_kv, w_xo, norm_mlp, w_up, w_down)
    return _layer(x, mem, *(p[0] for p in per_layer), norm_final)
```

```python
import functools
import itertools
import math

import jax
import jax.numpy as jnp
from jax import lax
from jax.experimental import pallas as pl
from jax.experimental.pallas import tpu as pltpu

F32 = jnp.float32
BF16 = jnp.bfloat16

D_MODEL = 1024
D_RWKV = 512
RWKV_HEAD = 64
DECAY_LORA = 64
AAA_LORA = 64
GATE_LORA = 160
LN_X_EPS = 64e-5
D_CONV = 512
N_XHEADS = 4
XHEAD_DIM = D_MODEL // N_XHEADS
D_FF = 4 * D_MODEL
RMS_EPS = 1e-6
EXP_MINUS_HALF = math.exp(-0.5)

LANES = 128
SUBLANES = 8
CHUNK = 64
PAIR = 2 * RWKV_HEAD
N_PAIRS = D_RWKV // PAIR
WA_OFF = 3 * D_RWKV
GD_OFF = WA_OFF + DECAY_LORA + AAA_LORA
RW_COLS = GD_OFF + 2 * LANES
assert DECAY_LORA + AAA_LORA == LANES and GATE_LORA <= 2 * LANES
VMEM_LIMIT = 56 * 1024 * 1024

ROW_W0, ROW_A0, ROW_KK, ROW_KA, ROW_RK, ROW_LNB, ROW_LNW, ROW_CONV = 0, 1, 2, 3, 4, 5, 6, 7
VEC_ROWS = 16

TILE_SPLIT = 128
TILE_INPROJ = 1024
TILE_WKV = 8 * CHUNK
TILE_MIX = 1024
TILE_MLP = 1024
PREP_STAGES_PER_CHAIN_STAGE = 3


def _bdot(a, b):
    return jnp.dot(a.astype(BF16), b.astype(BF16), preferred_element_type=F32)


def _bdot_nt(a, b):
    return lax.dot_general(a.astype(BF16), b.astype(BF16), (((1,), (1,)), ((), ())),
                           preferred_element_type=F32)


def _split2(a):
    hi = a.astype(BF16)
    lo = (a - hi.astype(F32)).astype(BF16)
    return hi, lo


def _head_sums(a):
    low = lax.broadcasted_iota(jnp.int32, (a.shape[0], LANES), 1) < RWKV_HEAD
    out = []
    for j in range(a.shape[1] // LANES):
        slab = a[:, j * LANES:(j + 1) * LANES]
        s_lo = jnp.sum(jnp.where(low, slab, 0.0), axis=-1, keepdims=True)
        s_hi = jnp.sum(jnp.where(low, 0.0, slab), axis=-1, keepdims=True)
        out.append(jnp.where(low, s_lo, s_hi))
    return jnp.concatenate(out, axis=-1)


def _dot_sel_lhs(sel, a):
    hi, lo = _split2(a)
    d = functools.partial(jnp.dot, preferred_element_type=F32)
    return d(sel, hi) + d(sel, lo)


def _rms(x, g):
    return x * lax.rsqrt(jnp.mean(x * x, axis=-1, keepdims=True) + RMS_EPS) * g


def _sigmoid(x):
    return 1.0 / (1.0 + jnp.exp(-x))


def _shift_rows(x, carry_rows, n):
    nc = carry_rows.shape[0]
    out = pltpu.roll(x, n, 0)
    rows = lax.broadcasted_iota(jnp.int32, (nc, 1), 0)
    head = out[:nc]
    for j in range(n):
        head = jnp.where(rows == j, carry_rows[nc - n + j:nc - n + j + 1, :], head)
    return jnp.concatenate([head, out[nc:]], axis=0)


def _const_spec(*shape):
    return pl.BlockSpec(shape, lambda *_: (0,) * len(shape), pipeline_mode=pl.Buffered(1))


def _memkv_kernel(mem_ref, g_ref, w_ref, k_ref, v_ref):
    m = _rms(mem_ref[0], g_ref[...])
    kv = _bdot(m, w_ref[...])
    k_ref[0] = kv[:, :D_MODEL].astype(BF16)
    v_ref[0] = kv[:, D_MODEL:].astype(BF16)


def _memkv(mem, g, w_kv):
    b, n, d = mem.shape
    return pl.pallas_call(
        _memkv_kernel,
        grid=(b,),
        in_specs=[pl.BlockSpec((1, n, d), lambda i: (i, 0, 0)),
                  pl.BlockSpec((1, d), lambda i: (0, 0)),
                  pl.BlockSpec((d, 2 * d), lambda i: (0, 0))],
        out_specs=[pl.BlockSpec((1, n, d), lambda i: (i, 0, 0))] * 2,
        out_shape=[jax.ShapeDtypeStruct((b, n, d), BF16)] * 2,
        compiler_params=pltpu.CompilerParams(dimension_semantics=("arbitrary",),
                                             vmem_limit_bytes=VMEM_LIMIT),
        name="memkv",
    )(mem, g, w_kv)


def _inproj_kernel(x_ref, nm_ref, wrw_ref, wconv_ref, mu_ref, vec_ref,
                   ww_ref, wa_ref, wg_ref, tri_ref,
                   rt_ref, at_ref, bt_ref, kt_ref, v_ref, wc_ref,
                   g_ref, bonus_ref, ob_ref,
                   pcarry, ucarry):
    @pl.when(pl.program_id(1) == 0)
    def _():
        pcarry[...] = jnp.zeros_like(pcarry)
        ucarry[...] = jnp.zeros_like(ucarry)

    t = x_ref.shape[1]
    h = _rms(x_ref[0], nm_ref[...]).astype(BF16)

    pc = jnp.dot(h, wconv_ref[...], preferred_element_type=F32)
    u = pc[:, D_CONV:2 * D_CONV] * pc[:, 2 * D_CONV:]
    uc = ucarry[...]
    conv = (vec_ref[ROW_CONV:ROW_CONV + 1, :] * _shift_rows(u, uc, 2)
            + vec_ref[ROW_CONV + 1:ROW_CONV + 2, :] * _shift_rows(u, uc, 1)
            + vec_ref[ROW_CONV + 2:ROW_CONV + 3, :] * u)
    ob_ref[0] = (pc[:, :D_CONV] * conv).astype(BF16)
    ucarry[...] = u[t - SUBLANES:, :]

    p = jnp.dot(h, wrw_ref[...], preferred_element_type=F32)
    ps = _shift_rows(p, pcarry[...], 1)
    pcarry[...] = p[t - SUBLANES:, :]
    xm = p + (ps - p) * mu_ref[...]
    r = xm[:, :D_RWKV]
    k = xm[:, D_RWKV:2 * D_RWKV]
    v = xm[:, 2 * D_RWKV:3 * D_RWKV]
    wa = xm[:, WA_OFF:GD_OFF]
    gd = xm[:, GD_OFF:]

    def vec(row):
        return vec_ref[row:row + 1, :]

    ld = -EXP_MINUS_HALF * _sigmoid(vec(ROW_W0) + _bdot(jnp.tanh(wa), ww_ref[...]))
    a = _sigmoid(vec(ROW_A0) + _bdot(wa, wa_ref[...]))
    g_ref[0] = _bdot(_sigmoid(gd), wg_ref[...])
    kk = k * vec(ROW_KK)
    kk = kk * lax.rsqrt(jnp.maximum(_head_sums(kk * kk), 1e-24))
    k2 = k * (1.0 + (a - 1.0) * vec(ROW_KA))
    bonus_ref[0] = _head_sums(r * k2 * vec(ROW_RK)) * v + vec(ROW_LNB)
    v_ref[0] = v.astype(BF16)
    na = -kk
    nb = kk * a

    tri = tri_ref[...]
    for c in range(t // CHUNK):
        sl = slice(c * CHUNK, (c + 1) * CHUNK)
        ldc = ld[sl]
        cum = _dot_sel_lhs(tri, ldc)
        tot = cum[CHUNK - 1:CHUNK, :]
        e_inc = jnp.exp(cum)
        e_inv = 1.0 / e_inc
        e_prev = jnp.exp(cum - ldc)
        rt_ref[0, sl, :] = (r[sl] * e_inc).astype(BF16)
        at_ref[0, sl, :] = (na[sl] * e_prev).astype(BF16)
        bt_ref[0, sl, :] = (nb[sl] * e_inv).astype(BF16)
        kt_ref[0, sl, :] = (k2[sl] * e_inv).astype(BF16)
        wc_ref[0, c] = jnp.exp(tot)


def _inproj(x, nm, wrw, wconv, mu, vec, lora, tri, tile):
    b, s, d = x.shape
    nt = s // tile
    const = _const_spec
    tok = lambda w: pl.BlockSpec((1, tile, w), lambda i, j: (i, j, 0))
    outs = ([jax.ShapeDtypeStruct((b, s, D_RWKV), BF16)] * 5
            + [jax.ShapeDtypeStruct((b, s // CHUNK, 1, D_RWKV), F32)]
            + [jax.ShapeDtypeStruct((b, s, D_RWKV), F32)] * 2
            + [jax.ShapeDtypeStruct((b, s, D_CONV), BF16)])
    out_specs = ([tok(D_RWKV)] * 5
                 + [pl.BlockSpec((1, tile // CHUNK, 1, D_RWKV), lambda i, j: (i, j, 0, 0))]
                 + [tok(D_RWKV)] * 2 + [tok(D_CONV)])
    return pl.pallas_call(
        _inproj_kernel,
        grid=(b, nt),
        in_specs=[tok(d), const(1, d), const(d, RW_COLS), const(d, 3 * D_CONV),
                  const(1, RW_COLS), const(VEC_ROWS, D_RWKV)]
                 + [const(*w.shape) for w in lora]
                 + [const(CHUNK, CHUNK)],
        out_specs=out_specs,
        out_shape=outs,
        scratch_shapes=[pltpu.VMEM((SUBLANES, RW_COLS), F32), pltpu.VMEM((SUBLANES, D_CONV), F32)],
        compiler_params=pltpu.CompilerParams(dimension_semantics=("arbitrary", "arbitrary"),
                                             vmem_limit_bytes=VMEM_LIMIT),
        name="inproj",
    )(x, nm, wrw, wconv, mu, vec, *lora, tri)


def _pair_expand(x):
    even = lax.broadcasted_iota(jnp.int32, x.shape, 1) < RWKV_HEAD
    zero = jnp.zeros_like(x)
    return jnp.concatenate([jnp.where(even, x, zero), jnp.where(even, zero, x)], axis=0)


def _rows(*parts):
    return jnp.concatenate(parts, axis=0)


def _wkv_masks():
    row = lax.broadcasted_iota(jnp.int32, (CHUNK, PAIR), 0)
    col = lax.broadcasted_iota(jnp.int32, (CHUNK, PAIR), 1) % RWKV_HEAD
    row2 = lax.broadcasted_iota(jnp.int32, (PAIR, PAIR), 0) < RWKV_HEAD
    col2 = lax.broadcasted_iota(jnp.int32, (PAIR, PAIR), 1) < RWKV_HEAD
    return dict(strict=col < row, incl=col <= row, eye=(col == row).astype(F32),
                same_head=row2 == col2)


def _wkv_prep(refs, wc_ref, c, inst, m, out):
    rows = slice(c * CHUNK, (c + 1) * CHUNK)
    load = lambda ref: [ref[b, rows, p * PAIR:(p + 1) * PAIR] for b, p in inst]
    r, a, bt, kt, v = (load(ref) for ref in refs)
    n = range(len(inst))
    ex = lambda x: _pair_expand(x.astype(BF16))
    g = [_bdot_nt(_rows(a[i], r[i]), _rows(_pair_expand(bt[i]), _pair_expand(kt[i]))) for i in n]
    yield
    lab = [jnp.where(m["strict"], g[i][:CHUNK, :PAIR], 0.0) for i in n]
    lak = [jnp.where(m["strict"], g[i][:CHUNK, PAIR:], 0.0) for i in n]
    mrb = [jnp.where(m["incl"], g[i][CHUNK:, :PAIR], 0.0) for i in n]
    mrk = [jnp.where(m["incl"], g[i][CHUNK:, PAIR:], 0.0) for i in n]
    npow = [_bdot(lab[i], ex(lab[i])) for i in n]
    tinv = [m["eye"] + lab[i] for i in n]
    lmv = [_bdot(_rows(lak[i], mrk[i]), _pair_expand(v[i])) for i in n]
    yield
    for _ in range(4):
        prod = [_bdot(_rows(tinv[i], npow[i]), ex(npow[i])) for i in n]
        tinv = [tinv[i] + prod[i][:CHUNK] for i in n]
        npow = [prod[i][CHUNK:] for i in n]
        yield
    tinv = [tinv[i] + _bdot(tinv[i], ex(npow[i])) for i in n]
    yield
    pq = [_bdot(tinv[i], jnp.concatenate([_pair_expand(a[i]), ex(lmv[i][:CHUNK])], axis=1))
          for i in n]
    bkt = [_rows(bt[i], kt[i]).astype(F32).T.astype(BF16) for i in n]
    wcol = [jnp.broadcast_to(wc_ref[b, c, :, p * PAIR:(p + 1) * PAIR], (PAIR, PAIR)).T
            for b, p in inst]
    out.update(p=[pq[i][:, :PAIR].astype(BF16) for i in n], q=[pq[i][:, PAIR:] for i in n],
               r=r, v=v, bkt=bkt, wcol=wcol, mrb=mrb, mv=[lmv[i][CHUNK:] for i in n])
    yield


def _wkv_chain(pre, z, y_ref, c, inst, m):
    rows = slice(c * CHUNK, (c + 1) * CHUNK)
    n = range(len(inst))
    pr = [_bdot(_rows(pre["p"][i], pre["r"][i]), z[i]) for i in n]
    yield
    u = [(pr[i][:CHUNK] + pre["q"][i]).astype(BF16) for i in n]
    for i in n:
        z[i] = pre["wcol"][i] * (z[i] + jnp.where(
            m["same_head"], _bdot(pre["bkt"][i], _rows(u[i], pre["v"][i])), 0.0))
    yield
    for i, (b, p) in enumerate(inst):
        y_ref[b, rows, p * PAIR:(p + 1) * PAIR] = (
            pr[i][CHUNK:] + _bdot(pre["mrb"][i], _pair_expand(u[i])) + pre["mv"][i])
    yield


def _interleave(primary, secondary, ratio):
    for _ in primary:
        for _ in itertools.islice(secondary, ratio):
            pass
    for _ in secondary:
        pass


def _cast_stages(src_refs, dst_refs):
    for src, dst in zip(src_refs, dst_refs):
        dst[...] = src[...].astype(dst.dtype)
        yield


def _wkv_kernel(rt_ref, at_ref, bt_ref, kt_ref, v_ref, wc_ref, *rest):
    nw = (len(rest) - 2) // 2
    w_refs, y_ref, wb_refs, state = rest[:nw], rest[nw], rest[nw + 1:-1], rest[-1]

    @pl.when(pl.program_id(0) == 0)
    def _():
        state[...] = jnp.zeros_like(state)

    casts = _cast_stages(w_refs, wb_refs)
    nb, t, _ = rt_ref.shape
    nc = t // CHUNK
    inst = [(b, p) for b in range(nb) for p in range(N_PAIRS)]
    m = _wkv_masks()
    refs = (rt_ref, at_ref, bt_ref, kt_ref, v_ref)
    z = [state[b, p] for b, p in inst]
    pre = [dict() for _ in range(nc)]
    for _ in _wkv_prep(refs, wc_ref, 0, inst, m, pre[0]):
        pass
    for c in range(nc):
        nxt = _wkv_prep(refs, wc_ref, c + 1, inst, m, pre[c + 1]) if c + 1 < nc else iter(())
        _interleave(_wkv_chain(pre[c], z, y_ref, c, inst, m), nxt, PREP_STAGES_PER_CHAIN_STAGE)
        next(casts, None)
    for _ in casts:
        pass
    for i, (b, p) in enumerate(inst):
        state[b, p] = z[i]


def _wkv(rt, at, bt, kt, v, wc, weights, tile):
    b, s, d = rt.shape
    steps = s // tile
    tok = pl.BlockSpec((b, tile, d), lambda j: (0, j, 0))
    wspecs = [pl.BlockSpec((w.shape[0] // steps, w.shape[1]), lambda j: (j, 0)) for w in weights]
    out = pl.pallas_call(
        _wkv_kernel,
        grid=(steps,),
        in_specs=[tok] * 5 + [pl.BlockSpec((b, tile // CHUNK, 1, d), lambda j: (0, j, 0, 0))] + wspecs,
        out_specs=[tok] + wspecs,
        out_shape=[jax.ShapeDtypeStruct((b, s, d), F32)]
                  + [jax.ShapeDtypeStruct(w.shape, BF16) for w in weights],
        scratch_shapes=[pltpu.VMEM((b, N_PAIRS, PAIR, PAIR), F32)],
        compiler_params=pltpu.CompilerParams(dimension_semantics=("arbitrary",),
                                             vmem_limit_bytes=VMEM_LIMIT),
        name="wkv",
    )(rt, at, bt, kt, v, wc, *weights)
    return out[0], out[1:]


def _mix_kernel(x_ref, y_ref, g_ref, bonus_ref, ob_ref, k_ref, v_ref, vec_ref, nm_ref, bg_ref,
                wgate_ref, wpa_ref, wpb_ref, wom_ref, nx_ref, wq_ref, wxo_ref, out_ref):
    x = x_ref[0]
    h = _rms(x, nm_ref[...]).astype(BF16)
    sg = _sigmoid(jnp.dot(h, wgate_ref[...], preferred_element_type=F32) + bg_ref[...])

    y = y_ref[0]
    yc = y - _head_sums(y) * (1.0 / RWKV_HEAD)
    var = _head_sums(yc * yc) * (1.0 / RWKV_HEAD)
    yn = yc * lax.rsqrt(var + LN_X_EPS)
    oa = (yn * vec_ref[ROW_LNW:ROW_LNW + 1, :] + bonus_ref[0]) * g_ref[0]

    merged = (sg[:, :D_MODEL] * _bdot(oa, wpa_ref[...])
              + sg[:, D_MODEL:] * jnp.dot(ob_ref[0], wpb_ref[...], preferred_element_type=F32))
    x1 = x + _bdot(merged, wom_ref[...])

    q = _bdot(_rms(x1, nx_ref[...]), wq_ref[...]) * (XHEAD_DIM ** -0.5)
    heads = []
    for hh in range(N_XHEADS):
        cols = slice(hh * XHEAD_DIM, (hh + 1) * XHEAD_DIM)
        sc = _bdot_nt(q[:, cols], k_ref[0, :, cols])
        e = jnp.exp(sc - jnp.max(sc, axis=-1, keepdims=True))
        pr = e / jnp.sum(e, axis=-1, keepdims=True)
        heads.append(_bdot(pr, v_ref[0, :, cols]))
    o = jnp.concatenate(heads, axis=-1)
    out_ref[0] = x1 + _bdot(o, wxo_ref[...])


def _mix(x, y, g, bonus, ob, kmem, vmem, vec, nm, bg, wgate, wpa, wpb, wom, nx, wq, wxo, tile):
    b, s, d = x.shape
    n_mem = kmem.shape[1]
    const = _const_spec
    tok = lambda w: pl.BlockSpec((1, tile, w), lambda i, j: (i, j, 0))
    memspec = pl.BlockSpec((1, n_mem, d), lambda i, j: (i, 0, 0))
    return pl.pallas_call(
        _mix_kernel,
        grid=(b, s // tile),
        in_specs=[tok(d), tok(D_RWKV), tok(D_RWKV), tok(D_RWKV), tok(D_CONV), memspec, memspec,
                  const(VEC_ROWS, D_RWKV), const(1, d), const(1, 2 * d), const(d, 2 * d),
                  const(D_RWKV, d), const(D_CONV, d), const(d, d), const(1, d), const(d, d),
                  const(d, d)],
        out_specs=tok(d),
        out_shape=jax.ShapeDtypeStruct((b, s, d), F32),
        compiler_params=pltpu.CompilerParams(dimension_semantics=("arbitrary", "arbitrary"),
                                             vmem_limit_bytes=VMEM_LIMIT),
        name="mix",
    )(x, y, g, bonus, ob, kmem, vmem, vec, nm, bg, wgate, wpa, wpb, wom, nx, wq, wxo)


def _mlp_kernel(x_ref, nm_ref, wup_ref, wdown_ref, nf_ref, out_ref):
    x = x_ref[...]
    h = _rms(x, nm_ref[...]).astype(BF16)
    acc = x
    for c in range(D_FF // D_MODEL):
        cols = slice(c * D_MODEL, (c + 1) * D_MODEL)
        up = jnp.maximum(jnp.dot(h, wup_ref[:, cols], preferred_element_type=F32), 0.0)
        acc = acc + _bdot(up * up, wdown_ref[cols, :])
    out_ref[...] = _rms(acc, nf_ref[...])


def _mlp(x, nm, wup, wdown, nf, tile):
    n, d = x.shape
    const = _const_spec
    tok = pl.BlockSpec((tile, d), lambda i: (i, 0))
    return pl.pallas_call(
        _mlp_kernel,
        grid=(n // tile,),
        in_specs=[tok, const(1, d), const(d, D_FF), const(D_FF, d), const(1, d)],
        out_specs=tok,
        out_shape=jax.ShapeDtypeStruct((n, d), F32),
        compiler_params=pltpu.CompilerParams(dimension_semantics=("arbitrary",),
                                             vmem_limit_bytes=VMEM_LIMIT),
        name="mlp",
    )(x, nm, wup, wdown, nf)


CONV_OFF = GD_OFF + GATE_LORA
GATE_OFF = CONV_OFF + 3 * D_CONV


def _split_w_in_kernel(w_ref, wrw_ref, wconv_ref, wgate_ref):
    wrw_ref[...] = w_ref[:, :RW_COLS].astype(BF16)
    wconv_ref[...] = w_ref[:, CONV_OFF:GATE_OFF].astype(BF16)
    wgate_ref[...] = w_ref[:, GATE_OFF:].astype(BF16)


def _split_w_in(w_in, rows):
    d, cols = w_in.shape
    widths = (RW_COLS, 3 * D_CONV, cols - GATE_OFF)
    return pl.pallas_call(
        _split_w_in_kernel,
        grid=(d // rows,),
        in_specs=[pl.BlockSpec((rows, cols), lambda i: (i, 0))],
        out_specs=[pl.BlockSpec((rows, w), lambda i: (i, 0)) for w in widths],
        out_shape=[jax.ShapeDtypeStruct((d, w), BF16) for w in widths],
        compiler_params=pltpu.CompilerParams(dimension_semantics=("arbitrary",),
                                             vmem_limit_bytes=VMEM_LIMIT),
        name="split_w_in",
    )(w_in)


def _pad_cols(w, width):
    return jnp.pad(w, ((0, 0), (0, width - w.shape[1])))


def _pad_rows(w, height):
    return jnp.pad(w, ((0, height - w.shape[0]), (0, 0)))


def _layer(x, mem, norm_mix, w_in, b_gate, mu_shift, w0, w_lora_w, a0, w_lora_a, w_lora_g,
           k_k, k_a, r_k, ln_x_w, ln_x_b, conv_w, w_proj_a, w_proj_b, w_out_mix, norm_xattn,
           norm_mem, w_q, w_kv, w_xo, norm_mlp, w_up, w_down, norm_final):
    bsz, s, d = x.shape
    wrw, wconv, wgate = _split_w_in(w_in, rows=TILE_SPLIT)
    mu = _pad_cols(mu_shift[None], RW_COLS)
    lora = [jnp.pad(w, ((top, rows - top - w.shape[0]), (0, 0))).astype(BF16)
            for w, top, rows in ((w_lora_w, 0, LANES), (w_lora_a, DECAY_LORA, LANES),
                                 (w_lora_g, 0, 2 * LANES))]
    vec = jnp.stack([w0, a0, k_k, k_a, r_k.reshape(-1), ln_x_b, ln_x_w,
                     conv_w[0, 0], conv_w[1, 0], conv_w[2, 0]])
    vec = _pad_rows(vec, VEC_ROWS)
    step = jnp.arange(CHUNK)
    tri = (step[None, :] <= step[:, None]).astype(BF16)

    rt, at, bt, kt, v, wc, g, bonus, ob = _inproj(
        x, norm_mix[None], wrw, wconv, mu, vec, lora, tri, tile=TILE_INPROJ)
    later = (w_proj_a, w_proj_b, w_out_mix, w_q, w_xo, w_up, w_down, w_kv)
    y, (wpa, wpb, wom, wq, wxo, wup, wdown, wkv) = _wkv(
        rt, at, bt, kt, v, wc, later, tile=TILE_WKV)
    kmem, vmem = _memkv(mem, norm_mem[None], wkv)
    x = _mix(x, y, g, bonus, ob, kmem, vmem, vec, norm_mix[None], b_gate[None], wgate,
             wpa, wpb, wom, norm_xattn[None], wq, wxo, tile=TILE_MIX)
    return _mlp(x.reshape(bsz * s, d), norm_mlp[None], wup, wdown, norm_final[None],
                tile=TILE_MLP).reshape(bsz, s, d)


def kernel(x, mem, norm_mix, w_in, b_gate, mu_shift, w0, w_lora_w, a0, w_lora_a, w_lora_g, k_k, k_a, r_k, ln_x_w, ln_x_b, conv_w, w_proj_a, w_proj_b, w_out_mix, norm_xattn, norm_mem, w_q, w_kv, w_xo, norm_mlp, w_up, w_down, norm_final):
    assert w_in.shape[0] == 1, "the MLP kernel fuses the final norm: single-layer trunk only"
    per_layer = (norm_mix, w_in, b_gate, mu_shift, w0, w_lora_w, a0, w_lora_a, w_lora_g, k_k, k_a,
                 r_k, ln_x_w, ln_x_b, conv_w, w_proj_a, w_proj_b, w_out_mix, norm_xattn, norm_mem,
                 w_q, w_kv, w_xo, norm_mlp, w_up, w_down)
    return _layer(x, mem, *(p[0] for p in per_layer), norm_final)
```

```python
import functools
import itertools
import math

import jax
import jax.numpy as jnp
from jax import lax
from jax.experimental import pallas as pl
from jax.experimental.pallas import tpu as pltpu

F32 = jnp.float32
BF16 = jnp.bfloat16

D_MODEL = 1024
D_RWKV = 512
RWKV_HEAD = 64
DECAY_LORA = 64
AAA_LORA = 64
GATE_LORA = 160
LN_X_EPS = 64e-5
D_CONV = 512
N_XHEADS = 4
XHEAD_DIM = D_MODEL // N_XHEADS
D_FF = 4 * D_MODEL
RMS_EPS = 1e-6
EXP_MINUS_HALF = math.exp(-0.5)

LANES = 128
SUBLANES = 8
CHUNK = 64
PAIR = 2 * RWKV_HEAD
N_PAIRS = D_RWKV // PAIR
WA_OFF = 3 * D_RWKV
GD_OFF = WA_OFF + DECAY_LORA + AAA_LORA
RW_COLS = GD_OFF + 2 * LANES
assert DECAY_LORA + AAA_LORA == LANES and GATE_LORA <= 2 * LANES
MIB = 1024 * 1024
VMEM_LIMIT = {"memkv": 12 * MIB, "inproj": 54 * MIB, "wkv": 32 * MIB, "mix": 56 * MIB,
              "mlp": 44 * MIB}

ROW_W0, ROW_A0, ROW_KK, ROW_KA, ROW_RK, ROW_LNB, ROW_LNW, ROW_CONV = 0, 1, 2, 3, 4, 5, 6, 7
VEC_ROWS = 16

TILE_INPROJ = 1024
TILE_WKV = 8 * CHUNK
TILE_MIX = 1024
TILE_MLP = 1024
PREP_STAGES_PER_CHAIN_STAGE = 3


def _bdot(a, b):
    return jnp.dot(a.astype(BF16), b.astype(BF16), preferred_element_type=F32)


def _bdot_nt(a, b):
    return lax.dot_general(a.astype(BF16), b.astype(BF16), (((1,), (1,)), ((), ())),
                           preferred_element_type=F32)


def _split2(a):
    hi = a.astype(BF16)
    lo = (a - hi.astype(F32)).astype(BF16)
    return hi, lo


def _head_sums(a):
    low = lax.broadcasted_iota(jnp.int32, (a.shape[0], LANES), 1) < RWKV_HEAD
    out = []
    for j in range(a.shape[1] // LANES):
        slab = a[:, j * LANES:(j + 1) * LANES]
        s_lo = jnp.sum(jnp.where(low, slab, 0.0), axis=-1, keepdims=True)
        s_hi = jnp.sum(jnp.where(low, 0.0, slab), axis=-1, keepdims=True)
        out.append(jnp.where(low, s_lo, s_hi))
    return jnp.concatenate(out, axis=-1)


def _dot_sel_lhs(sel, a):
    hi, lo = _split2(a)
    d = functools.partial(jnp.dot, preferred_element_type=F32)
    return d(sel, hi) + d(sel, lo)


def _rms(x, g):
    return x * lax.rsqrt(jnp.mean(x * x, axis=-1, keepdims=True) + RMS_EPS) * g


def _sigmoid(x):
    return 1.0 / (1.0 + jnp.exp(-x))


def _shift_rows(x, carry_rows, n):
    nc = carry_rows.shape[0]
    out = pltpu.roll(x, n, 0)
    rows = lax.broadcasted_iota(jnp.int32, (nc, 1), 0)
    head = out[:nc]
    for j in range(n):
        head = jnp.where(rows == j, carry_rows[nc - n + j:nc - n + j + 1, :], head)
    return jnp.concatenate([head, out[nc:]], axis=0)


def _const_spec(*shape):
    return pl.BlockSpec(shape, lambda *_: (0,) * len(shape), pipeline_mode=pl.Buffered(1))


def _memkv_kernel(mem_ref, g_ref, w_ref, k_ref, v_ref):
    m = _rms(mem_ref[0], g_ref[...])
    kv = _bdot(m, w_ref[...])
    k_ref[0] = kv[:, :D_MODEL].astype(BF16)
    v_ref[0] = kv[:, D_MODEL:].astype(BF16)


def _memkv(mem, g, w_kv):
    b, n, d = mem.shape
    return pl.pallas_call(
        _memkv_kernel,
        grid=(b,),
        in_specs=[pl.BlockSpec((1, n, d), lambda i: (i, 0, 0)),
                  pl.BlockSpec((1, d), lambda i: (0, 0)),
                  pl.BlockSpec((d, 2 * d), lambda i: (0, 0))],
        out_specs=[pl.BlockSpec((1, n, d), lambda i: (i, 0, 0))] * 2,
        out_shape=[jax.ShapeDtypeStruct((b, n, d), BF16)] * 2,
        compiler_params=pltpu.CompilerParams(dimension_semantics=("arbitrary",),
                                             vmem_limit_bytes=VMEM_LIMIT["memkv"]),
        name="memkv",
    )(mem, g, w_kv)


def _inproj_kernel(x_ref, nm_ref, wrw_ref, wconv_ref, mu_ref, vec_ref,
                   ww_ref, wa_ref, wg_ref, tri_ref,
                   rt_ref, at_ref, bt_ref, kt_ref, v_ref, wc_ref,
                   g_ref, bonus_ref, ob_ref,
                   pcarry, ucarry):
    @pl.when(pl.program_id(1) == 0)
    def _():
        pcarry[...] = jnp.zeros_like(pcarry)
        ucarry[...] = jnp.zeros_like(ucarry)

    t = x_ref.shape[1]
    h = _rms(x_ref[0], nm_ref[...]).astype(BF16)

    pc = jnp.dot(h, wconv_ref[...], preferred_element_type=F32)
    u = pc[:, D_CONV:2 * D_CONV] * pc[:, 2 * D_CONV:]
    uc = ucarry[...]
    conv = (vec_ref[ROW_CONV:ROW_CONV + 1, :] * _shift_rows(u, uc, 2)
            + vec_ref[ROW_CONV + 1:ROW_CONV + 2, :] * _shift_rows(u, uc, 1)
            + vec_ref[ROW_CONV + 2:ROW_CONV + 3, :] * u)
    ob_ref[0] = (pc[:, :D_CONV] * conv).astype(BF16)
    ucarry[...] = u[t - SUBLANES:, :]

    p = jnp.dot(h, wrw_ref[...], preferred_element_type=F32)
    ps = _shift_rows(p, pcarry[...], 1)
    pcarry[...] = p[t - SUBLANES:, :]
    xm = p + (ps - p) * mu_ref[...]
    r = xm[:, :D_RWKV]
    k = xm[:, D_RWKV:2 * D_RWKV]
    v = xm[:, 2 * D_RWKV:3 * D_RWKV]
    wa = xm[:, WA_OFF:GD_OFF]
    gd = xm[:, GD_OFF:]

    def vec(row):
        return vec_ref[row:row + 1, :]

    ld = -EXP_MINUS_HALF * _sigmoid(vec(ROW_W0) + _bdot(jnp.tanh(wa), ww_ref[...]))
    a = _sigmoid(vec(ROW_A0) + _bdot(wa, wa_ref[...]))
    g_ref[0] = _bdot(_sigmoid(gd), wg_ref[...])
    kk = k * vec(ROW_KK)
    kk = kk * lax.rsqrt(jnp.maximum(_head_sums(kk * kk), 1e-24))
    k2 = k * (1.0 + (a - 1.0) * vec(ROW_KA))
    bonus_ref[0] = _head_sums(r * k2 * vec(ROW_RK)) * v + vec(ROW_LNB)
    v_ref[0] = v.astype(BF16)
    na = -kk
    nb = kk * a

    tri = tri_ref[...]
    for c in range(t // CHUNK):
        sl = slice(c * CHUNK, (c + 1) * CHUNK)
        ldc = ld[sl]
        cum = _dot_sel_lhs(tri, ldc)
        tot = cum[CHUNK - 1:CHUNK, :]
        e_inc = jnp.exp(cum)
        e_inv = 1.0 / e_inc
        e_prev = jnp.exp(cum - ldc)
        rt_ref[0, sl, :] = (r[sl] * e_inc).astype(BF16)
        at_ref[0, sl, :] = (na[sl] * e_prev).astype(BF16)
        bt_ref[0, sl, :] = (nb[sl] * e_inv).astype(BF16)
        kt_ref[0, sl, :] = (k2[sl] * e_inv).astype(BF16)
        wc_ref[0, c] = jnp.exp(tot)


def _inproj(x, nm, wrw, wconv, mu, vec, lora, tri, tile):
    b, s, d = x.shape
    nt = s // tile
    const = _const_spec
    tok = lambda w: pl.BlockSpec((1, tile, w), lambda i, j: (i, j, 0))
    outs = ([jax.ShapeDtypeStruct((b, s, D_RWKV), BF16)] * 5
            + [jax.ShapeDtypeStruct((b, s // CHUNK, 1, D_RWKV), F32)]
            + [jax.ShapeDtypeStruct((b, s, D_RWKV), F32)] * 2
            + [jax.ShapeDtypeStruct((b, s, D_CONV), BF16)])
    out_specs = ([tok(D_RWKV)] * 5
                 + [pl.BlockSpec((1, tile // CHUNK, 1, D_RWKV), lambda i, j: (i, j, 0, 0))]
                 + [tok(D_RWKV)] * 2 + [tok(D_CONV)])
    return pl.pallas_call(
        _inproj_kernel,
        grid=(b, nt),
        in_specs=[tok(d), const(1, d), const(d, RW_COLS), const(d, 3 * D_CONV),
                  const(1, RW_COLS), const(VEC_ROWS, D_RWKV)]
                 + [const(*w.shape) for w in lora]
                 + [const(CHUNK, CHUNK)],
        out_specs=out_specs,
        out_shape=outs,
        scratch_shapes=[pltpu.VMEM((SUBLANES, RW_COLS), F32), pltpu.VMEM((SUBLANES, D_CONV), F32)],
        compiler_params=pltpu.CompilerParams(dimension_semantics=("arbitrary", "arbitrary"),
                                             vmem_limit_bytes=VMEM_LIMIT["inproj"]),
        name="inproj",
    )(x, nm, wrw, wconv, mu, vec, *lora, tri)


def _pair_expand(x):
    even = lax.broadcasted_iota(jnp.int32, x.shape, 1) < RWKV_HEAD
    zero = jnp.zeros_like(x)
    return jnp.concatenate([jnp.where(even, x, zero), jnp.where(even, zero, x)], axis=0)


def _rows(*parts):
    return jnp.concatenate(parts, axis=0)


def _wkv_masks():
    row = lax.broadcasted_iota(jnp.int32, (CHUNK, PAIR), 0)
    col = lax.broadcasted_iota(jnp.int32, (CHUNK, PAIR), 1) % RWKV_HEAD
    row2 = lax.broadcasted_iota(jnp.int32, (PAIR, PAIR), 0) < RWKV_HEAD
    col2 = lax.broadcasted_iota(jnp.int32, (PAIR, PAIR), 1) < RWKV_HEAD
    return dict(strict=col < row, incl=col <= row, eye=(col == row).astype(F32),
                same_head=row2 == col2)


def _wkv_prep(refs, wc_ref, c, inst, m, out):
    rows = slice(c * CHUNK, (c + 1) * CHUNK)
    load = lambda ref: [ref[b, rows, p * PAIR:(p + 1) * PAIR] for b, p in inst]
    r, a, bt, kt, v = (load(ref) for ref in refs)
    n = range(len(inst))
    ex = lambda x: _pair_expand(x.astype(BF16))
    g = [_bdot_nt(_rows(a[i], r[i]), _rows(_pair_expand(bt[i]), _pair_expand(kt[i]))) for i in n]
    yield
    lab = [jnp.where(m["strict"], g[i][:CHUNK, :PAIR], 0.0) for i in n]
    lak = [jnp.where(m["strict"], g[i][:CHUNK, PAIR:], 0.0) for i in n]
    mrb = [jnp.where(m["incl"], g[i][CHUNK:, :PAIR], 0.0) for i in n]
    mrk = [jnp.where(m["incl"], g[i][CHUNK:, PAIR:], 0.0) for i in n]
    npow = [_bdot(lab[i], ex(lab[i])) for i in n]
    tinv = [m["eye"] + lab[i] for i in n]
    lmv = [_bdot(_rows(lak[i], mrk[i]), _pair_expand(v[i])) for i in n]
    yield
    for _ in range(4):
        prod = [_bdot(_rows(tinv[i], npow[i]), ex(npow[i])) for i in n]
        tinv = [tinv[i] + prod[i][:CHUNK] for i in n]
        npow = [prod[i][CHUNK:] for i in n]
        yield
    tinv = [tinv[i] + _bdot(tinv[i], ex(npow[i])) for i in n]
    yield
    pq = [_bdot(tinv[i], jnp.concatenate([_pair_expand(a[i]), ex(lmv[i][:CHUNK])], axis=1))
          for i in n]
    bkt = [_rows(bt[i], kt[i]).astype(F32).T.astype(BF16) for i in n]
    wcol = [jnp.broadcast_to(wc_ref[b, c, :, p * PAIR:(p + 1) * PAIR], (PAIR, PAIR)).T
            for b, p in inst]
    out.update(p=[pq[i][:, :PAIR].astype(BF16) for i in n], q=[pq[i][:, PAIR:] for i in n],
               r=r, v=v, bkt=bkt, wcol=wcol, mrb=mrb, mv=[lmv[i][CHUNK:] for i in n])
    yield


def _wkv_chain(pre, z, y_ref, c, inst, m):
    rows = slice(c * CHUNK, (c + 1) * CHUNK)
    n = range(len(inst))
    pr = [_bdot(_rows(pre["p"][i], pre["r"][i]), z[i]) for i in n]
    yield
    u = [(pr[i][:CHUNK] + pre["q"][i]).astype(BF16) for i in n]
    for i in n:
        z[i] = pre["wcol"][i] * (z[i] + jnp.where(
            m["same_head"], _bdot(pre["bkt"][i], _rows(u[i], pre["v"][i])), 0.0))
    yield
    for i, (b, p) in enumerate(inst):
        y_ref[b, rows, p * PAIR:(p + 1) * PAIR] = (
            pr[i][CHUNK:] + _bdot(pre["mrb"][i], _pair_expand(u[i])) + pre["mv"][i])
    yield


def _interleave(primary, secondary, ratio):
    for _ in primary:
        for _ in itertools.islice(secondary, ratio):
            pass
    for _ in secondary:
        pass


def _cast_stages(src_refs, dst_refs):
    for src, dst in zip(src_refs, dst_refs):
        dst[...] = src[...].astype(dst.dtype)
        yield


def _wkv_kernel(rt_ref, at_ref, bt_ref, kt_ref, v_ref, wc_ref, *rest):
    nw = (len(rest) - 2) // 2
    w_refs, y_ref, wb_refs, state = rest[:nw], rest[nw], rest[nw + 1:-1], rest[-1]

    @pl.when(pl.program_id(0) == 0)
    def _():
        state[...] = jnp.zeros_like(state)

    casts = _cast_stages(w_refs, wb_refs)
    nb, t, _ = rt_ref.shape
    nc = t // CHUNK
    inst = [(b, p) for b in range(nb) for p in range(N_PAIRS)]
    m = _wkv_masks()
    refs = (rt_ref, at_ref, bt_ref, kt_ref, v_ref)
    z = [state[b, p] for b, p in inst]
    pre = [dict() for _ in range(nc)]
    for _ in _wkv_prep(refs, wc_ref, 0, inst, m, pre[0]):
        pass
    for c in range(nc):
        nxt = _wkv_prep(refs, wc_ref, c + 1, inst, m, pre[c + 1]) if c + 1 < nc else iter(())
        _interleave(_wkv_chain(pre[c], z, y_ref, c, inst, m), nxt, PREP_STAGES_PER_CHAIN_STAGE)
        next(casts, None)
    for _ in casts:
        pass
    for i, (b, p) in enumerate(inst):
        state[b, p] = z[i]


def _wkv(rt, at, bt, kt, v, wc, weights, tile):
    b, s, d = rt.shape
    steps = s // tile
    tok = pl.BlockSpec((b, tile, d), lambda j: (0, j, 0))
    wspecs = [pl.BlockSpec((w.shape[0] // steps, w.shape[1]), lambda j: (j, 0)) for w in weights]
    out = pl.pallas_call(
        _wkv_kernel,
        grid=(steps,),
        in_specs=[tok] * 5 + [pl.BlockSpec((b, tile // CHUNK, 1, d), lambda j: (0, j, 0, 0))] + wspecs,
        out_specs=[tok] + wspecs,
        out_shape=[jax.ShapeDtypeStruct((b, s, d), F32)]
                  + [jax.ShapeDtypeStruct(w.shape, BF16) for w in weights],
        scratch_shapes=[pltpu.VMEM((b, N_PAIRS, PAIR, PAIR), F32)],
        compiler_params=pltpu.CompilerParams(dimension_semantics=("arbitrary",),
                                             vmem_limit_bytes=VMEM_LIMIT["wkv"]),
        name="wkv",
    )(rt, at, bt, kt, v, wc, *weights)
    return out[0], out[1:]


def _mix_kernel(x_ref, y_ref, g_ref, bonus_ref, ob_ref, k_ref, v_ref, vec_ref, nm_ref, bg_ref,
                wgate_ref, wpa_ref, wpb_ref, wom_ref, nx_ref, wq_ref, wxo_ref, out_ref):
    x = x_ref[0]
    h = _rms(x, nm_ref[...]).astype(BF16)
    sg = _sigmoid(jnp.dot(h, wgate_ref[...], preferred_element_type=F32) + bg_ref[...])

    y = y_ref[0]
    yc = y - _head_sums(y) * (1.0 / RWKV_HEAD)
    var = _head_sums(yc * yc) * (1.0 / RWKV_HEAD)
    yn = yc * lax.rsqrt(var + LN_X_EPS)
    oa = (yn * vec_ref[ROW_LNW:ROW_LNW + 1, :] + bonus_ref[0]) * g_ref[0]

    merged = (sg[:, :D_MODEL] * _bdot(oa, wpa_ref[...])
              + sg[:, D_MODEL:] * jnp.dot(ob_ref[0], wpb_ref[...], preferred_element_type=F32))
    x1 = x + _bdot(merged, wom_ref[...])

    q = _bdot(_rms(x1, nx_ref[...]), wq_ref[...]) * (XHEAD_DIM ** -0.5)
    heads = []
    for hh in range(N_XHEADS):
        cols = slice(hh * XHEAD_DIM, (hh + 1) * XHEAD_DIM)
        sc = _bdot_nt(q[:, cols], k_ref[0, :, cols])
        e = jnp.exp(sc - jnp.max(sc, axis=-1, keepdims=True))
        pr = e / jnp.sum(e, axis=-1, keepdims=True)
        heads.append(_bdot(pr, v_ref[0, :, cols]))
    o = jnp.concatenate(heads, axis=-1)
    out_ref[0] = x1 + _bdot(o, wxo_ref[...])


def _mix(x, y, g, bonus, ob, kmem, vmem, vec, nm, bg, wgate, wpa, wpb, wom, nx, wq, wxo, tile):
    b, s, d = x.shape
    n_mem = kmem.shape[1]
    const = _const_spec
    tok = lambda w: pl.BlockSpec((1, tile, w), lambda i, j: (i, j, 0))
    memspec = pl.BlockSpec((1, n_mem, d), lambda i, j: (i, 0, 0))
    return pl.pallas_call(
        _mix_kernel,
        grid=(b, s // tile),
        in_specs=[tok(d), tok(D_RWKV), tok(D_RWKV), tok(D_RWKV), tok(D_CONV), memspec, memspec,
                  const(VEC_ROWS, D_RWKV), const(1, d), const(1, 2 * d), const(d, 2 * d),
                  const(D_RWKV, d), const(D_CONV, d), const(d, d), const(1, d), const(d, d),
                  const(d, d)],
        out_specs=tok(d),
        out_shape=jax.ShapeDtypeStruct((b, s, d), F32),
        compiler_params=pltpu.CompilerParams(dimension_semantics=("arbitrary", "arbitrary"),
                                             vmem_limit_bytes=VMEM_LIMIT["mix"]),
        name="mix",
    )(x, y, g, bonus, ob, kmem, vmem, vec, nm, bg, wgate, wpa, wpb, wom, nx, wq, wxo)


def _mlp_kernel(x_ref, nm_ref, wup_ref, wdown_ref, nf_ref, out_ref):
    x = x_ref[...]
    h = _rms(x, nm_ref[...]).astype(BF16)
    acc = x
    for c in range(D_FF // D_MODEL):
        cols = slice(c * D_MODEL, (c + 1) * D_MODEL)
        up = jnp.maximum(jnp.dot(h, wup_ref[:, cols], preferred_element_type=F32), 0.0)
        acc = acc + _bdot(up * up, wdown_ref[cols, :])
    out_ref[...] = _rms(acc, nf_ref[...])


def _mlp(x, nm, wup, wdown, nf, tile):
    n, d = x.shape
    const = _const_spec
    tok = pl.BlockSpec((tile, d), lambda i: (i, 0))
    return pl.pallas_call(
        _mlp_kernel,
        grid=(n // tile,),
        in_specs=[tok, const(1, d), const(d, D_FF), const(D_FF, d), const(1, d)],
        out_specs=tok,
        out_shape=jax.ShapeDtypeStruct((n, d), F32),
        compiler_params=pltpu.CompilerParams(dimension_semantics=("arbitrary",),
                                             vmem_limit_bytes=VMEM_LIMIT["mlp"]),
        name="mlp",
    )(x, nm, wup, wdown, nf)


def _pad_cols(w, width):
    return jnp.pad(w, ((0, 0), (0, width - w.shape[1])))


def _pad_rows(w, height):
    return jnp.pad(w, ((0, height - w.shape[0]), (0, 0)))


def _layer(x, mem, norm_mix, w_in, b_gate, mu_shift, w0, w_lora_w, a0, w_lora_a, w_lora_g,
           k_k, k_a, r_k, ln_x_w, ln_x_b, conv_w, w_proj_a, w_proj_b, w_out_mix, norm_xattn,
           norm_mem, w_q, w_kv, w_xo, norm_mlp, w_up, w_down, norm_final):
    bsz, s, d = x.shape
    c_conv = RW_COLS - 2 * LANES + GATE_LORA
    c_gate = c_conv + 3 * D_CONV
    w_in = w_in.astype(BF16)
    wconv = w_in[:, c_conv:c_gate]
    wgate = w_in[:, c_gate:]
    mu = _pad_cols(mu_shift[None], RW_COLS)
    lora = [jnp.pad(w, ((top, rows - top - w.shape[0]), (0, 0))).astype(BF16)
            for w, top, rows in ((w_lora_w, 0, LANES), (w_lora_a, DECAY_LORA, LANES),
                                 (w_lora_g, 0, 2 * LANES))]
    vec = jnp.stack([w0, a0, k_k, k_a, r_k.reshape(-1), ln_x_b, ln_x_w,
                     conv_w[0, 0], conv_w[1, 0], conv_w[2, 0]])
    vec = _pad_rows(vec, VEC_ROWS)
    step = jnp.arange(CHUNK)
    tri = (step[None, :] <= step[:, None]).astype(BF16)

    rt, at, bt, kt, v, wc, g, bonus, ob = _inproj(
        x, norm_mix[None], w_in, wconv, mu, vec, lora, tri, tile=TILE_INPROJ)
    later = (w_proj_a, w_proj_b, w_out_mix, w_q, w_xo, w_up, w_down, w_kv)
    y, (wpa, wpb, wom, wq, wxo, wup, wdown, wkv) = _wkv(
        rt, at, bt, kt, v, wc, later, tile=TILE_WKV)
    kmem, vmem = _memkv(mem, norm_mem[None], wkv)
    x = _mix(x, y, g, bonus, ob, kmem, vmem, vec, norm_mix[None], b_gate[None], wgate,
             wpa, wpb, wom, norm_xattn[None], wq, wxo, tile=TILE_MIX)
    return _mlp(x.reshape(bsz * s, d), norm_mlp[None], wup, wdown, norm_final[None],
                tile=TILE_MLP).reshape(bsz, s, d)


def kernel(x, mem, norm_mix, w_in, b_gate, mu_shift, w0, w_lora_w, a0, w_lora_a, w_lora_g, k_k, k_a, r_k, ln_x_w, ln_x_b, conv_w, w_proj_a, w_proj_b, w_out_mix, norm_xattn, norm_mem, w_q, w_kv, w_xo, norm_mlp, w_up, w_down, norm_final):
    assert w_in.shape[0] == 1, "the MLP kernel fuses the final norm: single-layer trunk only"
    per_layer = (norm_mix, w_in, b_gate, mu_shift, w0, w_lora_w, a0, w_lora_a, w_lora_g, k_k, k_a,
                 r_k, ln_x_w, ln_x_b, conv_w, w_proj_a, w_proj_b, w_out_mix, norm_xattn, norm_mem,
                 w_q, w_kv, w_xo, norm_mlp, w_up, w_down)
    return _layer(x, mem, *(p[0] for p in per_layer), norm_final)
```

```python
import functools
import itertools
import math

import jax
import jax.numpy as jnp
from jax import lax
from jax.experimental import pallas as pl
from jax.experimental.pallas import tpu as pltpu

F32 = jnp.float32
BF16 = jnp.bfloat16

D_MODEL = 1024
D_RWKV = 512
RWKV_HEAD = 64
DECAY_LORA = 64
AAA_LORA = 64
GATE_LORA = 160
LN_X_EPS = 64e-5
D_CONV = 512
N_XHEADS = 4
XHEAD_DIM = D_MODEL // N_XHEADS
D_FF = 4 * D_MODEL
RMS_EPS = 1e-6
EXP_MINUS_HALF = math.exp(-0.5)

LANES = 128
SUBLANES = 8
CHUNK = 64
PAIR = 2 * RWKV_HEAD
N_PAIRS = D_RWKV // PAIR
WA_OFF = 3 * D_RWKV
GD_OFF = WA_OFF + DECAY_LORA + AAA_LORA
RW_COLS = GD_OFF + 2 * LANES
assert DECAY_LORA + AAA_LORA == LANES and GATE_LORA <= 2 * LANES
MIB = 1024 * 1024
VMEM_LIMIT = {"memkv": 12 * MIB, "inproj": 54 * MIB, "wkv": 32 * MIB, "mix": 56 * MIB,
              "mlp": 44 * MIB}

TILE_INPROJ = 1024
TILE_WKV = 8 * CHUNK
TILE_MIX = 1024
TILE_MLP = 1024
PREP_STAGES_PER_CHAIN_STAGE = 3


def _bdot(a, b):
    return jnp.dot(a.astype(BF16), b.astype(BF16), preferred_element_type=F32)


def _bdot_nt(a, b):
    return lax.dot_general(a.astype(BF16), b.astype(BF16), (((1,), (1,)), ((), ())),
                           preferred_element_type=F32)


def _split2(a):
    hi = a.astype(BF16)
    lo = (a - hi.astype(F32)).astype(BF16)
    return hi, lo


def _head_sums(a):
    low = lax.broadcasted_iota(jnp.int32, (a.shape[0], LANES), 1) < RWKV_HEAD
    out = []
    for j in range(a.shape[1] // LANES):
        slab = a[:, j * LANES:(j + 1) * LANES]
        s_lo = jnp.sum(jnp.where(low, slab, 0.0), axis=-1, keepdims=True)
        s_hi = jnp.sum(jnp.where(low, 0.0, slab), axis=-1, keepdims=True)
        out.append(jnp.where(low, s_lo, s_hi))
    return jnp.concatenate(out, axis=-1)


def _dot_sel_lhs(sel, a):
    hi, lo = _split2(a)
    d = functools.partial(jnp.dot, preferred_element_type=F32)
    return d(sel, hi) + d(sel, lo)


def _rms(x, g):
    return x * lax.rsqrt(jnp.mean(x * x, axis=-1, keepdims=True) + RMS_EPS) * g


def _sigmoid(x):
    return 1.0 / (1.0 + jnp.exp(-x))


def _shift_rows(x, carry_rows, n):
    nc = carry_rows.shape[0]
    out = pltpu.roll(x, n, 0)
    rows = lax.broadcasted_iota(jnp.int32, (nc, 1), 0)
    head = out[:nc]
    for j in range(n):
        head = jnp.where(rows == j, carry_rows[nc - n + j:nc - n + j + 1, :], head)
    return jnp.concatenate([head, out[nc:]], axis=0)


def _const_spec(*shape):
    return pl.BlockSpec(shape, lambda *_: (0,) * len(shape), pipeline_mode=pl.Buffered(1))


def _memkv_kernel(mem_ref, g_ref, w_ref, k_ref, v_ref):
    m = _rms(mem_ref[0], g_ref[...])
    kv = _bdot(m, w_ref[...])
    k_ref[0] = kv[:, :D_MODEL].astype(BF16)
    v_ref[0] = kv[:, D_MODEL:].astype(BF16)


def _memkv(mem, g, w_kv):
    b, n, d = mem.shape
    return pl.pallas_call(
        _memkv_kernel,
        grid=(b,),
        in_specs=[pl.BlockSpec((1, n, d), lambda i: (i, 0, 0)),
                  pl.BlockSpec((1, d), lambda i: (0, 0)),
                  pl.BlockSpec((d, 2 * d), lambda i: (0, 0))],
        out_specs=[pl.BlockSpec((1, n, d), lambda i: (i, 0, 0))] * 2,
        out_shape=[jax.ShapeDtypeStruct((b, n, d), BF16)] * 2,
        compiler_params=pltpu.CompilerParams(dimension_semantics=("arbitrary",),
                                             vmem_limit_bytes=VMEM_LIMIT["memkv"]),
        name="memkv",
    )(mem, g, w_kv)


def _inproj_kernel(x_ref, nm_ref, wrw_ref, wconv_ref,
                   mu_ref, w0_ref, a0_ref, kk_ref, ka_ref, rk_ref, lnb_ref, cw_ref,
                   ww_ref, wa_ref, wg_ref, tri_ref,
                   rt_ref, at_ref, bt_ref, kt_ref, v_ref, wc_ref,
                   g_ref, bonus_ref, ob_ref,
                   pcarry, ucarry):
    @pl.when(pl.program_id(1) == 0)
    def _():
        pcarry[...] = jnp.zeros_like(pcarry)
        ucarry[...] = jnp.zeros_like(ucarry)

    t = x_ref.shape[1]
    h = _rms(x_ref[0], nm_ref[...]).astype(BF16)

    pc = jnp.dot(h, wconv_ref[...], preferred_element_type=F32)
    u = pc[:, D_CONV:2 * D_CONV] * pc[:, 2 * D_CONV:]
    uc = ucarry[...]
    conv = cw_ref[0] * _shift_rows(u, uc, 2) + cw_ref[1] * _shift_rows(u, uc, 1) + cw_ref[2] * u
    ob_ref[0] = (pc[:, :D_CONV] * conv).astype(BF16)
    ucarry[...] = u[t - SUBLANES:, :]

    p = jnp.dot(h, wrw_ref[...], preferred_element_type=F32)
    ps = _shift_rows(p, pcarry[...], 1)
    pcarry[...] = p[t - SUBLANES:, :]
    mu = jnp.concatenate([mu_ref[...], jnp.zeros((1, RW_COLS - mu_ref.shape[1]), F32)], axis=1)
    xm = p + (ps - p) * mu
    r = xm[:, :D_RWKV]
    k = xm[:, D_RWKV:2 * D_RWKV]
    v = xm[:, 2 * D_RWKV:3 * D_RWKV]
    wa = xm[:, WA_OFF:GD_OFF]
    gd = xm[:, GD_OFF:]

    zeros = lambda rows: jnp.zeros((rows, D_RWKV), BF16)
    ww = _rows(ww_ref[...].astype(BF16), zeros(LANES - DECAY_LORA))
    wa_w = _rows(zeros(DECAY_LORA), wa_ref[...].astype(BF16))
    wg = _rows(wg_ref[...].astype(BF16), zeros(2 * LANES - GATE_LORA))
    rk = jnp.concatenate([rk_ref[h:h + 1, :] for h in range(rk_ref.shape[0])], axis=1)
    ld = -EXP_MINUS_HALF * _sigmoid(w0_ref[...] + _bdot(jnp.tanh(wa), ww))
    a = _sigmoid(a0_ref[...] + _bdot(wa, wa_w))
    g_ref[0] = _bdot(_sigmoid(gd), wg)
    kk = k * kk_ref[...]
    kk = kk * lax.rsqrt(jnp.maximum(_head_sums(kk * kk), 1e-24))
    k2 = k * (1.0 + (a - 1.0) * ka_ref[...])
    bonus_ref[0] = _head_sums(r * k2 * rk) * v + lnb_ref[...]
    v_ref[0] = v.astype(BF16)
    na = -kk
    nb = kk * a

    tri = tri_ref[...]
    for c in range(t // CHUNK):
        sl = slice(c * CHUNK, (c + 1) * CHUNK)
        ldc = ld[sl]
        cum = _dot_sel_lhs(tri, ldc)
        tot = cum[CHUNK - 1:CHUNK, :]
        e_inc = jnp.exp(cum)
        e_inv = 1.0 / e_inc
        e_prev = jnp.exp(cum - ldc)
        rt_ref[0, sl, :] = (r[sl] * e_inc).astype(BF16)
        at_ref[0, sl, :] = (na[sl] * e_prev).astype(BF16)
        bt_ref[0, sl, :] = (nb[sl] * e_inv).astype(BF16)
        kt_ref[0, sl, :] = (k2[sl] * e_inv).astype(BF16)
        wc_ref[0, c] = jnp.exp(tot)


def _inproj(x, nm, wrw, wconv, small, tri, tile):
    b, s, d = x.shape
    nt = s // tile
    const = _const_spec
    tok = lambda w: pl.BlockSpec((1, tile, w), lambda i, j: (i, j, 0))
    outs = ([jax.ShapeDtypeStruct((b, s, D_RWKV), BF16)] * 5
            + [jax.ShapeDtypeStruct((b, s // CHUNK, 1, D_RWKV), F32)]
            + [jax.ShapeDtypeStruct((b, s, D_RWKV), F32)] * 2
            + [jax.ShapeDtypeStruct((b, s, D_CONV), BF16)])
    out_specs = ([tok(D_RWKV)] * 5
                 + [pl.BlockSpec((1, tile // CHUNK, 1, D_RWKV), lambda i, j: (i, j, 0, 0))]
                 + [tok(D_RWKV)] * 2 + [tok(D_CONV)])
    return pl.pallas_call(
        _inproj_kernel,
        grid=(b, nt),
        in_specs=[tok(d), const(1, d), const(d, RW_COLS), const(d, 3 * D_CONV)]
                 + [const(*w.shape) for w in small]
                 + [const(CHUNK, CHUNK)],
        out_specs=out_specs,
        out_shape=outs,
        scratch_shapes=[pltpu.VMEM((SUBLANES, RW_COLS), F32), pltpu.VMEM((SUBLANES, D_CONV), F32)],
        compiler_params=pltpu.CompilerParams(dimension_semantics=("arbitrary", "arbitrary"),
                                             vmem_limit_bytes=VMEM_LIMIT["inproj"]),
        name="inproj",
    )(x, nm, wrw, wconv, *small, tri)


def _pair_expand(x):
    even = lax.broadcasted_iota(jnp.int32, x.shape, 1) < RWKV_HEAD
    zero = jnp.zeros_like(x)
    return jnp.concatenate([jnp.where(even, x, zero), jnp.where(even, zero, x)], axis=0)


def _rows(*parts):
    return jnp.concatenate(parts, axis=0)


def _wkv_masks():
    row = lax.broadcasted_iota(jnp.int32, (CHUNK, PAIR), 0)
    col = lax.broadcasted_iota(jnp.int32, (CHUNK, PAIR), 1) % RWKV_HEAD
    row2 = lax.broadcasted_iota(jnp.int32, (PAIR, PAIR), 0) < RWKV_HEAD
    col2 = lax.broadcasted_iota(jnp.int32, (PAIR, PAIR), 1) < RWKV_HEAD
    return dict(strict=col < row, incl=col <= row, eye=(col == row).astype(F32),
                same_head=row2 == col2)


def _wkv_prep(refs, wc_ref, c, inst, m, out):
    rows = slice(c * CHUNK, (c + 1) * CHUNK)
    load = lambda ref: [ref[b, rows, p * PAIR:(p + 1) * PAIR] for b, p in inst]
    r, a, bt, kt, v = (load(ref) for ref in refs)
    n = range(len(inst))
    ex = lambda x: _pair_expand(x.astype(BF16))
    g = [_bdot_nt(_rows(a[i], r[i]), _rows(_pair_expand(bt[i]), _pair_expand(kt[i]))) for i in n]
    yield
    lab = [jnp.where(m["strict"], g[i][:CHUNK, :PAIR], 0.0) for i in n]
    lak = [jnp.where(m["strict"], g[i][:CHUNK, PAIR:], 0.0) for i in n]
    mrb = [jnp.where(m["incl"], g[i][CHUNK:, :PAIR], 0.0) for i in n]
    mrk = [jnp.where(m["incl"], g[i][CHUNK:, PAIR:], 0.0) for i in n]
    npow = [_bdot(lab[i], ex(lab[i])) for i in n]
    tinv = [m["eye"] + lab[i] for i in n]
    lmv = [_bdot(_rows(lak[i], mrk[i]), _pair_expand(v[i])) for i in n]
    yield
    for _ in range(4):
        prod = [_bdot(_rows(tinv[i], npow[i]), ex(npow[i])) for i in n]
        tinv = [tinv[i] + prod[i][:CHUNK] for i in n]
        npow = [prod[i][CHUNK:] for i in n]
        yield
    tinv = [tinv[i] + _bdot(tinv[i], ex(npow[i])) for i in n]
    yield
    pq = [_bdot(tinv[i], jnp.concatenate([_pair_expand(a[i]), ex(lmv[i][:CHUNK])], axis=1))
          for i in n]
    bkt = [_rows(bt[i], kt[i]).astype(F32).T.astype(BF16) for i in n]
    wcol = [jnp.broadcast_to(wc_ref[b, c, :, p * PAIR:(p + 1) * PAIR], (PAIR, PAIR)).T
            for b, p in inst]
    out.update(p=[pq[i][:, :PAIR].astype(BF16) for i in n], q=[pq[i][:, PAIR:] for i in n],
               r=r, v=v, bkt=bkt, wcol=wcol, mrb=mrb, mv=[lmv[i][CHUNK:] for i in n])
    yield


def _wkv_chain(pre, z, y_ref, c, inst, m):
    rows = slice(c * CHUNK, (c + 1) * CHUNK)
    n = range(len(inst))
    pr = [_bdot(_rows(pre["p"][i], pre["r"][i]), z[i]) for i in n]
    yield
    u = [(pr[i][:CHUNK] + pre["q"][i]).astype(BF16) for i in n]
    for i in n:
        z[i] = pre["wcol"][i] * (z[i] + jnp.where(
            m["same_head"], _bdot(pre["bkt"][i], _rows(u[i], pre["v"][i])), 0.0))
    yield
    for i, (b, p) in enumerate(inst):
        y_ref[b, rows, p * PAIR:(p + 1) * PAIR] = (
            pr[i][CHUNK:] + _bdot(pre["mrb"][i], _pair_expand(u[i])) + pre["mv"][i])
    yield


def _interleave(primary, secondary, ratio):
    for _ in primary:
        for _ in itertools.islice(secondary, ratio):
            pass
    for _ in secondary:
        pass


def _cast_stages(src_refs, dst_refs):
    for src, dst in zip(src_refs, dst_refs):
        dst[...] = src[:, src.shape[1] - dst.shape[1]:].astype(dst.dtype)
        yield


def _wkv_kernel(rt_ref, at_ref, bt_ref, kt_ref, v_ref, wc_ref, *rest):
    nw = (len(rest) - 2) // 2
    w_refs, y_ref, wb_refs, state = rest[:nw], rest[nw], rest[nw + 1:-1], rest[-1]

    @pl.when(pl.program_id(0) == 0)
    def _():
        state[...] = jnp.zeros_like(state)

    casts = _cast_stages(w_refs, wb_refs)
    nb, t, _ = rt_ref.shape
    nc = t // CHUNK
    inst = [(b, p) for b in range(nb) for p in range(N_PAIRS)]
    m = _wkv_masks()
    refs = (rt_ref, at_ref, bt_ref, kt_ref, v_ref)
    z = [state[b, p] for b, p in inst]
    pre = [dict() for _ in range(nc)]
    for _ in _wkv_prep(refs, wc_ref, 0, inst, m, pre[0]):
        pass
    for c in range(nc):
        nxt = _wkv_prep(refs, wc_ref, c + 1, inst, m, pre[c + 1]) if c + 1 < nc else iter(())
        _interleave(_wkv_chain(pre[c], z, y_ref, c, inst, m), nxt, PREP_STAGES_PER_CHAIN_STAGE)
        next(casts, None)
    for _ in casts:
        pass
    for i, (b, p) in enumerate(inst):
        state[b, p] = z[i]


def _wkv(rt, at, bt, kt, v, wc, weights, tile):
    b, s, d = rt.shape
    steps = s // tile
    tok = pl.BlockSpec((b, tile, d), lambda j: (0, j, 0))
    rows = lambda w, cols: pl.BlockSpec((w.shape[0] // steps, cols), lambda j: (j, 0))
    out = pl.pallas_call(
        _wkv_kernel,
        grid=(steps,),
        in_specs=[tok] * 5 + [pl.BlockSpec((b, tile // CHUNK, 1, d), lambda j: (0, j, 0, 0))]
                 + [rows(w, w.shape[1]) for w, _ in weights],
        out_specs=[tok] + [rows(w, cols) for w, cols in weights],
        out_shape=[jax.ShapeDtypeStruct((b, s, d), F32)]
                  + [jax.ShapeDtypeStruct((w.shape[0], cols), BF16) for w, cols in weights],
        scratch_shapes=[pltpu.VMEM((b, N_PAIRS, PAIR, PAIR), F32)],
        compiler_params=pltpu.CompilerParams(dimension_semantics=("arbitrary",),
                                             vmem_limit_bytes=VMEM_LIMIT["wkv"]),
        name="wkv",
    )(rt, at, bt, kt, v, wc, *(w for w, _ in weights))
    return out[0], out[1:]


def _mix_kernel(x_ref, y_ref, g_ref, bonus_ref, ob_ref, k_ref, v_ref, lnw_ref, nm_ref, bg_ref,
                wgate_ref, wpa_ref, wpb_ref, wom_ref, nx_ref, wq_ref, wxo_ref, out_ref):
    x = x_ref[0]
    h = _rms(x, nm_ref[...]).astype(BF16)
    sg = _sigmoid(jnp.dot(h, wgate_ref[...], preferred_element_type=F32) + bg_ref[...])

    y = y_ref[0]
    yc = y - _head_sums(y) * (1.0 / RWKV_HEAD)
    var = _head_sums(yc * yc) * (1.0 / RWKV_HEAD)
    yn = yc * lax.rsqrt(var + LN_X_EPS)
    oa = (yn * lnw_ref[...] + bonus_ref[0]) * g_ref[0]

    merged = (sg[:, :D_MODEL] * _bdot(oa, wpa_ref[...])
              + sg[:, D_MODEL:] * jnp.dot(ob_ref[0], wpb_ref[...], preferred_element_type=F32))
    x1 = x + _bdot(merged, wom_ref[...])

    q = _bdot(_rms(x1, nx_ref[...]), wq_ref[...]) * (XHEAD_DIM ** -0.5)
    heads = []
    for hh in range(N_XHEADS):
        cols = slice(hh * XHEAD_DIM, (hh + 1) * XHEAD_DIM)
        sc = _bdot_nt(q[:, cols], k_ref[0, :, cols])
        e = jnp.exp(sc - jnp.max(sc, axis=-1, keepdims=True))
        pr = e / jnp.sum(e, axis=-1, keepdims=True)
        heads.append(_bdot(pr, v_ref[0, :, cols]))
    o = jnp.concatenate(heads, axis=-1)
    out_ref[0] = x1 + _bdot(o, wxo_ref[...])


def _mix(x, y, g, bonus, ob, kmem, vmem, lnw, nm, bg, wgate, wpa, wpb, wom, nx, wq, wxo, tile):
    b, s, d = x.shape
    n_mem = kmem.shape[1]
    const = _const_spec
    tok = lambda w: pl.BlockSpec((1, tile, w), lambda i, j: (i, j, 0))
    memspec = pl.BlockSpec((1, n_mem, d), lambda i, j: (i, 0, 0))
    return pl.pallas_call(
        _mix_kernel,
        grid=(b, s // tile),
        in_specs=[tok(d), tok(D_RWKV), tok(D_RWKV), tok(D_RWKV), tok(D_CONV), memspec, memspec,
                  const(1, D_RWKV), const(1, d), const(1, 2 * d), const(d, 2 * d),
                  const(D_RWKV, d), const(D_CONV, d), const(d, d), const(1, d), const(d, d),
                  const(d, d)],
        out_specs=tok(d),
        out_shape=jax.ShapeDtypeStruct((b, s, d), F32),
        compiler_params=pltpu.CompilerParams(dimension_semantics=("arbitrary", "arbitrary"),
                                             vmem_limit_bytes=VMEM_LIMIT["mix"]),
        name="mix",
    )(x, y, g, bonus, ob, kmem, vmem, lnw, nm, bg, wgate, wpa, wpb, wom, nx, wq, wxo)


def _mlp_kernel(x_ref, nm_ref, wup_ref, wdown_ref, nf_ref, out_ref):
    x = x_ref[...]
    h = _rms(x, nm_ref[...]).astype(BF16)
    acc = x
    for c in range(D_FF // D_MODEL):
        cols = slice(c * D_MODEL, (c + 1) * D_MODEL)
        up = jnp.maximum(jnp.dot(h, wup_ref[:, cols], preferred_element_type=F32), 0.0)
        acc = acc + _bdot(up * up, wdown_ref[cols, :])
    out_ref[...] = _rms(acc, nf_ref[...])


def _mlp(x, nm, wup, wdown, nf, tile):
    n, d = x.shape
    const = _const_spec
    tok = pl.BlockSpec((tile, d), lambda i: (i, 0))
    return pl.pallas_call(
        _mlp_kernel,
        grid=(n // tile,),
        in_specs=[tok, const(1, d), const(d, D_FF), const(D_FF, d), const(1, d)],
        out_specs=tok,
        out_shape=jax.ShapeDtypeStruct((n, d), F32),
        compiler_params=pltpu.CompilerParams(dimension_semantics=("arbitrary",),
                                             vmem_limit_bytes=VMEM_LIMIT["mlp"]),
        name="mlp",
    )(x, nm, wup, wdown, nf)


def _layer(x, mem, norm_mix, w_in, b_gate, mu_shift, w0, w_lora_w, a0, w_lora_a, w_lora_g,
           k_k, k_a, r_k, ln_x_w, ln_x_b, conv_w, w_proj_a, w_proj_b, w_out_mix, norm_xattn,
           norm_mem, w_q, w_kv, w_xo, norm_mlp, w_up, w_down, norm_final):
    bsz, s, d = x.shape
    c_conv = RW_COLS - 2 * LANES + GATE_LORA
    c_gate = c_conv + 3 * D_CONV
    w_in = w_in.astype(BF16)
    wconv = w_in[:, c_conv:c_gate]
    small = (mu_shift[None], w0[None], a0[None], k_k[None], k_a[None], r_k, ln_x_b[None], conv_w,
             w_lora_w, w_lora_a, w_lora_g)
    step = jnp.arange(CHUNK)
    tri = (step[None, :] <= step[:, None]).astype(BF16)

    rt, at, bt, kt, v, wc, g, bonus, ob = _inproj(
        x, norm_mix[None], w_in, wconv, small, tri, tile=TILE_INPROJ)
    later = (w_proj_a, w_proj_b, w_out_mix, w_q, w_xo, w_up, w_down, w_kv)
    y, (wpa, wpb, wom, wq, wxo, wup, wdown, wkv, wgate) = _wkv(
        rt, at, bt, kt, v, wc,
        [(w, w.shape[1]) for w in later] + [(w_in, w_in.shape[1] - c_gate)], tile=TILE_WKV)
    kmem, vmem = _memkv(mem, norm_mem[None], wkv)
    x = _mix(x, y, g, bonus, ob, kmem, vmem, ln_x_w[None], norm_mix[None], b_gate[None], wgate,
             wpa, wpb, wom, norm_xattn[None], wq, wxo, tile=TILE_MIX)
    return _mlp(x.reshape(bsz * s, d), norm_mlp[None], wup, wdown, norm_final[None],
                tile=TILE_MLP).reshape(bsz, s, d)


def kernel(x, mem, norm_mix, w_in, b_gate, mu_shift, w0, w_lora_w, a0, w_lora_a, w_lora_g, k_k, k_a, r_k, ln_x_w, ln_x_b, conv_w, w_proj_a, w_proj_b, w_out_mix, norm_xattn, norm_mem, w_q, w_kv, w_xo, norm_mlp, w_up, w_down, norm_final):
    assert w_in.shape[0] == 1, "the MLP kernel fuses the final norm: single-layer trunk only"
    per_layer = (norm_mix, w_in, b_gate, mu_shift, w0, w_lora_w, a0, w_lora_a, w_lora_g, k_k, k_a,
                 r_k, ln_x_w, ln_x_b, conv_w, w_proj_a, w_proj_b, w_out_mix, norm_xattn, norm_mem,
                 w_q, w_kv, w_xo, norm_mlp, w_up, w_down)
    return _layer(x, mem, *(p[0] for p in per_layer), norm_final)
```

```python
import functools
import itertools
import math

import jax
import jax.numpy as jnp
from jax import lax
from jax.experimental import pallas as pl
from jax.experimental.pallas import tpu as pltpu

F32 = jnp.float32
BF16 = jnp.bfloat16

D_MODEL = 1024
D_RWKV = 512
RWKV_HEAD = 64
DECAY_LORA = 64
AAA_LORA = 64
GATE_LORA = 160
LN_X_EPS = 64e-5
D_CONV = 512
N_XHEADS = 4
XHEAD_DIM = D_MODEL // N_XHEADS
D_FF = 4 * D_MODEL
RMS_EPS = 1e-6
EXP_MINUS_HALF = math.exp(-0.5)

LANES = 128
SUBLANES = 8
CHUNK = 64
PAIR = 2 * RWKV_HEAD
N_PAIRS = D_RWKV // PAIR
WA_OFF = 3 * D_RWKV
GD_OFF = WA_OFF + DECAY_LORA + AAA_LORA
RW_COLS = GD_OFF + 2 * LANES
assert DECAY_LORA + AAA_LORA == LANES and GATE_LORA <= 2 * LANES
MIB = 1024 * 1024
VMEM_LIMIT = {"memkv": 12 * MIB, "inproj": 54 * MIB, "wkv": 32 * MIB, "mix": 56 * MIB,
              "mlp": 44 * MIB}

TILE_INPROJ = 1024
TILE_WKV = 8 * CHUNK
TILE_MIX = 1024
TILE_MLP = 1024
PREP_STAGES_PER_CHAIN_STAGE = 3


def _bdot(a, b):
    return jnp.dot(a.astype(BF16), b.astype(BF16), preferred_element_type=F32)


def _bdot_nt(a, b):
    return lax.dot_general(a.astype(BF16), b.astype(BF16), (((1,), (1,)), ((), ())),
                           preferred_element_type=F32)


def _split2(a):
    hi = a.astype(BF16)
    lo = (a - hi.astype(F32)).astype(BF16)
    return hi, lo


def _head_sums(a):
    low = lax.broadcasted_iota(jnp.int32, (a.shape[0], LANES), 1) < RWKV_HEAD
    out = []
    for j in range(a.shape[1] // LANES):
        slab = a[:, j * LANES:(j + 1) * LANES]
        s_lo = jnp.sum(jnp.where(low, slab, 0.0), axis=-1, keepdims=True)
        s_hi = jnp.sum(jnp.where(low, 0.0, slab), axis=-1, keepdims=True)
        out.append(jnp.where(low, s_lo, s_hi))
    return jnp.concatenate(out, axis=-1)


def _dot_sel_lhs(sel, a):
    hi, lo = _split2(a)
    d = functools.partial(jnp.dot, preferred_element_type=F32)
    return d(sel, hi) + d(sel, lo)


def _rms(x, g):
    return x * lax.rsqrt(jnp.mean(x * x, axis=-1, keepdims=True) + RMS_EPS) * g


def _sigmoid(x):
    return 1.0 / (1.0 + jnp.exp(-x))


def _shift_rows(x, carry_rows, n):
    nc = carry_rows.shape[0]
    out = pltpu.roll(x, n, 0)
    rows = lax.broadcasted_iota(jnp.int32, (nc, 1), 0)
    head = out[:nc]
    for j in range(n):
        head = jnp.where(rows == j, carry_rows[nc - n + j:nc - n + j + 1, :], head)
    return jnp.concatenate([head, out[nc:]], axis=0)


def _const_spec(*shape):
    return pl.BlockSpec(shape, lambda *_: (0,) * len(shape), pipeline_mode=pl.Buffered(1))


def _memkv_kernel(mem_ref, g_ref, w_ref, k_ref, v_ref):
    m = _rms(mem_ref[0], g_ref[...])
    kv = _bdot(m, w_ref[...])
    k_ref[0] = kv[:, :D_MODEL].astype(BF16)
    v_ref[0] = kv[:, D_MODEL:].astype(BF16)


def _memkv(mem, g, w_kv):
    b, n, d = mem.shape
    return pl.pallas_call(
        _memkv_kernel,
        grid=(b,),
        in_specs=[pl.BlockSpec((1, n, d), lambda i: (i, 0, 0)),
                  pl.BlockSpec((1, d), lambda i: (0, 0)),
                  pl.BlockSpec((d, 2 * d), lambda i: (0, 0))],
        out_specs=[pl.BlockSpec((1, n, d), lambda i: (i, 0, 0))] * 2,
        out_shape=[jax.ShapeDtypeStruct((b, n, d), BF16)] * 2,
        compiler_params=pltpu.CompilerParams(dimension_semantics=("arbitrary",),
                                             vmem_limit_bytes=VMEM_LIMIT["memkv"]),
        name="memkv",
    )(mem, g, w_kv)


def _inproj_kernel(x_ref, nm_ref, wrw_ref, wconv_ref,
                   mu_ref, w0_ref, a0_ref, kk_ref, ka_ref, rk_ref, lnb_ref, cw_ref,
                   ww_ref, wa_ref, wg_ref, tri_ref,
                   rt_ref, at_ref, bt_ref, kt_ref, v_ref, wc_ref,
                   g_ref, bonus_ref, ob_ref,
                   pcarry, ucarry):
    @pl.when(pl.program_id(1) == 0)
    def _():
        pcarry[...] = jnp.zeros_like(pcarry)
        ucarry[...] = jnp.zeros_like(ucarry)

    t = x_ref.shape[1]
    h = _rms(x_ref[0], nm_ref[...]).astype(BF16)

    pc = jnp.dot(h, wconv_ref[...], preferred_element_type=F32)
    u = pc[:, D_CONV:2 * D_CONV] * pc[:, 2 * D_CONV:]
    uc = ucarry[...]
    conv = cw_ref[0] * _shift_rows(u, uc, 2) + cw_ref[1] * _shift_rows(u, uc, 1) + cw_ref[2] * u
    ob_ref[0] = (pc[:, :D_CONV] * conv).astype(BF16)
    ucarry[...] = u[t - SUBLANES:, :]

    p = jnp.dot(h, wrw_ref[...], preferred_element_type=F32)
    ps = _shift_rows(p, pcarry[...], 1)
    pcarry[...] = p[t - SUBLANES:, :]
    mu = jnp.concatenate([mu_ref[...], jnp.zeros((1, RW_COLS - mu_ref.shape[1]), F32)], axis=1)
    xm = p + (ps - p) * mu
    r = xm[:, :D_RWKV]
    k = xm[:, D_RWKV:2 * D_RWKV]
    v = xm[:, 2 * D_RWKV:3 * D_RWKV]
    wa = xm[:, WA_OFF:GD_OFF]
    gd = xm[:, GD_OFF:]

    zeros = lambda rows: jnp.zeros((rows, D_RWKV), BF16)
    ww = _rows(ww_ref[...].astype(BF16), zeros(LANES - DECAY_LORA))
    wa_w = _rows(zeros(DECAY_LORA), wa_ref[...].astype(BF16))
    wg = _rows(wg_ref[...].astype(BF16), zeros(2 * LANES - GATE_LORA))
    rk = jnp.concatenate([rk_ref[h:h + 1, :] for h in range(rk_ref.shape[0])], axis=1)
    ld = -EXP_MINUS_HALF * _sigmoid(w0_ref[...] + _bdot(jnp.tanh(wa), ww))
    a = _sigmoid(a0_ref[...] + _bdot(wa, wa_w))
    g_ref[0] = _bdot(_sigmoid(gd), wg)
    kk = k * kk_ref[...]
    kk = kk * lax.rsqrt(jnp.maximum(_head_sums(kk * kk), 1e-24))
    k2 = k * (1.0 + (a - 1.0) * ka_ref[...])
    bonus_ref[0] = _head_sums(r * k2 * rk) * v + lnb_ref[...]
    v_ref[0] = v.astype(BF16)
    na = -kk
    nb = kk * a

    tri = tri_ref[...]
    for c in range(t // CHUNK):
        sl = slice(c * CHUNK, (c + 1) * CHUNK)
        ldc = ld[sl]
        cum = _dot_sel_lhs(tri, ldc)
        tot = cum[CHUNK - 1:CHUNK, :]
        e_inc = jnp.exp(cum)
        e_inv = 1.0 / e_inc
        e_prev = jnp.exp(cum - ldc)
        rt_ref[0, sl, :] = (r[sl] * e_inc).astype(BF16)
        at_ref[0, sl, :] = (na[sl] * e_prev).astype(BF16)
        bt_ref[0, sl, :] = (nb[sl] * e_inv).astype(BF16)
        kt_ref[0, sl, :] = (k2[sl] * e_inv).astype(BF16)
        wc_ref[0, c] = jnp.exp(tot)


def _inproj(x, nm, wrw, wconv, small, tri, tile):
    b, s, d = x.shape
    nt = s // tile
    const = _const_spec
    tok = lambda w: pl.BlockSpec((1, tile, w), lambda i, j: (i, j, 0))
    outs = ([jax.ShapeDtypeStruct((b, s, D_RWKV), BF16)] * 5
            + [jax.ShapeDtypeStruct((b, s // CHUNK, 1, D_RWKV), F32)]
            + [jax.ShapeDtypeStruct((b, s, D_RWKV), F32)] * 2
            + [jax.ShapeDtypeStruct((b, s, D_CONV), BF16)])
    out_specs = ([tok(D_RWKV)] * 5
                 + [pl.BlockSpec((1, tile // CHUNK, 1, D_RWKV), lambda i, j: (i, j, 0, 0))]
                 + [tok(D_RWKV)] * 2 + [tok(D_CONV)])
    return pl.pallas_call(
        _inproj_kernel,
        grid=(b, nt),
        in_specs=[tok(d), const(1, d), const(d, RW_COLS), const(d, 3 * D_CONV)]
                 + [const(*w.shape) for w in small]
                 + [const(CHUNK, CHUNK)],
        out_specs=out_specs,
        out_shape=outs,
        scratch_shapes=[pltpu.VMEM((SUBLANES, RW_COLS), F32), pltpu.VMEM((SUBLANES, D_CONV), F32)],
        compiler_params=pltpu.CompilerParams(dimension_semantics=("arbitrary", "arbitrary"),
                                             vmem_limit_bytes=VMEM_LIMIT["inproj"]),
        name="inproj",
    )(x, nm, wrw, wconv, *small, tri)


def _pair_expand(x):
    even = lax.broadcasted_iota(jnp.int32, x.shape, 1) < RWKV_HEAD
    zero = jnp.zeros_like(x)
    return jnp.concatenate([jnp.where(even, x, zero), jnp.where(even, zero, x)], axis=0)


def _rows(*parts):
    return jnp.concatenate(parts, axis=0)


def _wkv_masks():
    row = lax.broadcasted_iota(jnp.int32, (CHUNK, PAIR), 0)
    col = lax.broadcasted_iota(jnp.int32, (CHUNK, PAIR), 1) % RWKV_HEAD
    row2 = lax.broadcasted_iota(jnp.int32, (PAIR, PAIR), 0) < RWKV_HEAD
    col2 = lax.broadcasted_iota(jnp.int32, (PAIR, PAIR), 1) < RWKV_HEAD
    return dict(strict=col < row, incl=col <= row, eye=(col == row).astype(F32),
                same_head=row2 == col2)


def _wkv_prep(refs, wc_ref, c, inst, m, out):
    rows = slice(c * CHUNK, (c + 1) * CHUNK)
    load = lambda ref: [ref[b, rows, p * PAIR:(p + 1) * PAIR] for b, p in inst]
    r, a, bt, kt, v = (load(ref) for ref in refs)
    n = range(len(inst))
    ex = lambda x: _pair_expand(x.astype(BF16))
    g = [_bdot_nt(_rows(a[i], r[i]), _rows(_pair_expand(bt[i]), _pair_expand(kt[i]))) for i in n]
    yield
    lab = [jnp.where(m["strict"], g[i][:CHUNK, :PAIR], 0.0) for i in n]
    lak = [jnp.where(m["strict"], g[i][:CHUNK, PAIR:], 0.0) for i in n]
    mrb = [jnp.where(m["incl"], g[i][CHUNK:, :PAIR], 0.0) for i in n]
    mrk = [jnp.where(m["incl"], g[i][CHUNK:, PAIR:], 0.0) for i in n]
    npow = [_bdot(lab[i], ex(lab[i])) for i in n]
    tinv = [m["eye"] + lab[i] for i in n]
    lmv = [_bdot(_rows(lak[i], mrk[i]), _pair_expand(v[i])) for i in n]
    yield
    for _ in range(4):
        prod = [_bdot(_rows(tinv[i], npow[i]), ex(npow[i])) for i in n]
        tinv = [tinv[i] + prod[i][:CHUNK] for i in n]
        npow = [prod[i][CHUNK:] for i in n]
        yield
    tinv = [tinv[i] + _bdot(tinv[i], ex(npow[i])) for i in n]
    yield
    pq = [_bdot(tinv[i], jnp.concatenate([_pair_expand(a[i]), ex(lmv[i][:CHUNK])], axis=1))
          for i in n]
    bkt = [_rows(bt[i], kt[i]).astype(F32).T.astype(BF16) for i in n]
    wcol = [jnp.broadcast_to(wc_ref[b, c, :, p * PAIR:(p + 1) * PAIR], (PAIR, PAIR)).T
            for b, p in inst]
    out.update(p=[pq[i][:, :PAIR].astype(BF16) for i in n], q=[pq[i][:, PAIR:] for i in n],
               r=r, v=v, bkt=bkt, wcol=wcol, mrb=mrb, mv=[lmv[i][CHUNK:] for i in n])
    yield


def _wkv_chain(pre, z, y_ref, c, inst, m):
    rows = slice(c * CHUNK, (c + 1) * CHUNK)
    n = range(len(inst))
    pr = [_bdot(_rows(pre["p"][i], pre["r"][i]), z[i]) for i in n]
    yield
    u = [(pr[i][:CHUNK] + pre["q"][i]).astype(BF16) for i in n]
    for i in n:
        z[i] = pre["wcol"][i] * (z[i] + jnp.where(
            m["same_head"], _bdot(pre["bkt"][i], _rows(u[i], pre["v"][i])), 0.0))
    yield
    for i, (b, p) in enumerate(inst):
        y_ref[b, rows, p * PAIR:(p + 1) * PAIR] = (
            pr[i][CHUNK:] + _bdot(pre["mrb"][i], _pair_expand(u[i])) + pre["mv"][i])
    yield


def _interleave(primary, secondary, ratio):
    for _ in primary:
        for _ in itertools.islice(secondary, ratio):
            pass
    for _ in secondary:
        pass


def _cast_stages(src_refs, dst_refs):
    for src, dst in zip(src_refs, dst_refs):
        dst[...] = src[...].astype(dst.dtype)
        yield


def _wkv_kernel(rt_ref, at_ref, bt_ref, kt_ref, v_ref, wc_ref, *rest):
    nw = (len(rest) - 2) // 2
    w_refs, y_ref, wb_refs, state = rest[:nw], rest[nw], rest[nw + 1:-1], rest[-1]

    @pl.when(pl.program_id(0) == 0)
    def _():
        state[...] = jnp.zeros_like(state)

    casts = _cast_stages(w_refs, wb_refs)
    nb, t, _ = rt_ref.shape
    nc = t // CHUNK
    inst = [(b, p) for b in range(nb) for p in range(N_PAIRS)]
    m = _wkv_masks()
    refs = (rt_ref, at_ref, bt_ref, kt_ref, v_ref)
    z = [state[b, p] for b, p in inst]
    pre = [dict() for _ in range(nc)]
    for _ in _wkv_prep(refs, wc_ref, 0, inst, m, pre[0]):
        pass
    for c in range(nc):
        nxt = _wkv_prep(refs, wc_ref, c + 1, inst, m, pre[c + 1]) if c + 1 < nc else iter(())
        _interleave(_wkv_chain(pre[c], z, y_ref, c, inst, m), nxt, PREP_STAGES_PER_CHAIN_STAGE)
        next(casts, None)
    for _ in casts:
        pass
    for i, (b, p) in enumerate(inst):
        state[b, p] = z[i]


def _wkv(rt, at, bt, kt, v, wc, weights, tile):
    b, s, d = rt.shape
    steps = s // tile
    tok = pl.BlockSpec((b, tile, d), lambda j: (0, j, 0))
    wspecs = [pl.BlockSpec((w.shape[0] // steps, w.shape[1]), lambda j: (j, 0)) for w in weights]
    out = pl.pallas_call(
        _wkv_kernel,
        grid=(steps,),
        in_specs=[tok] * 5 + [pl.BlockSpec((b, tile // CHUNK, 1, d), lambda j: (0, j, 0, 0))] + wspecs,
        out_specs=[tok] + wspecs,
        out_shape=[jax.ShapeDtypeStruct((b, s, d), F32)]
                  + [jax.ShapeDtypeStruct(w.shape, BF16) for w in weights],
        scratch_shapes=[pltpu.VMEM((b, N_PAIRS, PAIR, PAIR), F32)],
        compiler_params=pltpu.CompilerParams(dimension_semantics=("arbitrary",),
                                             vmem_limit_bytes=VMEM_LIMIT["wkv"]),
        name="wkv",
    )(rt, at, bt, kt, v, wc, *weights)
    return out[0], out[1:]


def _mix_kernel(x_ref, y_ref, g_ref, bonus_ref, ob_ref, k_ref, v_ref, lnw_ref, nm_ref, bg_ref,
                wgate_ref, wpa_ref, wpb_ref, wom_ref, nx_ref, wq_ref, wxo_ref, out_ref):
    x = x_ref[0]
    h = _rms(x, nm_ref[...]).astype(BF16)
    sg = _sigmoid(jnp.dot(h, wgate_ref[...], preferred_element_type=F32) + bg_ref[...])

    y = y_ref[0]
    yc = y - _head_sums(y) * (1.0 / RWKV_HEAD)
    var = _head_sums(yc * yc) * (1.0 / RWKV_HEAD)
    yn = yc * lax.rsqrt(var + LN_X_EPS)
    oa = (yn * lnw_ref[...] + bonus_ref[0]) * g_ref[0]

    merged = (sg[:, :D_MODEL] * _bdot(oa, wpa_ref[...])
              + sg[:, D_MODEL:] * jnp.dot(ob_ref[0], wpb_ref[...], preferred_element_type=F32))
    x1 = x + _bdot(merged, wom_ref[...])

    q = _bdot(_rms(x1, nx_ref[...]), wq_ref[...]) * (XHEAD_DIM ** -0.5)
    cols = [slice(hh * XHEAD_DIM, (hh + 1) * XHEAD_DIM) for hh in range(N_XHEADS)]
    sc = [_bdot_nt(q[:, c], k_ref[0, :, c]) for c in cols]
    e = [jnp.exp(s - jnp.max(s, axis=-1, keepdims=True)) for s in sc]
    pr = [ee / jnp.sum(ee, axis=-1, keepdims=True) for ee in e]
    o = jnp.concatenate([_bdot(p, v_ref[0, :, c]) for p, c in zip(pr, cols)], axis=-1)
    out_ref[0] = x1 + _bdot(o, wxo_ref[...])


def _mix(x, y, g, bonus, ob, kmem, vmem, lnw, nm, bg, wgate, wpa, wpb, wom, nx, wq, wxo, tile):
    b, s, d = x.shape
    n_mem = kmem.shape[1]
    const = _const_spec
    tok = lambda w: pl.BlockSpec((1, tile, w), lambda i, j: (i, j, 0))
    memspec = pl.BlockSpec((1, n_mem, d), lambda i, j: (i, 0, 0))
    return pl.pallas_call(
        _mix_kernel,
        grid=(b, s // tile),
        in_specs=[tok(d), tok(D_RWKV), tok(D_RWKV), tok(D_RWKV), tok(D_CONV), memspec, memspec,
                  const(1, D_RWKV), const(1, d), const(1, 2 * d), const(d, 2 * d),
                  const(D_RWKV, d), const(D_CONV, d), const(d, d), const(1, d), const(d, d),
                  const(d, d)],
        out_specs=tok(d),
        out_shape=jax.ShapeDtypeStruct((b, s, d), F32),
        compiler_params=pltpu.CompilerParams(dimension_semantics=("arbitrary", "arbitrary"),
                                             vmem_limit_bytes=VMEM_LIMIT["mix"]),
        name="mix",
    )(x, y, g, bonus, ob, kmem, vmem, lnw, nm, bg, wgate, wpa, wpb, wom, nx, wq, wxo)


def _mlp_kernel(x_ref, nm_ref, wup_ref, wdown_ref, nf_ref, out_ref):
    x = x_ref[...]
    h = _rms(x, nm_ref[...]).astype(BF16)
    acc = x
    for c in range(D_FF // D_MODEL):
        cols = slice(c * D_MODEL, (c + 1) * D_MODEL)
        up = jnp.maximum(jnp.dot(h, wup_ref[:, cols], preferred_element_type=F32), 0.0)
        acc = acc + _bdot(up * up, wdown_ref[cols, :])
    out_ref[...] = _rms(acc, nf_ref[...])


def _mlp(x, nm, wup, wdown, nf, tile):
    n, d = x.shape
    const = _const_spec
    tok = pl.BlockSpec((tile, d), lambda i: (i, 0))
    return pl.pallas_call(
        _mlp_kernel,
        grid=(n // tile,),
        in_specs=[tok, const(1, d), const(d, D_FF), const(D_FF, d), const(1, d)],
        out_specs=tok,
        out_shape=jax.ShapeDtypeStruct((n, d), F32),
        compiler_params=pltpu.CompilerParams(dimension_semantics=("arbitrary",),
                                             vmem_limit_bytes=VMEM_LIMIT["mlp"]),
        name="mlp",
    )(x, nm, wup, wdown, nf)


def _layer(x, mem, norm_mix, w_in, b_gate, mu_shift, w0, w_lora_w, a0, w_lora_a, w_lora_g,
           k_k, k_a, r_k, ln_x_w, ln_x_b, conv_w, w_proj_a, w_proj_b, w_out_mix, norm_xattn,
           norm_mem, w_q, w_kv, w_xo, norm_mlp, w_up, w_down, norm_final):
    bsz, s, d = x.shape
    c_conv = RW_COLS - 2 * LANES + GATE_LORA
    c_gate = c_conv + 3 * D_CONV
    w_in = w_in.astype(BF16)
    wconv = w_in[:, c_conv:c_gate]
    wgate = w_in[:, c_gate:]
    small = (mu_shift[None], w0[None], a0[None], k_k[None], k_a[None], r_k, ln_x_b[None], conv_w,
             w_lora_w, w_lora_a, w_lora_g)
    step = jnp.arange(CHUNK)
    tri = (step[None, :] <= step[:, None]).astype(BF16)

    rt, at, bt, kt, v, wc, g, bonus, ob = _inproj(
        x, norm_mix[None], w_in, wconv, small, tri, tile=TILE_INPROJ)
    later = (w_proj_a, w_proj_b, w_out_mix, w_q, w_xo, w_up, w_down, w_kv)
    y, (wpa, wpb, wom, wq, wxo, wup, wdown, wkv) = _wkv(
        rt, at, bt, kt, v, wc, later, tile=TILE_WKV)
    kmem, vmem = _memkv(mem, norm_mem[None], wkv)
    x = _mix(x, y, g, bonus, ob, kmem, vmem, ln_x_w[None], norm_mix[None], b_gate[None], wgate,
             wpa, wpb, wom, norm_xattn[None], wq, wxo, tile=TILE_MIX)
    return _mlp(x.reshape(bsz * s, d), norm_mlp[None], wup, wdown, norm_final[None],
                tile=TILE_MLP).reshape(bsz, s, d)


def kernel(x, mem, norm_mix, w_in, b_gate, mu_shift, w0, w_lora_w, a0, w_lora_a, w_lora_g, k_k, k_a, r_k, ln_x_w, ln_x_b, conv_w, w_proj_a, w_proj_b, w_out_mix, norm_xattn, norm_mem, w_q, w_kv, w_xo, norm_mlp, w_up, w_down, norm_final):
    assert w_in.shape[0] == 1, "the MLP kernel fuses the final norm: single-layer trunk only"
    per_layer = (norm_mix, w_in, b_gate, mu_shift, w0, w_lora_w, a0, w_lora_a, w_lora_g, k_k, k_a,
                 r_k, ln_x_w, ln_x_b, conv_w, w_proj_a, w_proj_b, w_out_mix, norm_xattn, norm_mem,
                 w_q, w_kv, w_xo, norm_mlp, w_up, w_down)
    return _layer(x, mem, *(p[0] for p in per_layer), norm_final)
```

```python
import functools
import itertools
import math

import jax
import jax.numpy as jnp
from jax import lax
from jax.experimental import pallas as pl
from jax.experimental.pallas import tpu as pltpu

F32 = jnp.float32
BF16 = jnp.bfloat16

D_MODEL = 1024
D_RWKV = 512
RWKV_HEAD = 64
DECAY_LORA = 64
AAA_LORA = 64
GATE_LORA = 160
LN_X_EPS = 64e-5
D_CONV = 512
N_XHEADS = 4
XHEAD_DIM = D_MODEL // N_XHEADS
D_FF = 4 * D_MODEL
RMS_EPS = 1e-6
EXP_MINUS_HALF = math.exp(-0.5)

LANES = 128
SUBLANES = 8
CHUNK = 64
PAIR = 2 * RWKV_HEAD
N_PAIRS = D_RWKV // PAIR
WA_OFF = 3 * D_RWKV
GD_OFF = WA_OFF + DECAY_LORA + AAA_LORA
RW_COLS = GD_OFF + 2 * LANES
assert DECAY_LORA + AAA_LORA == LANES and GATE_LORA <= 2 * LANES
MIB = 1024 * 1024
VMEM_LIMIT = {"memkv": 12 * MIB, "inproj": 54 * MIB, "wkv": 32 * MIB, "mix": 56 * MIB,
              "mlp": 44 * MIB}

TILE_INPROJ = 1024
FEATURE_ROWS = 512
TILE_WKV = 8 * CHUNK
TILE_MIX = 1024
TILE_MLP = 1024
PREP_STAGES_PER_CHAIN_STAGE = 3


def _bdot(a, b):
    return jnp.dot(a.astype(BF16), b.astype(BF16), preferred_element_type=F32)


def _bdot_nt(a, b):
    return lax.dot_general(a.astype(BF16), b.astype(BF16), (((1,), (1,)), ((), ())),
                           preferred_element_type=F32)


def _split2(a):
    hi = a.astype(BF16)
    lo = (a - hi.astype(F32)).astype(BF16)
    return hi, lo


def _head_sums(a):
    low = lax.broadcasted_iota(jnp.int32, (a.shape[0], LANES), 1) < RWKV_HEAD
    out = []
    for j in range(a.shape[1] // LANES):
        slab = a[:, j * LANES:(j + 1) * LANES]
        s_lo = jnp.sum(jnp.where(low, slab, 0.0), axis=-1, keepdims=True)
        s_hi = jnp.sum(jnp.where(low, 0.0, slab), axis=-1, keepdims=True)
        out.append(jnp.where(low, s_lo, s_hi))
    return jnp.concatenate(out, axis=-1)


def _dot_sel_lhs(sel, a):
    hi, lo = _split2(a)
    d = functools.partial(jnp.dot, preferred_element_type=F32)
    return d(sel, hi) + d(sel, lo)


def _rms(x, g):
    return x * lax.rsqrt(jnp.mean(x * x, axis=-1, keepdims=True) + RMS_EPS) * g


def _sigmoid(x):
    return 1.0 / (1.0 + jnp.exp(-x))


def _shift_rows(x, carry_rows, n):
    nc = carry_rows.shape[0]
    out = pltpu.roll(x, n, 0)
    rows = lax.broadcasted_iota(jnp.int32, (nc, 1), 0)
    head = out[:nc]
    for j in range(n):
        head = jnp.where(rows == j, carry_rows[nc - n + j:nc - n + j + 1, :], head)
    return jnp.concatenate([head, out[nc:]], axis=0)


def _const_spec(*shape):
    return pl.BlockSpec(shape, lambda *_: (0,) * len(shape), pipeline_mode=pl.Buffered(1))


def _memkv_kernel(mem_ref, g_ref, w_ref, k_ref, v_ref):
    m = _rms(mem_ref[0], g_ref[...])
    kv = _bdot(m, w_ref[...])
    k_ref[0] = kv[:, :D_MODEL].astype(BF16)
    v_ref[0] = kv[:, D_MODEL:].astype(BF16)


def _memkv(mem, g, w_kv):
    b, n, d = mem.shape
    return pl.pallas_call(
        _memkv_kernel,
        grid=(b,),
        in_specs=[pl.BlockSpec((1, n, d), lambda i: (i, 0, 0)),
                  pl.BlockSpec((1, d), lambda i: (0, 0)),
                  pl.BlockSpec((d, 2 * d), lambda i: (0, 0))],
        out_specs=[pl.BlockSpec((1, n, d), lambda i: (i, 0, 0))] * 2,
        out_shape=[jax.ShapeDtypeStruct((b, n, d), BF16)] * 2,
        compiler_params=pltpu.CompilerParams(dimension_semantics=("arbitrary",),
                                             vmem_limit_bytes=VMEM_LIMIT["memkv"]),
        name="memkv",
    )(mem, g, w_kv)


def _inproj_kernel(x_ref, nm_ref, wrw_ref, wconv_ref,
                   mu_ref, w0_ref, a0_ref, kk_ref, ka_ref, rk_ref, lnb_ref, cw_ref,
                   ww_ref, wa_ref, wg_ref, tri_ref,
                   rt_ref, at_ref, bt_ref, kt_ref, v_ref, wc_ref,
                   g_ref, bonus_ref, ob_ref,
                   pcarry, ucarry, pbuf, pcbuf):
    @pl.when(pl.program_id(1) == 0)
    def _():
        pcarry[...] = jnp.zeros_like(pcarry)
        ucarry[...] = jnp.zeros_like(ucarry)

    t = x_ref.shape[1]
    h = _rms(x_ref[0], nm_ref[...]).astype(BF16)
    pcbuf[...] = jnp.dot(h, wconv_ref[...], preferred_element_type=F32)
    pbuf[...] = jnp.dot(h, wrw_ref[...], preferred_element_type=F32)

    mu = jnp.concatenate([mu_ref[...], jnp.zeros((1, RW_COLS - mu_ref.shape[1]), F32)], axis=1)
    zeros = lambda rows: jnp.zeros((rows, D_RWKV), BF16)
    ww = _rows(ww_ref[...].astype(BF16), zeros(LANES - DECAY_LORA))
    wa_w = _rows(zeros(DECAY_LORA), wa_ref[...].astype(BF16))
    wg = _rows(wg_ref[...].astype(BF16), zeros(2 * LANES - GATE_LORA))
    rk = jnp.concatenate([rk_ref[i:i + 1, :] for i in range(rk_ref.shape[0])], axis=1)
    tri = tri_ref[...]

    for lo in range(0, t, FEATURE_ROWS):
        rows = slice(lo, lo + FEATURE_ROWS)
        prev = slice(lo - SUBLANES, lo)

        pc = pcbuf[rows, :]
        u = pc[:, D_CONV:2 * D_CONV] * pc[:, 2 * D_CONV:]
        if lo == 0:
            uc = ucarry[...]
        else:
            uc = pcbuf[prev, D_CONV:2 * D_CONV] * pcbuf[prev, 2 * D_CONV:]
        conv = cw_ref[0] * _shift_rows(u, uc, 2) + cw_ref[1] * _shift_rows(u, uc, 1) + cw_ref[2] * u
        ob_ref[0, rows, :] = (pc[:, :D_CONV] * conv).astype(BF16)

        p = pbuf[rows, :]
        ps = _shift_rows(p, pcarry[...] if lo == 0 else pbuf[prev, :], 1)
        xm = p + (ps - p) * mu
        r = xm[:, :D_RWKV]
        k = xm[:, D_RWKV:2 * D_RWKV]
        v = xm[:, 2 * D_RWKV:3 * D_RWKV]
        wa = xm[:, WA_OFF:GD_OFF]
        gd = xm[:, GD_OFF:]
        ld = -EXP_MINUS_HALF * _sigmoid(w0_ref[...] + _bdot(jnp.tanh(wa), ww))
        a = _sigmoid(a0_ref[...] + _bdot(wa, wa_w))
        g_ref[0, rows, :] = _bdot(_sigmoid(gd), wg)
        kk = k * kk_ref[...]
        kk = kk * lax.rsqrt(jnp.maximum(_head_sums(kk * kk), 1e-24))
        k2 = k * (1.0 + (a - 1.0) * ka_ref[...])
        bonus_ref[0, rows, :] = _head_sums(r * k2 * rk) * v + lnb_ref[...]
        v_ref[0, rows, :] = v.astype(BF16)
        na = -kk
        nb = kk * a
        for c in range(FEATURE_ROWS // CHUNK):
            sl = slice(c * CHUNK, (c + 1) * CHUNK)
            osl = slice(lo + c * CHUNK, lo + (c + 1) * CHUNK)
            ldc = ld[sl]
            cum = _dot_sel_lhs(tri, ldc)
            tot = cum[CHUNK - 1:CHUNK, :]
            e_inc = jnp.exp(cum)
            e_inv = 1.0 / e_inc
            e_prev = jnp.exp(cum - ldc)
            rt_ref[0, osl, :] = (r[sl] * e_inc).astype(BF16)
            at_ref[0, osl, :] = (na[sl] * e_prev).astype(BF16)
            bt_ref[0, osl, :] = (nb[sl] * e_inv).astype(BF16)
            kt_ref[0, osl, :] = (k2[sl] * e_inv).astype(BF16)
            wc_ref[0, lo // CHUNK + c] = jnp.exp(tot)

    ucarry[...] = pcbuf[t - SUBLANES:, D_CONV:2 * D_CONV] * pcbuf[t - SUBLANES:, 2 * D_CONV:]
    pcarry[...] = pbuf[t - SUBLANES:, :]


def _inproj(x, nm, wrw, wconv, small, tri, tile):
    b, s, d = x.shape
    nt = s // tile
    const = _const_spec
    tok = lambda w: pl.BlockSpec((1, tile, w), lambda i, j: (i, j, 0))
    outs = ([jax.ShapeDtypeStruct((b, s, D_RWKV), BF16)] * 5
            + [jax.ShapeDtypeStruct((b, s // CHUNK, 1, D_RWKV), F32)]
            + [jax.ShapeDtypeStruct((b, s, D_RWKV), F32)] * 2
            + [jax.ShapeDtypeStruct((b, s, D_CONV), BF16)])
    out_specs = ([tok(D_RWKV)] * 5
                 + [pl.BlockSpec((1, tile // CHUNK, 1, D_RWKV), lambda i, j: (i, j, 0, 0))]
                 + [tok(D_RWKV)] * 2 + [tok(D_CONV)])
    return pl.pallas_call(
        _inproj_kernel,
        grid=(b, nt),
        in_specs=[tok(d), const(1, d), const(d, RW_COLS), const(d, 3 * D_CONV)]
                 + [const(*w.shape) for w in small]
                 + [const(CHUNK, CHUNK)],
        out_specs=out_specs,
        out_shape=outs,
        scratch_shapes=[pltpu.VMEM((SUBLANES, RW_COLS), F32), pltpu.VMEM((SUBLANES, D_CONV), F32),
                        pltpu.VMEM((tile, RW_COLS), F32), pltpu.VMEM((tile, 3 * D_CONV), F32)],
        compiler_params=pltpu.CompilerParams(dimension_semantics=("arbitrary", "arbitrary"),
                                             vmem_limit_bytes=VMEM_LIMIT["inproj"]),
        name="inproj",
    )(x, nm, wrw, wconv, *small, tri)


def _pair_expand(x):
    even = lax.broadcasted_iota(jnp.int32, x.shape, 1) < RWKV_HEAD
    zero = jnp.zeros_like(x)
    return jnp.concatenate([jnp.where(even, x, zero), jnp.where(even, zero, x)], axis=0)


def _rows(*parts):
    return jnp.concatenate(parts, axis=0)


def _wkv_masks():
    row = lax.broadcasted_iota(jnp.int32, (CHUNK, PAIR), 0)
    col = lax.broadcasted_iota(jnp.int32, (CHUNK, PAIR), 1) % RWKV_HEAD
    row2 = lax.broadcasted_iota(jnp.int32, (PAIR, PAIR), 0) < RWKV_HEAD
    col2 = lax.broadcasted_iota(jnp.int32, (PAIR, PAIR), 1) < RWKV_HEAD
    return dict(strict=col < row, incl=col <= row, eye=(col == row).astype(F32),
                same_head=row2 == col2)


def _wkv_prep(refs, wc_ref, c, inst, m, out):
    rows = slice(c * CHUNK, (c + 1) * CHUNK)
    load = lambda ref: [ref[b, rows, p * PAIR:(p + 1) * PAIR] for b, p in inst]
    r, a, bt, kt, v = (load(ref) for ref in refs)
    n = range(len(inst))
    ex = lambda x: _pair_expand(x.astype(BF16))
    g = [_bdot_nt(_rows(a[i], r[i]), _rows(_pair_expand(bt[i]), _pair_expand(kt[i]))) for i in n]
    yield
    lab = [jnp.where(m["strict"], g[i][:CHUNK, :PAIR], 0.0) for i in n]
    lak = [jnp.where(m["strict"], g[i][:CHUNK, PAIR:], 0.0) for i in n]
    mrb = [jnp.where(m["incl"], g[i][CHUNK:, :PAIR], 0.0) for i in n]
    mrk = [jnp.where(m["incl"], g[i][CHUNK:, PAIR:], 0.0) for i in n]
    npow = [_bdot(lab[i], ex(lab[i])) for i in n]
    tinv = [m["eye"] + lab[i] for i in n]
    lmv = [_bdot(_rows(lak[i], mrk[i]), _pair_expand(v[i])) for i in n]
    yield
    for _ in range(4):
        prod = [_bdot(_rows(tinv[i], npow[i]), ex(npow[i])) for i in n]
        tinv = [tinv[i] + prod[i][:CHUNK] for i in n]
        npow = [prod[i][CHUNK:] for i in n]
        yield
    tinv = [tinv[i] + _bdot(tinv[i], ex(npow[i])) for i in n]
    yield
    pq = [_bdot(tinv[i], jnp.concatenate([_pair_expand(a[i]), ex(lmv[i][:CHUNK])], axis=1))
          for i in n]
    bkt = [_rows(bt[i], kt[i]).astype(F32).T.astype(BF16) for i in n]
    wcol = [jnp.broadcast_to(wc_ref[b, c, :, p * PAIR:(p + 1) * PAIR], (PAIR, PAIR)).T
            for b, p in inst]
    out.update(p=[pq[i][:, :PAIR].astype(BF16) for i in n], q=[pq[i][:, PAIR:] for i in n],
               r=r, v=v, bkt=bkt, wcol=wcol, mrb=mrb, mv=[lmv[i][CHUNK:] for i in n])
    yield


def _wkv_chain(pre, z, y_ref, c, inst, m):
    rows = slice(c * CHUNK, (c + 1) * CHUNK)
    n = range(len(inst))
    pr = [_bdot(_rows(pre["p"][i], pre["r"][i]), z[i]) for i in n]
    yield
    u = [(pr[i][:CHUNK] + pre["q"][i]).astype(BF16) for i in n]
    for i in n:
        z[i] = pre["wcol"][i] * (z[i] + jnp.where(
            m["same_head"], _bdot(pre["bkt"][i], _rows(u[i], pre["v"][i])), 0.0))
    yield
    for i, (b, p) in enumerate(inst):
        y_ref[b, rows, p * PAIR:(p + 1) * PAIR] = (
            pr[i][CHUNK:] + _bdot(pre["mrb"][i], _pair_expand(u[i])) + pre["mv"][i])
    yield


def _interleave(primary, secondary, ratio):
    for _ in primary:
        for _ in itertools.islice(secondary, ratio):
            pass
    for _ in secondary:
        pass


def _cast_stages(src_refs, dst_refs):
    for src, dst in zip(src_refs, dst_refs):
        dst[...] = src[...].astype(dst.dtype)
        yield


def _wkv_kernel(rt_ref, at_ref, bt_ref, kt_ref, v_ref, wc_ref, *rest):
    nw = (len(rest) - 2) // 2
    w_refs, y_ref, wb_refs, state = rest[:nw], rest[nw], rest[nw + 1:-1], rest[-1]

    @pl.when(pl.program_id(0) == 0)
    def _():
        state[...] = jnp.zeros_like(state)

    casts = _cast_stages(w_refs, wb_refs)
    nb, t, _ = rt_ref.shape
    nc = t // CHUNK
    inst = [(b, p) for b in range(nb) for p in range(N_PAIRS)]
    m = _wkv_masks()
    refs = (rt_ref, at_ref, bt_ref, kt_ref, v_ref)
    z = [state[b, p] for b, p in inst]
    pre = [dict() for _ in range(nc)]
    for _ in _wkv_prep(refs, wc_ref, 0, inst, m, pre[0]):
        pass
    for c in range(nc):
        nxt = _wkv_prep(refs, wc_ref, c + 1, inst, m, pre[c + 1]) if c + 1 < nc else iter(())
        _interleave(_wkv_chain(pre[c], z, y_ref, c, inst, m), nxt, PREP_STAGES_PER_CHAIN_STAGE)
        next(casts, None)
    for _ in casts:
        pass
    for i, (b, p) in enumerate(inst):
        state[b, p] = z[i]


def _wkv(rt, at, bt, kt, v, wc, weights, tile):
    b, s, d = rt.shape
    steps = s // tile
    tok = pl.BlockSpec((b, tile, d), lambda j: (0, j, 0))
    wspecs = [pl.BlockSpec((w.shape[0] // steps, w.shape[1]), lambda j: (j, 0)) for w in weights]
    out = pl.pallas_call(
        _wkv_kernel,
        grid=(steps,),
        in_specs=[tok] * 5 + [pl.BlockSpec((b, tile // CHUNK, 1, d), lambda j: (0, j, 0, 0))] + wspecs,
        out_specs=[tok] + wspecs,
        out_shape=[jax.ShapeDtypeStruct((b, s, d), F32)]
                  + [jax.ShapeDtypeStruct(w.shape, BF16) for w in weights],
        scratch_shapes=[pltpu.VMEM((b, N_PAIRS, PAIR, PAIR), F32)],
        compiler_params=pltpu.CompilerParams(dimension_semantics=("arbitrary",),
                                             vmem_limit_bytes=VMEM_LIMIT["wkv"]),
        name="wkv",
    )(rt, at, bt, kt, v, wc, *weights)
    return out[0], out[1:]


def _mix_kernel(x_ref, y_ref, g_ref, bonus_ref, ob_ref, k_ref, v_ref, lnw_ref, nm_ref, bg_ref,
                wgate_ref, wpa_ref, wpb_ref, wom_ref, nx_ref, wq_ref, wxo_ref, out_ref):
    x = x_ref[0]
    h = _rms(x, nm_ref[...]).astype(BF16)
    sg = _sigmoid(jnp.dot(h, wgate_ref[...], preferred_element_type=F32) + bg_ref[...])

    y = y_ref[0]
    yc = y - _head_sums(y) * (1.0 / RWKV_HEAD)
    var = _head_sums(yc * yc) * (1.0 / RWKV_HEAD)
    yn = yc * lax.rsqrt(var + LN_X_EPS)
    oa = (yn * lnw_ref[...] + bonus_ref[0]) * g_ref[0]

    merged = (sg[:, :D_MODEL] * _bdot(oa, wpa_ref[...])
              + sg[:, D_MODEL:] * jnp.dot(ob_ref[0], wpb_ref[...], preferred_element_type=F32))
    x1 = x + _bdot(merged, wom_ref[...])

    q = _bdot(_rms(x1, nx_ref[...]), wq_ref[...]) * (XHEAD_DIM ** -0.5)
    cols = [slice(hh * XHEAD_DIM, (hh + 1) * XHEAD_DIM) for hh in range(N_XHEADS)]
    sc = [_bdot_nt(q[:, c], k_ref[0, :, c]) for c in cols]
    e = [jnp.exp(s - jnp.max(s, axis=-1, keepdims=True)) for s in sc]
    pr = [ee / jnp.sum(ee, axis=-1, keepdims=True) for ee in e]
    o = jnp.concatenate([_bdot(p, v_ref[0, :, c]) for p, c in zip(pr, cols)], axis=-1)
    out_ref[0] = x1 + _bdot(o, wxo_ref[...])


def _mix(x, y, g, bonus, ob, kmem, vmem, lnw, nm, bg, wgate, wpa, wpb, wom, nx, wq, wxo, tile):
    b, s, d = x.shape
    n_mem = kmem.shape[1]
    const = _const_spec
    tok = lambda w: pl.BlockSpec((1, tile, w), lambda i, j: (i, j, 0))
    memspec = pl.BlockSpec((1, n_mem, d), lambda i, j: (i, 0, 0))
    return pl.pallas_call(
        _mix_kernel,
        grid=(b, s // tile),
        in_specs=[tok(d), tok(D_RWKV), tok(D_RWKV), tok(D_RWKV), tok(D_CONV), memspec, memspec,
                  const(1, D_RWKV), const(1, d), const(1, 2 * d), const(d, 2 * d),
                  const(D_RWKV, d), const(D_CONV, d), const(d, d), const(1, d), const(d, d),
                  const(d, d)],
        out_specs=tok(d),
        out_shape=jax.ShapeDtypeStruct((b, s, d), F32),
        compiler_params=pltpu.CompilerParams(dimension_semantics=("arbitrary", "arbitrary"),
                                             vmem_limit_bytes=VMEM_LIMIT["mix"]),
        name="mix",
    )(x, y, g, bonus, ob, kmem, vmem, lnw, nm, bg, wgate, wpa, wpb, wom, nx, wq, wxo)


def _mlp_kernel(x_ref, nm_ref, wup_ref, wdown_ref, nf_ref, out_ref):
    x = x_ref[...]
    h = _rms(x, nm_ref[...]).astype(BF16)
    acc = x
    for c in range(D_FF // D_MODEL):
        cols = slice(c * D_MODEL, (c + 1) * D_MODEL)
        up = jnp.maximum(jnp.dot(h, wup_ref[:, cols], preferred_element_type=F32), 0.0)
        acc = acc + _bdot(up * up, wdown_ref[cols, :])
    out_ref[...] = _rms(acc, nf_ref[...])


def _mlp(x, nm, wup, wdown, nf, tile):
    n, d = x.shape
    const = _const_spec
    tok = pl.BlockSpec((tile, d), lambda i: (i, 0))
    return pl.pallas_call(
        _mlp_kernel,
        grid=(n // tile,),
        in_specs=[tok, const(1, d), const(d, D_FF), const(D_FF, d), const(1, d)],
        out_specs=tok,
        out_shape=jax.ShapeDtypeStruct((n, d), F32),
        compiler_params=pltpu.CompilerParams(dimension_semantics=("arbitrary",),
                                             vmem_limit_bytes=VMEM_LIMIT["mlp"]),
        name="mlp",
    )(x, nm, wup, wdown, nf)


def _layer(x, mem, norm_mix, w_in, b_gate, mu_shift, w0, w_lora_w, a0, w_lora_a, w_lora_g,
           k_k, k_a, r_k, ln_x_w, ln_x_b, conv_w, w_proj_a, w_proj_b, w_out_mix, norm_xattn,
           norm_mem, w_q, w_kv, w_xo, norm_mlp, w_up, w_down, norm_final):
    bsz, s, d = x.shape
    c_conv = RW_COLS - 2 * LANES + GATE_LORA
    c_gate = c_conv + 3 * D_CONV
    w_in = w_in.astype(BF16)
    wconv = w_in[:, c_conv:c_gate]
    wgate = w_in[:, c_gate:]
    small = (mu_shift[None], w0[None], a0[None], k_k[None], k_a[None], r_k, ln_x_b[None], conv_w,
             w_lora_w, w_lora_a, w_lora_g)
    step = jnp.arange(CHUNK)
    tri = (step[None, :] <= step[:, None]).astype(BF16)

    rt, at, bt, kt, v, wc, g, bonus, ob = _inproj(
        x, norm_mix[None], w_in, wconv, small, tri, tile=TILE_INPROJ)
    later = (w_proj_a, w_proj_b, w_out_mix, w_q, w_xo, w_up, w_down, w_kv)
    y, (wpa, wpb, wom, wq, wxo, wup, wdown, wkv) = _wkv(
        rt, at, bt, kt, v, wc, later, tile=TILE_WKV)
    kmem, vmem = _memkv(mem, norm_mem[None], wkv)
    x = _mix(x, y, g, bonus, ob, kmem, vmem, ln_x_w[None], norm_mix[None], b_gate[None], wgate,
             wpa, wpb, wom, norm_xattn[None], wq, wxo, tile=TILE_MIX)
    return _mlp(x.reshape(bsz * s, d), norm_mlp[None], wup, wdown, norm_final[None],
                tile=TILE_MLP).reshape(bsz, s, d)


def kernel(x, mem, norm_mix, w_in, b_gate, mu_shift, w0, w_lora_w, a0, w_lora_a, w_lora_g, k_k, k_a, r_k, ln_x_w, ln_x_b, conv_w, w_proj_a, w_proj_b, w_out_mix, norm_xattn, norm_mem, w_q, w_kv, w_xo, norm_mlp, w_up, w_down, norm_final):
    assert w_in.shape[0] == 1, "the MLP kernel fuses the final norm: single-layer trunk only"
    per_layer = (norm_mix, w_in, b_gate, mu_shift, w0, w_lora_w, a0, w_lora_a, w_lora_g, k_k, k_a,
                 r_k, ln_x_w, ln_x_b, conv_w, w_proj_a, w_proj_b, w_out_mix, norm_xattn, norm_mem,
                 w_q, w_kv, w_xo, norm_mlp, w_up, w_down)
    return _layer(x, mem, *(p[0] for p in per_layer), norm_final)
```

```python
import functools
import itertools
import math

import jax
import jax.numpy as jnp
from jax import lax
from jax.experimental import pallas as pl
from jax.experimental.pallas import tpu as pltpu

F32 = jnp.float32
BF16 = jnp.bfloat16

D_MODEL = 1024
D_RWKV = 512
RWKV_HEAD = 64
DECAY_LORA = 64
AAA_LORA = 64
GATE_LORA = 160
LN_X_EPS = 64e-5
D_CONV = 512
N_XHEADS = 4
XHEAD_DIM = D_MODEL // N_XHEADS
D_FF = 4 * D_MODEL
RMS_EPS = 1e-6
EXP_MINUS_HALF = math.exp(-0.5)

LANES = 128
SUBLANES = 8
CHUNK = 64
PAIR = 2 * RWKV_HEAD
N_PAIRS = D_RWKV // PAIR
WA_OFF = 3 * D_RWKV
GD_OFF = WA_OFF + DECAY_LORA + AAA_LORA
RW_COLS = GD_OFF + 2 * LANES
assert DECAY_LORA + AAA_LORA == LANES and GATE_LORA <= 2 * LANES
MIB = 1024 * 1024
VMEM_LIMIT = {"memkv": 12 * MIB, "inproj": 54 * MIB, "wkv": 32 * MIB, "mix": 56 * MIB,
              "mlp": 44 * MIB}

TILE_INPROJ = 1024
FEATURE_ROWS = 512
TILE_WKV = 8 * CHUNK
TILE_MIX = 1024
TILE_MLP = 1024
PREP_STAGES_PER_CHAIN_STAGE = 3


def _bdot(a, b):
    return jnp.dot(a.astype(BF16), b.astype(BF16), preferred_element_type=F32)


def _bdot_nt(a, b):
    return lax.dot_general(a.astype(BF16), b.astype(BF16), (((1,), (1,)), ((), ())),
                           preferred_element_type=F32)


def _split2(a):
    hi = a.astype(BF16)
    lo = (a - hi.astype(F32)).astype(BF16)
    return hi, lo


def _head_sums(a):
    low = lax.broadcasted_iota(jnp.int32, (a.shape[0], LANES), 1) < RWKV_HEAD
    out = []
    for j in range(a.shape[1] // LANES):
        slab = a[:, j * LANES:(j + 1) * LANES]
        s_lo = jnp.sum(jnp.where(low, slab, 0.0), axis=-1, keepdims=True)
        s_hi = jnp.sum(jnp.where(low, 0.0, slab), axis=-1, keepdims=True)
        out.append(jnp.where(low, s_lo, s_hi))
    return jnp.concatenate(out, axis=-1)


def _dot_sel_lhs(sel, a):
    hi, lo = _split2(a)
    d = functools.partial(jnp.dot, preferred_element_type=F32)
    return d(sel, hi) + d(sel, lo)


def _rms(x, g):
    return x * lax.rsqrt(jnp.mean(x * x, axis=-1, keepdims=True) + RMS_EPS) * g


def _sigmoid(x):
    return 1.0 / (1.0 + jnp.exp(-x))


def _shift_rows(x, carry_rows, n):
    nc = carry_rows.shape[0]
    out = pltpu.roll(x, n, 0)
    rows = lax.broadcasted_iota(jnp.int32, (nc, 1), 0)
    head = out[:nc]
    for j in range(n):
        head = jnp.where(rows == j, carry_rows[nc - n + j:nc - n + j + 1, :], head)
    return jnp.concatenate([head, out[nc:]], axis=0)


def _const_spec(*shape):
    return pl.BlockSpec(shape, lambda *_: (0,) * len(shape), pipeline_mode=pl.Buffered(1))


def _memkv_kernel(mem_ref, g_ref, w_ref, k_ref, v_ref):
    m = _rms(mem_ref[0], g_ref[...])
    kv = _bdot(m, w_ref[...])
    k_ref[0] = kv[:, :D_MODEL].astype(BF16)
    v_ref[0] = kv[:, D_MODEL:].astype(BF16)


def _memkv(mem, g, w_kv):
    b, n, d = mem.shape
    return pl.pallas_call(
        _memkv_kernel,
        grid=(b,),
        in_specs=[pl.BlockSpec((1, n, d), lambda i: (i, 0, 0)),
                  pl.BlockSpec((1, d), lambda i: (0, 0)),
                  pl.BlockSpec((d, 2 * d), lambda i: (0, 0))],
        out_specs=[pl.BlockSpec((1, n, d), lambda i: (i, 0, 0))] * 2,
        out_shape=[jax.ShapeDtypeStruct((b, n, d), BF16)] * 2,
        compiler_params=pltpu.CompilerParams(dimension_semantics=("arbitrary",),
                                             vmem_limit_bytes=VMEM_LIMIT["memkv"]),
        name="memkv",
    )(mem, g, w_kv)


def _inproj_kernel(x_ref, nm_ref, wrw_ref, wconv_ref,
                   mu_ref, w0_ref, a0_ref, kk_ref, ka_ref, rk_ref, lnb_ref, cw_ref,
                   ww_ref, wa_ref, wg_ref, tri_ref,
                   rt_ref, at_ref, bt_ref, kt_ref, v_ref, wc_ref,
                   g_ref, bonus_ref, ob_ref,
                   pcarry, ucarry, pbuf, pcbuf):
    @pl.when(pl.program_id(1) == 0)
    def _():
        pcarry[...] = jnp.zeros_like(pcarry)
        ucarry[...] = jnp.zeros_like(ucarry)

    t = x_ref.shape[1]
    h = _rms(x_ref[0], nm_ref[...]).astype(BF16)
    pcbuf[...] = jnp.dot(h, wconv_ref[...], preferred_element_type=F32)
    pbuf[...] = jnp.dot(h, wrw_ref[...], preferred_element_type=F32)

    mu = jnp.concatenate([mu_ref[...], jnp.zeros((1, RW_COLS - mu_ref.shape[1]), F32)], axis=1)
    zeros = lambda rows: jnp.zeros((rows, D_RWKV), BF16)
    ww = _rows(ww_ref[...].astype(BF16), zeros(LANES - DECAY_LORA))
    wa_w = _rows(zeros(DECAY_LORA), wa_ref[...].astype(BF16))
    wg = _rows(wg_ref[...].astype(BF16), zeros(2 * LANES - GATE_LORA))
    rk = jnp.concatenate([rk_ref[i:i + 1, :] for i in range(rk_ref.shape[0])], axis=1)
    tri = tri_ref[...]

    for lo in range(0, t, FEATURE_ROWS):
        rows = slice(lo, lo + FEATURE_ROWS)
        prev = slice(lo - SUBLANES, lo)

        pc = pcbuf[rows, :]
        u = pc[:, D_CONV:2 * D_CONV] * pc[:, 2 * D_CONV:]
        if lo == 0:
            uc = ucarry[...]
        else:
            uc = pcbuf[prev, D_CONV:2 * D_CONV] * pcbuf[prev, 2 * D_CONV:]
        conv = cw_ref[0] * _shift_rows(u, uc, 2) + cw_ref[1] * _shift_rows(u, uc, 1) + cw_ref[2] * u
        ob_ref[0, rows, :] = (pc[:, :D_CONV] * conv).astype(BF16)

        p = pbuf[rows, :]
        ps = _shift_rows(p, pcarry[...] if lo == 0 else pbuf[prev, :], 1)
        xm = p + (ps - p) * mu
        r = xm[:, :D_RWKV]
        k = xm[:, D_RWKV:2 * D_RWKV]
        v = xm[:, 2 * D_RWKV:3 * D_RWKV]
        wa = xm[:, WA_OFF:GD_OFF]
        gd = xm[:, GD_OFF:]
        ld = -EXP_MINUS_HALF * _sigmoid(w0_ref[...] + _bdot(jnp.tanh(wa), ww))
        a = _sigmoid(a0_ref[...] + _bdot(wa, wa_w))
        g_ref[0, rows, :] = _bdot(_sigmoid(gd), wg)
        kk = k * kk_ref[...]
        kk = kk * lax.rsqrt(jnp.maximum(_head_sums(kk * kk), 1e-24))
        k2 = k * (1.0 + (a - 1.0) * ka_ref[...])
        bonus_ref[0, rows, :] = _head_sums(r * k2 * rk) * v + lnb_ref[...]
        v_ref[0, rows, :] = v.astype(BF16)
        na = -kk
        nb = kk * a
        for c in range(FEATURE_ROWS // CHUNK):
            sl = slice(c * CHUNK, (c + 1) * CHUNK)
            osl = slice(lo + c * CHUNK, lo + (c + 1) * CHUNK)
            ldc = ld[sl]
            cum = _dot_sel_lhs(tri, ldc)
            tot = cum[CHUNK - 1:CHUNK, :]
            e_inc = jnp.exp(cum)
            e_inv = 1.0 / e_inc
            e_prev = jnp.exp(cum - ldc)
            rt_ref[0, osl, :] = (r[sl] * e_inc).astype(BF16)
            at_ref[0, osl, :] = (na[sl] * e_prev).astype(BF16)
            bt_ref[0, osl, :] = (nb[sl] * e_inv).astype(BF16)
            kt_ref[0, osl, :] = (k2[sl] * e_inv).astype(BF16)
            wc_ref[0, lo // CHUNK + c] = jnp.exp(tot)

    ucarry[...] = pcbuf[t - SUBLANES:, D_CONV:2 * D_CONV] * pcbuf[t - SUBLANES:, 2 * D_CONV:]
    pcarry[...] = pbuf[t - SUBLANES:, :]


def _inproj(x, nm, wrw, wconv, small, tri, tile):
    b, s, d = x.shape
    nt = s // tile
    const = _const_spec
    tok = lambda w: pl.BlockSpec((1, tile, w), lambda i, j: (i, j, 0))
    outs = ([jax.ShapeDtypeStruct((b, s, D_RWKV), BF16)] * 5
            + [jax.ShapeDtypeStruct((b, s // CHUNK, 1, D_RWKV), F32)]
            + [jax.ShapeDtypeStruct((b, s, D_RWKV), F32)] * 2
            + [jax.ShapeDtypeStruct((b, s, D_CONV), BF16)])
    out_specs = ([tok(D_RWKV)] * 5
                 + [pl.BlockSpec((1, tile // CHUNK, 1, D_RWKV), lambda i, j: (i, j, 0, 0))]
                 + [tok(D_RWKV)] * 2 + [tok(D_CONV)])
    return pl.pallas_call(
        _inproj_kernel,
        grid=(b, nt),
        in_specs=[tok(d), const(1, d), const(d, RW_COLS), const(d, 3 * D_CONV)]
                 + [const(*w.shape) for w in small]
                 + [const(CHUNK, CHUNK)],
        out_specs=out_specs,
        out_shape=outs,
        scratch_shapes=[pltpu.VMEM((SUBLANES, RW_COLS), F32), pltpu.VMEM((SUBLANES, D_CONV), F32),
                        pltpu.VMEM((tile, RW_COLS), F32), pltpu.VMEM((tile, 3 * D_CONV), F32)],
        compiler_params=pltpu.CompilerParams(dimension_semantics=("arbitrary", "arbitrary"),
                                             vmem_limit_bytes=VMEM_LIMIT["inproj"]),
        name="inproj",
    )(x, nm, wrw, wconv, *small, tri)


def _pair_expand(x):
    even = lax.broadcasted_iota(jnp.int32, x.shape, 1) < RWKV_HEAD
    zero = jnp.zeros_like(x)
    return jnp.concatenate([jnp.where(even, x, zero), jnp.where(even, zero, x)], axis=0)


def _rows(*parts):
    return jnp.concatenate(parts, axis=0)


def _wkv_masks():
    row = lax.broadcasted_iota(jnp.int32, (CHUNK, PAIR), 0)
    col = lax.broadcasted_iota(jnp.int32, (CHUNK, PAIR), 1) % RWKV_HEAD
    row2 = lax.broadcasted_iota(jnp.int32, (PAIR, PAIR), 0) < RWKV_HEAD
    col2 = lax.broadcasted_iota(jnp.int32, (PAIR, PAIR), 1) < RWKV_HEAD
    return dict(strict=col < row, incl=col <= row, eye=(col == row).astype(F32),
                same_head=row2 == col2)


def _wkv_prep(refs, wc_ref, c, inst, m, out):
    rows = slice(c * CHUNK, (c + 1) * CHUNK)
    load = lambda ref: [ref[b, rows, p * PAIR:(p + 1) * PAIR] for b, p in inst]
    r, a, bt, kt, v = (load(ref) for ref in refs)
    n = range(len(inst))
    ex = lambda x: _pair_expand(x.astype(BF16))
    g = [_bdot_nt(_rows(a[i], r[i]), _rows(_pair_expand(bt[i]), _pair_expand(kt[i]))) for i in n]
    yield
    lab = [jnp.where(m["strict"], g[i][:CHUNK, :PAIR], 0.0) for i in n]
    lak = [jnp.where(m["strict"], g[i][:CHUNK, PAIR:], 0.0) for i in n]
    mrb = [jnp.where(m["incl"], g[i][CHUNK:, :PAIR], 0.0) for i in n]
    mrk = [jnp.where(m["incl"], g[i][CHUNK:, PAIR:], 0.0) for i in n]
    npow = [_bdot(lab[i], ex(lab[i])) for i in n]
    tinv = [m["eye"] + lab[i] for i in n]
    lmv = [_bdot(_rows(lak[i], mrk[i]), _pair_expand(v[i])) for i in n]
    yield
    for _ in range(4):
        prod = [_bdot(_rows(tinv[i], npow[i]), ex(npow[i])) for i in n]
        tinv = [tinv[i] + prod[i][:CHUNK] for i in n]
        npow = [prod[i][CHUNK:] for i in n]
        yield
    tinv = [tinv[i] + _bdot(tinv[i], ex(npow[i])) for i in n]
    yield
    pq = [_bdot(tinv[i], jnp.concatenate([_pair_expand(a[i]), ex(lmv[i][:CHUNK])], axis=1))
          for i in n]
    bkt = [_rows(bt[i], kt[i]).astype(F32).T.astype(BF16) for i in n]
    wcol = [jnp.broadcast_to(wc_ref[b, c, :, p * PAIR:(p + 1) * PAIR], (PAIR, PAIR)).T
            for b, p in inst]
    out.update(p=[pq[i][:, :PAIR].astype(BF16) for i in n], q=[pq[i][:, PAIR:] for i in n],
               r=r, v=v, bkt=bkt, wcol=wcol, mrb=mrb, mv=[lmv[i][CHUNK:] for i in n])
    yield


def _wkv_chain(pre, z, y_ref, c, inst, m):
    rows = slice(c * CHUNK, (c + 1) * CHUNK)
    n = range(len(inst))
    pr = [_bdot(_rows(pre["p"][i], pre["r"][i]), z[i]) for i in n]
    yield
    u = [(pr[i][:CHUNK] + pre["q"][i]).astype(BF16) for i in n]
    for i in n:
        z[i] = pre["wcol"][i] * (z[i] + jnp.where(
            m["same_head"], _bdot(pre["bkt"][i], _rows(u[i], pre["v"][i])), 0.0))
    yield
    for i, (b, p) in enumerate(inst):
        y_ref[b, rows, p * PAIR:(p + 1) * PAIR] = (
            pr[i][CHUNK:] + _bdot(pre["mrb"][i], _pair_expand(u[i])) + pre["mv"][i])
    yield


def _interleave(primary, secondary, ratio):
    for _ in primary:
        for _ in itertools.islice(secondary, ratio):
            pass
    for _ in secondary:
        pass


def _cast_stages(src_refs, dst_refs):
    for src, dst in zip(src_refs, dst_refs):
        dst[...] = src[...].astype(dst.dtype)
        yield


def _wkv_kernel(rt_ref, at_ref, bt_ref, kt_ref, v_ref, wc_ref, *rest):
    nw = (len(rest) - 2) // 2
    w_refs, y_ref, wb_refs, state = rest[:nw], rest[nw], rest[nw + 1:-1], rest[-1]

    @pl.when(pl.program_id(0) == 0)
    def _():
        state[...] = jnp.zeros_like(state)

    casts = _cast_stages(w_refs, wb_refs)
    nb, t, _ = rt_ref.shape
    nc = t // CHUNK
    inst = [(b, p) for b in range(nb) for p in range(N_PAIRS)]
    m = _wkv_masks()
    refs = (rt_ref, at_ref, bt_ref, kt_ref, v_ref)
    z = [state[b, p] for b, p in inst]
    pre = [dict() for _ in range(nc)]
    for _ in _wkv_prep(refs, wc_ref, 0, inst, m, pre[0]):
        pass
    for c in range(nc):
        nxt = _wkv_prep(refs, wc_ref, c + 1, inst, m, pre[c + 1]) if c + 1 < nc else iter(())
        _interleave(_wkv_chain(pre[c], z, y_ref, c, inst, m), nxt, PREP_STAGES_PER_CHAIN_STAGE)
        next(casts, None)
    for _ in casts:
        pass
    for i, (b, p) in enumerate(inst):
        state[b, p] = z[i]


def _wkv(rt, at, bt, kt, v, wc, weights, tile):
    b, s, d = rt.shape
    steps = s // tile
    tok = pl.BlockSpec((b, tile, d), lambda j: (0, j, 0))
    wspecs = [pl.BlockSpec((w.shape[0] // steps, w.shape[1]), lambda j: (j, 0)) for w in weights]
    out = pl.pallas_call(
        _wkv_kernel,
        grid=(steps,),
        in_specs=[tok] * 5 + [pl.BlockSpec((b, tile // CHUNK, 1, d), lambda j: (0, j, 0, 0))] + wspecs,
        out_specs=[tok] + wspecs,
        out_shape=[jax.ShapeDtypeStruct((b, s, d), F32)]
                  + [jax.ShapeDtypeStruct(w.shape, BF16) for w in weights],
        scratch_shapes=[pltpu.VMEM((b, N_PAIRS, PAIR, PAIR), F32)],
        compiler_params=pltpu.CompilerParams(dimension_semantics=("arbitrary",),
                                             vmem_limit_bytes=VMEM_LIMIT["wkv"]),
        name="wkv",
    )(rt, at, bt, kt, v, wc, *weights)
    return out[0], out[1:]


def _mix_kernel(x_ref, y_ref, g_ref, bonus_ref, ob_ref, k_ref, v_ref, lnw_ref, nm_ref, bg_ref,
                wgate_ref, wpa_ref, wpb_ref, wom_ref, nx_ref, wq_ref, wxo_ref, out_ref):
    x = x_ref[0]
    h = _rms(x, nm_ref[...]).astype(BF16)
    sg = _sigmoid(jnp.dot(h, wgate_ref[...], preferred_element_type=F32) + bg_ref[...])

    y = y_ref[0]
    yc = y - _head_sums(y) * (1.0 / RWKV_HEAD)
    var = _head_sums(yc * yc) * (1.0 / RWKV_HEAD)
    yn = yc * lax.rsqrt(var + LN_X_EPS)
    oa = (yn * lnw_ref[...] + bonus_ref[0]) * g_ref[0]

    merged = (sg[:, :D_MODEL] * _bdot(oa, wpa_ref[...])
              + sg[:, D_MODEL:] * jnp.dot(ob_ref[0], wpb_ref[...], preferred_element_type=F32))
    x1 = x + _bdot(merged, wom_ref[...])

    q = _bdot(_rms(x1, nx_ref[...]), wq_ref[...]) * (XHEAD_DIM ** -0.5)
    cols = [slice(hh * XHEAD_DIM, (hh + 1) * XHEAD_DIM) for hh in range(N_XHEADS)]
    sc = [_bdot_nt(q[:, c], k_ref[0, :, c]) for c in cols]
    e = [jnp.exp(s - jnp.max(s, axis=-1, keepdims=True)) for s in sc]
    o = jnp.concatenate([_bdot(ee, v_ref[0, :, c]) / jnp.sum(ee, axis=-1, keepdims=True)
                         for ee, c in zip(e, cols)], axis=-1)
    out_ref[0] = x1 + _bdot(o, wxo_ref[...])


def _mix(x, y, g, bonus, ob, kmem, vmem, lnw, nm, bg, wgate, wpa, wpb, wom, nx, wq, wxo, tile):
    b, s, d = x.shape
    n_mem = kmem.shape[1]
    const = _const_spec
    tok = lambda w: pl.BlockSpec((1, tile, w), lambda i, j: (i, j, 0))
    memspec = pl.BlockSpec((1, n_mem, d), lambda i, j: (i, 0, 0))
    return pl.pallas_call(
        _mix_kernel,
        grid=(b, s // tile),
        in_specs=[tok(d), tok(D_RWKV), tok(D_RWKV), tok(D_RWKV), tok(D_CONV), memspec, memspec,
                  const(1, D_RWKV), const(1, d), const(1, 2 * d), const(d, 2 * d),
                  const(D_RWKV, d), const(D_CONV, d), const(d, d), const(1, d), const(d, d),
                  const(d, d)],
        out_specs=tok(d),
        out_shape=jax.ShapeDtypeStruct((b, s, d), F32),
        compiler_params=pltpu.CompilerParams(dimension_semantics=("arbitrary", "arbitrary"),
                                             vmem_limit_bytes=VMEM_LIMIT["mix"]),
        name="mix",
    )(x, y, g, bonus, ob, kmem, vmem, lnw, nm, bg, wgate, wpa, wpb, wom, nx, wq, wxo)


def _mlp_kernel(x_ref, nm_ref, wup_ref, wdown_ref, nf_ref, out_ref):
    x = x_ref[...]
    h = _rms(x, nm_ref[...]).astype(BF16)
    acc = x
    for c in range(D_FF // D_MODEL):
        cols = slice(c * D_MODEL, (c + 1) * D_MODEL)
        up = jnp.maximum(jnp.dot(h, wup_ref[:, cols], preferred_element_type=F32), 0.0)
        acc = acc + _bdot(up * up, wdown_ref[cols, :])
    out_ref[...] = _rms(acc, nf_ref[...])


def _mlp(x, nm, wup, wdown, nf, tile):
    n, d = x.shape
    const = _const_spec
    tok = pl.BlockSpec((tile, d), lambda i: (i, 0))
    return pl.pallas_call(
        _mlp_kernel,
        grid=(n // tile,),
        in_specs=[tok, const(1, d), const(d, D_FF), const(D_FF, d), const(1, d)],
        out_specs=tok,
        out_shape=jax.ShapeDtypeStruct((n, d), F32),
        compiler_params=pltpu.CompilerParams(dimension_semantics=("arbitrary",),
                                             vmem_limit_bytes=VMEM_LIMIT["mlp"]),
        name="mlp",
    )(x, nm, wup, wdown, nf)


def _layer(x, mem, norm_mix, w_in, b_gate, mu_shift, w0, w_lora_w, a0, w_lora_a, w_lora_g,
           k_k, k_a, r_k, ln_x_w, ln_x_b, conv_w, w_proj_a, w_proj_b, w_out_mix, norm_xattn,
           norm_mem, w_q, w_kv, w_xo, norm_mlp, w_up, w_down, norm_final):
    bsz, s, d = x.shape
    c_conv = RW_COLS - 2 * LANES + GATE_LORA
    c_gate = c_conv + 3 * D_CONV
    w_in = w_in.astype(BF16)
    wconv = w_in[:, c_conv:c_gate]
    wgate = w_in[:, c_gate:]
    small = (mu_shift[None], w0[None], a0[None], k_k[None], k_a[None], r_k, ln_x_b[None], conv_w,
             w_lora_w, w_lora_a, w_lora_g)
    step = jnp.arange(CHUNK)
    tri = (step[None, :] <= step[:, None]).astype(BF16)

    rt, at, bt, kt, v, wc, g, bonus, ob = _inproj(
        x, norm_mix[None], w_in, wconv, small, tri, tile=TILE_INPROJ)
    later = (w_proj_a, w_proj_b, w_out_mix, w_q, w_xo, w_up, w_down, w_kv)
    y, (wpa, wpb, wom, wq, wxo, wup, wdown, wkv) = _wkv(
        rt, at, bt, kt, v, wc, later, tile=TILE_WKV)
    kmem, vmem = _memkv(mem, norm_mem[None], wkv)
    x = _mix(x, y, g, bonus, ob, kmem, vmem, ln_x_w[None], norm_mix[None], b_gate[None], wgate,
             wpa, wpb, wom, norm_xattn[None], wq, wxo, tile=TILE_MIX)
    return _mlp(x.reshape(bsz * s, d), norm_mlp[None], wup, wdown, norm_final[None],
                tile=TILE_MLP).reshape(bsz, s, d)


def kernel(x, mem, norm_mix, w_in, b_gate, mu_shift, w0, w_lora_w, a0, w_lora_a, w_lora_g, k_k, k_a, r_k, ln_x_w, ln_x_b, conv_w, w_proj_a, w_proj_b, w_out_mix, norm_xattn, norm_mem, w_q, w_kv, w_xo, norm_mlp, w_up, w_down, norm_final):
    assert w_in.shape[0] == 1, "the MLP kernel fuses the final norm: single-layer trunk only"
    per_layer = (norm_mix, w_in, b_gate, mu_shift, w0, w_lora_w, a0, w_lora_a, w_lora_g, k_k, k_a,
                 r_k, ln_x_w, ln_x_b, conv_w, w_proj_a, w_proj_b, w_out_mix, norm_xattn, norm_mem,
                 w_q, w_kv, w_xo, norm_mlp, w_up, w_down)
    return _layer(x, mem, *(p[0] for p in per_layer), norm_final)
```
